```python
import math
import jax, jax.numpy as jnp
from jax import lax
import numpy as np

D_MODEL = 1024
BATCH = 8
SEQ = 2048
DEPTH = 1
DEC_BATCH = 128
DEC_SEQ = 8
PAST_LEN = 16384
PAGE_SIZE = 128

S5_WIDTH = D_MODEL // 2
S5_GROUP = 16
S5_GROUPS = S5_WIDTH // S5_GROUP
S5_STATE = 64
CM_WIDTH = D_MODEL - S5_WIDTH
CM_HEADS = 8
CM_HEAD_DIM = CM_WIDTH // CM_HEADS
CHUNK = 128
IN_WIDTH = S5_WIDTH + 2 * CM_WIDTH
MIX_WIDTH = S5_WIDTH + CM_WIDTH
D_FF = -(-8 * D_MODEL // (3 * 256)) * 256
EPS = 1e-6
DT_MIN = 1e-3
DT_MAX = 1e-1

kernel_name = "hymba_s5_chunkgmlp_decoder_step"


def rmsnorm(x, g):
    xf = x.astype(jnp.float32)
    r = lax.rsqrt(jnp.mean(xf * xf, axis=-1, keepdims=True) + EPS)
    return (xf * r * g.astype(jnp.float32)).astype(x.dtype)


def layernorm(x, g, b):
    xf = x.astype(jnp.float32)
    mu = jnp.mean(xf, axis=-1, keepdims=True)
    xc = xf - mu
    r = lax.rsqrt(jnp.mean(xc * xc, axis=-1, keepdims=True) + EPS)
    return (xc * r * g.astype(jnp.float32) + b.astype(jnp.float32)).astype(x.dtype)


def _complex_affine_combine(e1, e2):
    a1r, a1i, b1r, b1i = e1
    a2r, a2i, b2r, b2i = e2
    ar = a2r * a1r - a2i * a1i
    ai = a2r * a1i + a2i * a1r
    br = a2r * b1r - a2i * b1i + b2r
    bi = a2r * b1i + a2i * b1r + b2i
    return (ar, ai, br, bi)


def s5_mixer(xs, h0_re, h0_im, lam_re, lam_im, log_dt, b_re, b_im, c_re, c_im,
             d_skip, w_glu, b_glu):
    f32 = jnp.float32
    Bn, T, _ = xs.shape
    xf = xs.astype(f32)
    u = xf.reshape(Bn, T, S5_GROUPS, S5_GROUP)
    dt = jnp.exp(log_dt.astype(f32))[:, None]
    lr = lam_re.astype(f32)
    li = lam_im.astype(f32)
    mag = jnp.exp(lr * dt)
    ar = mag * jnp.cos(li * dt)
    ai = mag * jnp.sin(li * dt)
    den = lr * lr + li * li
    qr = ((ar - 1.0) * lr + ai * li) / den
    qi = (ai * lr - (ar - 1.0) * li) / den
    br_, bi_ = b_re.astype(f32), b_im.astype(f32)
    bbr = qr[..., None] * br_ - qi[..., None] * bi_
    bbi = qr[..., None] * bi_ + qi[..., None] * br_
    bu_r = jnp.einsum('btgi,gni->btgn', u, bbr)
    bu_i = jnp.einsum('btgi,gni->btgn', u, bbi)
    h0r = h0_re.astype(f32)
    h0i = h0_im.astype(f32)
    bu_r = bu_r.at[:, 0].add(ar * h0r - ai * h0i)
    bu_i = bu_i.at[:, 0].add(ar * h0i + ai * h0r)
    a_r = jnp.broadcast_to(ar, bu_r.shape)
    a_i = jnp.broadcast_to(ai, bu_i.shape)
    _, _, hr, hi = lax.associative_scan(_complex_affine_combine, (a_r, a_i, bu_r, bu_i), axis=1)
    y = (jnp.einsum('btgn,gon->btgo', hr, c_re.astype(f32))
         - jnp.einsum('btgn,gon->btgo', hi, c_im.astype(f32)))
    y = y.reshape(Bn, T, S5_WIDTH) + d_skip.astype(f32) * xf
    g = jax.nn.gelu(y)
    out = g * jax.nn.sigmoid(g @ w_glu.astype(f32) + b_glu.astype(f32))
    return out.astype(xs.dtype), hr[:, -1], hi[:, -1]


def chunk_mixer(uv, ln_g, ln_b, w_s, b_s):
    Bn, T, _ = uv.shape
    uv = jax.nn.gelu(uv)
    u = uv[..., :CM_WIDTH]
    v = layernorm(uv[..., CM_WIDTH:], ln_g, ln_b)
    Lc = T if T < CHUNK else CHUNK
    n_chunks = -(-T // Lc)
    Tp = n_chunks * Lc
    vp = jnp.pad(v, ((0, 0), (0, Tp - T), (0, 0)))
    vc = vp.reshape(Bn, n_chunks, Lc, CM_HEADS, CM_HEAD_DIM)
    mask = jnp.tril(jnp.ones((Lc, Lc), dtype=bool))
    w = jnp.where(mask[None], w_s[:, :Lc, :Lc], jnp.zeros((), w_s.dtype))
    mixed = jnp.einsum('hts,bcshd->bcthd', w, vc)
    mixed = mixed + jnp.transpose(b_s[:, :Lc])[None, None, :, :, None]
    mixed = mixed.reshape(Bn, Tp, CM_WIDTH)[:, :T]
    out = u * mixed
    start = ((T - 1) // CHUNK) * CHUNK
    return out, v[:, start:]


def swiglu(x, w_gate, w_up, w_down):
    return (jax.nn.silu(x @ w_gate) * (x @ w_up)) @ w_down


def setup_inputs(seed: int = 0) -> dict:
    key = jax.random.key(seed)
    ks = jax.random.split(key, 32)
    f32 = jnp.float32
    nrm = lambda k, shape, s: jax.random.normal(k, shape, f32) * s
    x_prompt = nrm(ks[0], (BATCH, SEQ, D_MODEL), 1.0)
    x_sample = nrm(ks[1], (DEC_BATCH, DEC_SEQ, D_MODEL), 1.0)
    state_s5_re = nrm(ks[2], (DEPTH, DEC_BATCH, S5_GROUPS, S5_STATE), 0.1)
    state_s5_im = nrm(ks[3], (DEPTH, DEC_BATCH, S5_GROUPS, S5_STATE), 0.1)
    norm1 = 1.0 + nrm(ks[4], (DEPTH, D_MODEL), 0.02)
    w_in = nrm(ks[5], (DEPTH, D_MODEL, IN_WIDTH), D_MODEL ** -0.5)
    lam_re = -0.5 * jnp.exp(nrm(ks[6], (DEPTH, S5_GROUPS, S5_STATE), 0.05))
    lam_im = jnp.broadcast_to(math.pi * jnp.arange(S5_STATE, dtype=f32),
                              (DEPTH, S5_GROUPS, S5_STATE)) + nrm(ks[7], (DEPTH, S5_GROUPS, S5_STATE), 0.01)
    log_dt = jax.random.uniform(ks[8], (DEPTH, S5_GROUPS), f32,
                                math.log(DT_MIN), math.log(DT_MAX))
    b_re = nrm(ks[9], (DEPTH, S5_GROUPS, S5_STATE, S5_GROUP), (2 * S5_GROUP) ** -0.5)
    b_im = nrm(ks[10], (DEPTH, S5_GROUPS, S5_STATE, S5_GROUP), (2 * S5_GROUP) ** -0.5)
    c_re = nrm(ks[11], (DEPTH, S5_GROUPS, S5_GROUP, S5_STATE), (2 * S5_STATE) ** -0.5)
    c_im = nrm(ks[12], (DEPTH, S5_GROUPS, S5_GROUP, S5_STATE), (2 * S5_STATE) ** -0.5)
    d_skip = nrm(ks[13], (DEPTH, S5_WIDTH), 1.0)
    w_glu = nrm(ks[14], (DEPTH, S5_WIDTH, S5_WIDTH), S5_WIDTH ** -0.5)
    b_glu = nrm(ks[15], (DEPTH, S5_WIDTH), 0.01)
    cm_ln_g = 1.0 + nrm(ks[16], (DEPTH, CM_WIDTH), 0.02)
    cm_ln_b = nrm(ks[17], (DEPTH, CM_WIDTH), 0.01)
    w_s = nrm(ks[18], (DEPTH, CM_HEADS, CHUNK, CHUNK), CHUNK ** -0.5)
    b_s = 1.0 + nrm(ks[19], (DEPTH, CM_HEADS, CHUNK), 0.01)
    g_s5 = 1.0 + nrm(ks[20], (DEPTH, S5_WIDTH), 0.02)
    g_cm = 1.0 + nrm(ks[21], (DEPTH, CM_WIDTH), 0.02)
    w_out = nrm(ks[22], (DEPTH, MIX_WIDTH, D_MODEL), MIX_WIDTH ** -0.5)
    norm2 = 1.0 + nrm(ks[23], (DEPTH, D_MODEL), 0.02)
    w_gate = nrm(ks[24], (DEPTH, D_MODEL, D_FF), D_MODEL ** -0.5)
    w_up = nrm(ks[25], (DEPTH, D_MODEL, D_FF), D_MODEL ** -0.5)
    w_down = nrm(ks[26], (DEPTH, D_FF, D_MODEL), D_FF ** -0.5)
    norm_f = 1.0 + nrm(ks[27], (D_MODEL,), 0.02)
    return {"x_prompt": x_prompt, "x_sample": x_sample,
            "state_s5_re": state_s5_re, "state_s5_im": state_s5_im,
            "norm1": norm1, "w_in": w_in, "lam_re": lam_re, "lam_im": lam_im,
            "log_dt": log_dt, "b_re": b_re, "b_im": b_im, "c_re": c_re, "c_im": c_im,
            "d_skip": d_skip, "w_glu": w_glu, "b_glu": b_glu,
            "cm_ln_g": cm_ln_g, "cm_ln_b": cm_ln_b, "w_s": w_s, "b_s": b_s,
            "g_s5": g_s5, "g_cm": g_cm, "w_out": w_out, "norm2": norm2,
            "w_gate": w_gate, "w_up": w_up, "w_down": w_down, "norm_f": norm_f}


def reference(x_prompt, x_sample, state_s5_re, state_s5_im, norm1, w_in, lam_re, lam_im,
              log_dt, b_re, b_im, c_re, c_im, d_skip, w_glu, b_glu, cm_ln_g, cm_ln_b,
              w_s, b_s, g_s5, g_cm, w_out, norm2, w_gate, w_up, w_down, norm_f):
    def layer(x, h0r, h0i, l):
        h = rmsnorm(x, norm1[l])
        p = h @ w_in[l]
        o_s5, hr, hi = s5_mixer(p[..., :S5_WIDTH], h0r, h0i, lam_re[l], lam_im[l], log_dt[l],
                                b_re[l], b_im[l], c_re[l], c_im[l], d_skip[l], w_glu[l], b_glu[l])
        o_cm, v_rows = chunk_mixer(p[..., S5_WIDTH:], cm_ln_g[l], cm_ln_b[l], w_s[l], b_s[l])
        mix = jnp.concatenate([rmsnorm(o_s5, g_s5[l]), rmsnorm(o_cm, g_cm[l])], axis=-1)
        x = x + mix @ w_out[l]
        x = x + swiglu(rmsnorm(x, norm2[l]), w_gate[l], w_up[l], w_down[l])
        return x, hr, hi, v_rows

    xp, xs = x_prompt, x_sample
    zeros_state = jnp.zeros((BATCH, S5_GROUPS, S5_STATE), jnp.float32)
    pr, pi_, pv, sr, si, sv = [], [], [], [], [], []
    for l in range(DEPTH):
        xp, hr, hi, vr = layer(xp, zeros_state, zeros_state, l)
        pr.append(hr); pi_.append(hi); pv.append(vr)
        xs, hr, hi, vr = layer(xs, state_s5_re[l], state_s5_im[l], l)
        sr.append(hr); si.append(hi); sv.append(vr)
    y_prompt = rmsnorm(xp, norm_f)
    y_sample = rmsnorm(xs, norm_f)
    return (y_prompt, y_sample, jnp.stack(pr), jnp.stack(pi_), jnp.stack(pv),
            jnp.stack(sr), jnp.stack(si), jnp.stack(sv))
```

```python
import functools
import math

import jax
import jax.numpy as jnp
from jax import lax
from jax.experimental import pallas as pl
from jax.experimental.pallas import tpu as pltpu

F32 = jnp.float32
BF16 = jnp.bfloat16

D_MODEL = 1024
S5_WIDTH = 512
S5_GROUP = 16
S5_GROUPS = 32
S5_STATE = 64
N_STATES = S5_GROUPS * S5_STATE
CM_WIDTH = 512
CM_HEADS = 8
CM_HEAD_DIM = 64
CHUNK = 128
IN_WIDTH = S5_WIDTH + 2 * CM_WIDTH
D_FF = 2816
EPS = 1e-6

S5_HALVES = 2
HALF_IN = S5_WIDTH // S5_HALVES
HALF_STATES = N_STATES // S5_HALVES
SUBLANES = 8
LANES = 128
SCAN_LANES = 512
SCAN_BLOCKS = SCAN_LANES // LANES
FF_CHUNK = 256
VMEM_LIMIT_BYTES = 58 * 1024 * 1024


def _rms(x, g):
    r = lax.rsqrt(jnp.mean(x * x, axis=-1, keepdims=True) + EPS)
    return x * r * g


def _dot(a, b):
    return jnp.dot(a, b, preferred_element_type=F32)


def _store_lane_blocks(ref, val):
    for k in range(ref.shape[0]):
        ref[k] = val[:, k * LANES:(k + 1) * LANES]


def _load_lane_blocks(ref):
    return jnp.concatenate([ref[k] for k in range(ref.shape[0])], axis=1)


def _prep_kernel(lr_ref, li_ref, ldt_ref, br_ref, bi_ref, ar_ref, ai_ref, bbr_ref, bbi_ref):
    dt = jnp.exp(ldt_ref[...])
    lr = lr_ref[...]
    li = li_ref[...]
    mag = jnp.exp(lr * dt)
    ar = mag * jnp.cos(li * dt)
    ai = mag * jnp.sin(li * dt)
    den = lr * lr + li * li
    qr = ((ar - 1.0) * lr + ai * li) / den
    qi = (ai * lr - (ar - 1.0) * li) / den
    br = br_ref[...]
    bi = bi_ref[...]
    ar_ref[...] = ar
    ai_ref[...] = ai
    bbr_ref[...] = qr[:, None, :] * br - qi[:, None, :] * bi
    bbi_ref[...] = qr[:, None, :] * bi + qi[:, None, :] * br


def _mixer_kernel(nb, tt, has_h0, *refs):
    refs = list(refs)
    x_ref = refs.pop(0)
    if has_h0:
        h0r_ref = refs.pop(0)
        h0i_ref = refs.pop(0)
    (g1_ref, win_ref, bdr_ref, bdi_ref, cdr_ref, cdi_ref, ar_ref, ai_ref, dskip_ref,
     wglu_ref, bglu_ref, lng_ref, lnb_ref, wmix_ref, bmix_ref, gs5_ref, gcm_ref, wout_ref,
     x1_ref, hro_ref, hio_ref, vo_ref,
     bur_ref, bui_ref, hsr_ref, hsi_ref) = refs[:26]
    extra = refs[26:]
    rows = nb * tt
    step = pl.program_id(0)

    @pl.when(step == 0)
    def _():
        if has_h0:
            hsr_ref[...] = h0r_ref[...]
            hsi_ref[...] = h0i_ref[...]
        else:
            hsr_ref[...] = jnp.zeros_like(hsr_ref)
            hsi_ref[...] = jnp.zeros_like(hsi_ref)

    x = x_ref[...].reshape(rows, D_MODEL)
    h = _rms(x, g1_ref[...])
    p = _dot(h.astype(BF16), win_ref[...])
    xs = p[:, :S5_WIDTH]

    y_halves = []
    for hf in range(S5_HALVES):
        xs_h = xs[:, hf * HALF_IN:(hf + 1) * HALF_IN].astype(BF16)
        _store_lane_blocks(bur_ref, _dot(xs_h, bdr_ref[hf]))
        _store_lane_blocks(bui_ref, _dot(xs_h, bdi_ref[hf]))
        for sg in range(nb // SUBLANES):
            srows = slice(sg * SUBLANES, (sg + 1) * SUBLANES)
            for lc in range(HALF_STATES // SCAN_LANES):
                blocks = range(lc * SCAN_BLOCKS, (lc + 1) * SCAN_BLOCKS)
                glanes = slice(hf * HALF_STATES + lc * SCAN_LANES,
                               hf * HALF_STATES + (lc + 1) * SCAN_LANES)
                a_r = jnp.broadcast_to(ar_ref[:, glanes], (SUBLANES, SCAN_LANES))
                a_i = jnp.broadcast_to(ai_ref[:, glanes], (SUBLANES, SCAN_LANES))

                def scan_step(t, carry, sg=sg, blocks=blocks, a_r=a_r, a_i=a_i):
                    hr, hi = carry
                    trows = pl.ds(sg * SUBLANES * tt + t, SUBLANES, stride=tt)
                    b_r = jnp.concatenate([bur_ref[k, trows, :] for k in blocks], axis=1)
                    b_i = jnp.concatenate([bui_ref[k, trows, :] for k in blocks], axis=1)
                    nr = a_r * hr - a_i * hi + b_r
                    ni = a_r * hi + a_i * hr + b_i
                    for n, k in enumerate(blocks):
                        bur_ref[k, trows, :] = nr[:, n * LANES:(n + 1) * LANES]
                        bui_ref[k, trows, :] = ni[:, n * LANES:(n + 1) * LANES]
                    return nr, ni

                carry = (hsr_ref[srows, glanes], hsi_ref[srows, glanes])
                if tt <= SUBLANES:
                    for t in range(tt):
                        carry = scan_step(t, carry)
                else:
                    carry = lax.fori_loop(0, tt, scan_step, carry, unroll=8)
                hsr_ref[srows, glanes] = carry[0]
                hsi_ref[srows, glanes] = carry[1]
        y_halves.append(_dot(_load_lane_blocks(bur_ref).astype(BF16), cdr_ref[hf])
                        - _dot(_load_lane_blocks(bui_ref).astype(BF16), cdi_ref[hf]))
    hro_ref[...] = hsr_ref[...]
    hio_ref[...] = hsi_ref[...]
    y = jnp.concatenate(y_halves, axis=1) + dskip_ref[...] * xs
    g = jax.nn.gelu(y)
    o_s5 = g * jax.nn.sigmoid(_dot(g.astype(BF16), wglu_ref[...]) + bglu_ref[...])

    uv = jax.nn.gelu(p[:, S5_WIDTH:])
    u = uv[:, :CM_WIDTH]
    vv = uv[:, CM_WIDTH:]
    mu = jnp.mean(vv, axis=-1, keepdims=True)
    vc = vv - mu
    v = vc * lax.rsqrt(jnp.mean(vc * vc, axis=-1, keepdims=True) + EPS) * lng_ref[...] + lnb_ref[...]
    vo_ref[...] = v.reshape(nb, tt, CM_WIDTH)

    if tt == CHUNK:
        r_idx = lax.broadcasted_iota(jnp.int32, (2 * CHUNK, CHUNK), 0)
        c_idx = lax.broadcasted_iota(jnp.int32, (2 * CHUNK, CHUNK), 1)
        causal = (r_idx % CHUNK) >= c_idx
        lane = lax.broadcasted_iota(jnp.int32, (CHUNK, 2 * CM_HEAD_DIM), 1)
        first_head = lane < CM_HEAD_DIM
        w_pairs = [jnp.where(causal, wmix_ref[j], 0.0).astype(BF16) for j in range(CM_HEADS // 2)]
        v_bf = v.astype(BF16)
        seq_blocks = []
        for b in range(nb):
            pair_blocks = []
            for j in range(CM_HEADS // 2):
                vb = v_bf[b * CHUNK:(b + 1) * CHUNK, j * 128:(j + 1) * 128]
                r = _dot(w_pairs[j], vb)
                pair_blocks.append(jnp.where(first_head, r[:CHUNK], r[CHUNK:]))
            seq_blocks.append(jnp.concatenate(pair_blocks, axis=1))
        mixed = jnp.concatenate(seq_blocks, axis=0)
        mixed = (mixed.reshape(nb, tt, CM_WIDTH) + bmix_ref[...][None]).reshape(rows, CM_WIDTH)
        o_cm = u * mixed
    else:
        us_ref, vs_ref, oc_ref = extra
        _store_lane_blocks(us_ref, u)
        _store_lane_blocks(vs_ref, v)
        for k in range(CM_WIDTH // LANES):
            lanes = slice(k * LANES, (k + 1) * LANES)
            v_at = [vs_ref[k, pl.ds(s, nb, stride=tt), :] for s in range(tt)]
            for t in range(tt):
                acc = bmix_ref[t:t + 1, lanes] + wmix_ref[t, 0:1, lanes] * v_at[0]
                for s in range(1, t + 1):
                    acc = acc + wmix_ref[t, s:s + 1, lanes] * v_at[s]
                trows = pl.ds(t, nb, stride=tt)
                oc_ref[k, trows, :] = us_ref[k, trows, :] * acc
        o_cm = _load_lane_blocks(oc_ref)

    n_s5 = _rms(o_s5, gs5_ref[...]).astype(BF16)
    n_cm = _rms(o_cm, gcm_ref[...]).astype(BF16)
    x1 = x + _dot(n_s5, wout_ref[:S5_WIDTH, :]) + _dot(n_cm, wout_ref[S5_WIDTH:, :])
    x1_ref[...] = x1.reshape(x1_ref.shape)


def _const_spec(shape):
    nd = len(shape)
    return pl.BlockSpec(shape, lambda i, nd=nd: (0,) * nd, pipeline_mode=pl.Buffered(1))


def _mixer(x3, h0, weights, *, nb, tt, name):
    n_seq, T, _ = x3.shape
    assert n_seq == nb and T % tt == 0
    steps = T // tt
    rows = nb * tt
    has_h0 = h0 is not None
    short = tt != CHUNK

    in_specs = [pl.BlockSpec((nb, tt, D_MODEL), lambda i: (0, i, 0))]
    args = [x3]
    if has_h0:
        in_specs += [_const_spec((nb, N_STATES))] * 2
        args += list(h0)
    for w in weights:
        in_specs.append(_const_spec(w.shape))
        args.append(w)

    out_shape = (
        jax.ShapeDtypeStruct((nb, T, D_MODEL), F32),
        jax.ShapeDtypeStruct((nb, N_STATES), F32),
        jax.ShapeDtypeStruct((nb, N_STATES), F32),
        jax.ShapeDtypeStruct((nb, tt, CM_WIDTH), F32),
    )
    out_specs = (
        pl.BlockSpec((nb, tt, D_MODEL), lambda i: (0, i, 0)),
        pl.BlockSpec((nb, N_STATES), lambda i: (0, 0)),
        pl.BlockSpec((nb, N_STATES), lambda i: (0, 0)),
        pl.BlockSpec((nb, tt, CM_WIDTH), lambda i: (0, 0, 0)),
    )
    scratch = [
        pltpu.VMEM((HALF_STATES // LANES, rows, LANES), F32),
        pltpu.VMEM((HALF_STATES // LANES, rows, LANES), F32),
        pltpu.VMEM((nb, N_STATES), F32),
        pltpu.VMEM((nb, N_STATES), F32),
    ]
    if short:
        scratch += [pltpu.VMEM((CM_WIDTH // LANES, rows, LANES), F32)] * 3
    return pl.pallas_call(
        functools.partial(_mixer_kernel, nb, tt, has_h0),
        grid=(steps,),
        in_specs=in_specs,
        out_specs=out_specs,
        out_shape=out_shape,
        scratch_shapes=scratch,
        compiler_params=pltpu.CompilerParams(
            dimension_semantics=("arbitrary",), vmem_limit_bytes=VMEM_LIMIT_BYTES),
        name=name,
    )(*args)


def _ffn_kernel(x_ref, g2_ref, wg_ref, wu_ref, wd_ref, gf_ref, y_ref):
    x = x_ref[...]
    h = _rms(x, g2_ref[...]).astype(BF16)
    acc = x
    for c in range(D_FF // FF_CHUNK):
        cols = slice(c * FF_CHUNK, (c + 1) * FF_CHUNK)
        gate = _dot(h, wg_ref[:, cols])
        up = _dot(h, wu_ref[:, cols])
        acc = acc + _dot((jax.nn.silu(gate) * up).astype(BF16), wd_ref[cols, :])
    y_ref[...] = _rms(acc, gf_ref[...])


def _ffn(x2, g2, wg, wu, wd, gf, *, tile, name):
    n = x2.shape[0]
    assert n % tile == 0
    return pl.pallas_call(
        _ffn_kernel,
        grid=(n // tile,),
        in_specs=[
            pl.BlockSpec((tile, D_MODEL), lambda i: (i, 0)),
            _const_spec(g2.shape), _const_spec(wg.shape), _const_spec(wu.shape),
            _const_spec(wd.shape), _const_spec(gf.shape),
        ],
        out_specs=pl.BlockSpec((tile, D_MODEL), lambda i: (i, 0)),
        out_shape=jax.ShapeDtypeStruct((n, D_MODEL), F32),
        compiler_params=pltpu.CompilerParams(
            dimension_semantics=("arbitrary",), vmem_limit_bytes=VMEM_LIMIT_BYTES),
        name=name,
    )(x2, g2, wg, wu, wd, gf)


def _block_diag(blocks):
    hv, gl, r, c = blocks.shape
    eye = jnp.eye(gl, dtype=blocks.dtype)
    out = blocks[:, :, :, None, :] * eye[None, :, None, :, None]
    return out.reshape(hv, gl * r, gl * c)


def kernel(x_prompt, x_sample, state_s5_re, state_s5_im, norm1, w_in, lam_re, lam_im, log_dt, b_re, b_im, c_re, c_im, d_skip, w_glu, b_glu, cm_ln_g, cm_ln_b, w_s, b_s, g_s5, g_cm, w_out, norm2, w_gate, w_up, w_down, norm_f):
    depth = norm1.shape[0]
    assert depth == 1
    l = 0
    batch, seq, _ = x_prompt.shape
    dec_batch, dec_seq, _ = x_sample.shape
    gph = S5_GROUPS // S5_HALVES

    ar, ai, bbr, bbi = pl.pallas_call(
        _prep_kernel,
        out_shape=(
            jax.ShapeDtypeStruct((S5_GROUPS, S5_STATE), F32),
            jax.ShapeDtypeStruct((S5_GROUPS, S5_STATE), F32),
            jax.ShapeDtypeStruct((S5_GROUPS, S5_GROUP, S5_STATE), F32),
            jax.ShapeDtypeStruct((S5_GROUPS, S5_GROUP, S5_STATE), F32),
        ),
        name="s5_discretise",
    )(lam_re[l], lam_im[l], log_dt[l][:, None],
      jnp.swapaxes(b_re[l], 1, 2), jnp.swapaxes(b_im[l], 1, 2))
    bdr = _block_diag(bbr.reshape(S5_HALVES, gph, S5_GROUP, S5_STATE)).astype(BF16)
    bdi = _block_diag(bbi.reshape(S5_HALVES, gph, S5_GROUP, S5_STATE)).astype(BF16)
    cdr = _block_diag(jnp.swapaxes(c_re[l], 1, 2).reshape(S5_HALVES, gph, S5_STATE, S5_GROUP)).astype(BF16)
    cdi = _block_diag(jnp.swapaxes(c_im[l], 1, 2).reshape(S5_HALVES, gph, S5_STATE, S5_GROUP)).astype(BF16)

    row = lambda a: a.reshape(1, -1)
    common_head = [row(norm1[l]), w_in[l].astype(BF16), bdr, bdi, cdr, cdi,
                   row(ar), row(ai), row(d_skip[l]), w_glu[l].astype(BF16), row(b_glu[l]),
                   row(cm_ln_g[l]), row(cm_ln_b[l])]
    common_tail = [row(g_s5[l]), row(g_cm[l]), w_out[l].astype(BF16)]

    wmix_p = w_s[l].reshape(CM_HEADS // 2, 2 * CHUNK, CHUNK)
    bmix_p = jnp.repeat(b_s[l].T, CM_HEAD_DIM, axis=1)
    xp1, pr, pi_, pv = _mixer(x_prompt, None, common_head + [wmix_p, bmix_p] + common_tail,
                              nb=batch, tt=CHUNK, name="mixer_prompt")

    wmix_s = jnp.repeat(jnp.transpose(w_s[l][:, :dec_seq, :dec_seq], (1, 2, 0)), CM_HEAD_DIM, axis=2)
    bmix_s = jnp.repeat(b_s[l][:, :dec_seq].T, CM_HEAD_DIM, axis=1)
    h0 = (state_s5_re[l].reshape(dec_batch, N_STATES), state_s5_im[l].reshape(dec_batch, N_STATES))
    xs1, sr, si, sv = _mixer(x_sample, h0, common_head + [wmix_s, bmix_s] + common_tail,
                             nb=dec_batch, tt=dec_seq, name="mixer_sample")

    ffn_w = (row(norm2[l]), w_gate[l].astype(BF16), w_up[l].astype(BF16),
             w_down[l].astype(BF16), row(norm_f))
    y_prompt = _ffn(xp1.reshape(batch * seq, D_MODEL), *ffn_w, tile=512, name="ffn_prompt")
    y_sample = _ffn(xs1.reshape(dec_batch * dec_seq, D_MODEL), *ffn_w, tile=512, name="ffn_sample")

    st = lambda a, n: a.reshape(1, n, S5_GROUPS, S5_STATE)
    return (y_prompt.reshape(batch, seq, D_MODEL), y_sample.reshape(dec_batch, dec_seq, D_MODEL),
            st(pr, batch), st(pi_, batch), pv[None],
            st(sr, dec_batch), st(si, dec_batch), sv[None])
```

```python
import functools

import jax
import jax.numpy as jnp
from jax import lax
from jax.experimental import pallas as pl
from jax.experimental.pallas import tpu as pltpu

F32 = jnp.float32
BF16 = jnp.bfloat16

D_MODEL = 1024
S5_WIDTH = 512
S5_GROUP = 16
S5_GROUPS = 32
S5_STATE = 64
N_STATES = S5_GROUPS * S5_STATE
CM_WIDTH = 512
CM_HEADS = 8
CM_HEAD_DIM = 64
CHUNK = 128
IN_WIDTH = S5_WIDTH + 2 * CM_WIDTH
D_FF = 2816
EPS = 1e-6

S5_HALVES = 2
HALF_IN = S5_WIDTH // S5_HALVES
HALF_STATES = N_STATES // S5_HALVES
SUBLANES = 8
LANES = 128
SCAN_LANES = 512
SCAN_BLOCKS = SCAN_LANES // LANES
FF_CHUNK = 256
FFN_TILE = 512
VMEM_LIMIT_BYTES = 58 * 1024 * 1024


def _rms(x, g):
    r = lax.rsqrt(jnp.mean(x * x, axis=-1, keepdims=True) + EPS)
    return x * r * g


def _dot(a, b):
    return jnp.dot(a, b, preferred_element_type=F32)


def _store_lane_blocks(ref, val):
    for k in range(ref.shape[0]):
        ref[k] = val[:, k * LANES:(k + 1) * LANES]


def _load_lane_blocks(ref):
    return jnp.concatenate([ref[k] for k in range(ref.shape[0])], axis=1)


def _prep_kernel(lr_ref, li_ref, ldt_ref, br_ref, bi_ref, ar_ref, ai_ref, bbr_ref, bbi_ref):
    dt = jnp.exp(ldt_ref[...])
    lr = lr_ref[...]
    li = li_ref[...]
    mag = jnp.exp(lr * dt)
    ar = mag * jnp.cos(li * dt)
    ai = mag * jnp.sin(li * dt)
    den = lr * lr + li * li
    qr = ((ar - 1.0) * lr + ai * li) / den
    qi = (ai * lr - (ar - 1.0) * li) / den
    br = br_ref[...]
    bi = bi_ref[...]
    ar_ref[...] = ar
    ai_ref[...] = ai
    bbr_ref[...] = qr[:, None, :] * br - qi[:, None, :] * bi
    bbi_ref[...] = qr[:, None, :] * bi + qi[:, None, :] * br


def _s5_gate(y_halves, xs, dskip_ref, wglu_ref, bglu_ref):
    y = jnp.concatenate(y_halves, axis=1) + dskip_ref[...] * xs
    g = jax.nn.gelu(y)
    return g * jax.nn.sigmoid(_dot(g.astype(BF16), wglu_ref[...]) + bglu_ref[...])


def _cm_uv(p, lng_ref, lnb_ref):
    uv = jax.nn.gelu(p[:, S5_WIDTH:])
    u = uv[:, :CM_WIDTH]
    vv = uv[:, CM_WIDTH:]
    mu = jnp.mean(vv, axis=-1, keepdims=True)
    vc = vv - mu
    v = vc * lax.rsqrt(jnp.mean(vc * vc, axis=-1, keepdims=True) + EPS) * lng_ref[...] + lnb_ref[...]
    return u, v


def _out_proj(x, o_s5, o_cm, gs5_ref, gcm_ref, wout_ref):
    n_s5 = _rms(o_s5, gs5_ref[...]).astype(BF16)
    n_cm = _rms(o_cm, gcm_ref[...]).astype(BF16)
    return x + _dot(n_s5, wout_ref[:S5_WIDTH, :]) + _dot(n_cm, wout_ref[S5_WIDTH:, :])


def _scan_update(a_r, a_i, hr, hi, b_r, b_i):
    return a_r * hr - a_i * hi + b_r, a_r * hi + a_i * hr + b_i


def _mixer_prompt_kernel(nb, x_hbm, g1_ref, win_ref, bdr_ref, bdi_ref, cdr_ref, cdi_ref, ar_ref,
                         ai_ref, dskip_ref, wglu_ref, bglu_ref, lng_ref, lnb_ref, wmix_ref,
                         bmix_ref, gs5_ref, gcm_ref, wout_ref,
                         x1_ref, hro_ref, hio_ref, vo_ref,
                         xt_ref, in_sem, bur_ref, bui_ref, hsr_ref, hsi_ref, vs_ref, mx_ref):
    tt = CHUNK
    rows = nb * tt
    step = pl.program_id(0)
    n_steps = pl.num_programs(0)
    slot = step % 2

    def fetch(s, sl):
        return [pltpu.make_async_copy(x_hbm.at[b, pl.ds(s * tt, tt), :], xt_ref.at[sl, :, b, :],
                                      in_sem.at[sl, b]) for b in range(nb)]

    @pl.when(step == 0)
    def _():
        for cp in fetch(0, 0):
            cp.start()
        hsr_ref[...] = jnp.zeros_like(hsr_ref)
        hsi_ref[...] = jnp.zeros_like(hsi_ref)

    @pl.when(step + 1 < n_steps)
    def _():
        for cp in fetch(step + 1, 1 - slot):
            cp.start()

    for cp in fetch(step, slot):
        cp.wait()

    x = xt_ref[slot].reshape(rows, D_MODEL)
    p = _dot(_rms(x, g1_ref[...]).astype(BF16), win_ref[...])
    xs = p[:, :S5_WIDTH]

    y_halves = []
    for hf in range(S5_HALVES):
        xs_h = xs[:, hf * HALF_IN:(hf + 1) * HALF_IN].astype(BF16)
        bur_ref[...] = _dot(xs_h, bdr_ref[hf])
        bui_ref[...] = _dot(xs_h, bdi_ref[hf])
        for lc in range(HALF_STATES // SCAN_LANES):
            lanes = slice(lc * SCAN_LANES, (lc + 1) * SCAN_LANES)
            glanes = slice(hf * HALF_STATES + lc * SCAN_LANES, hf * HALF_STATES + (lc + 1) * SCAN_LANES)
            a_r = jnp.broadcast_to(ar_ref[:, glanes], (nb, SCAN_LANES))
            a_i = jnp.broadcast_to(ai_ref[:, glanes], (nb, SCAN_LANES))

            def scan_step(t, carry, lanes=lanes, a_r=a_r, a_i=a_i):
                trows = pl.ds(pl.multiple_of(t * nb, nb), nb)
                nr, ni = _scan_update(a_r, a_i, carry[0], carry[1],
                                      bur_ref[trows, lanes], bui_ref[trows, lanes])
                bur_ref[trows, lanes] = nr
                bui_ref[trows, lanes] = ni
                return nr, ni

            hr, hi = lax.fori_loop(0, tt, scan_step, (hsr_ref[:, glanes], hsi_ref[:, glanes]), unroll=8)
            hsr_ref[:, glanes] = hr
            hsi_ref[:, glanes] = hi
        y_halves.append(_dot(bur_ref[...].astype(BF16), cdr_ref[hf])
                        - _dot(bui_ref[...].astype(BF16), cdi_ref[hf]))
    hro_ref[...] = hsr_ref[...]
    hio_ref[...] = hsi_ref[...]
    o_s5 = _s5_gate(y_halves, xs, dskip_ref, wglu_ref, bglu_ref)

    u, v = _cm_uv(p, lng_ref, lnb_ref)
    _store_lane_blocks(vs_ref, v)
    r_idx = lax.broadcasted_iota(jnp.int32, (2 * CHUNK, CHUNK), 0)
    c_idx = lax.broadcasted_iota(jnp.int32, (2 * CHUNK, CHUNK), 1)
    causal = (r_idx % CHUNK) >= c_idx
    first_head = lax.broadcasted_iota(jnp.int32, (CHUNK, LANES), 1) < CM_HEAD_DIM
    for j in range(CM_HEADS // 2):
        w_pair = jnp.where(causal, wmix_ref[j], 0.0).astype(BF16)
        bias = bmix_ref[:, j * LANES:(j + 1) * LANES]
        for b in range(nb):
            seq_rows = pl.ds(b, tt, stride=nb)
            v_seq = vs_ref[j, seq_rows, :]
            vo_ref[b, :, j * LANES:(j + 1) * LANES] = v_seq
            r = _dot(w_pair, v_seq.astype(BF16))
            mx_ref[j, seq_rows, :] = jnp.where(first_head, r[:CHUNK], r[CHUNK:]) + bias
    o_cm = u * _load_lane_blocks(mx_ref)

    x1_ref[...] = _out_proj(x, o_s5, o_cm, gs5_ref, gcm_ref, wout_ref)


def _mixer_sample_kernel(nb, tt, x_ref, h0r_ref, h0i_ref, g1_ref, win_ref, bdr_ref, bdi_ref, cdr_ref,
                         cdi_ref, ar_ref, ai_ref, dskip_ref, wglu_ref, bglu_ref, lng_ref, lnb_ref,
                         wmix_ref, bmix_ref, gs5_ref, gcm_ref, wout_ref,
                         x1_ref, hro_ref, hio_ref, vo_ref,
                         bur_ref, bui_ref, us_ref, vs_ref, oc_ref):
    rows = nb * tt
    x = x_ref[...].reshape(rows, D_MODEL)
    p = _dot(_rms(x, g1_ref[...]).astype(BF16), win_ref[...])
    xs = p[:, :S5_WIDTH]

    y_halves = []
    for hf in range(S5_HALVES):
        xs_h = xs[:, hf * HALF_IN:(hf + 1) * HALF_IN].astype(BF16)
        _store_lane_blocks(bur_ref, _dot(xs_h, bdr_ref[hf]))
        _store_lane_blocks(bui_ref, _dot(xs_h, bdi_ref[hf]))
        for sg in range(nb // SUBLANES):
            srows = slice(sg * SUBLANES, (sg + 1) * SUBLANES)
            for lc in range(HALF_STATES // SCAN_LANES):
                blocks = range(lc * SCAN_BLOCKS, (lc + 1) * SCAN_BLOCKS)
                glanes = slice(hf * HALF_STATES + lc * SCAN_LANES,
                               hf * HALF_STATES + (lc + 1) * SCAN_LANES)
                a_r = jnp.broadcast_to(ar_ref[:, glanes], (SUBLANES, SCAN_LANES))
                a_i = jnp.broadcast_to(ai_ref[:, glanes], (SUBLANES, SCAN_LANES))
                hr = h0r_ref[srows, glanes]
                hi = h0i_ref[srows, glanes]
                for t in range(tt):
                    trows = pl.ds(sg * SUBLANES * tt + t, SUBLANES, stride=tt)
                    b_r = jnp.concatenate([bur_ref[k, trows, :] for k in blocks], axis=1)
                    b_i = jnp.concatenate([bui_ref[k, trows, :] for k in blocks], axis=1)
                    hr, hi = _scan_update(a_r, a_i, hr, hi, b_r, b_i)
                    for n, k in enumerate(blocks):
                        bur_ref[k, trows, :] = hr[:, n * LANES:(n + 1) * LANES]
                        bui_ref[k, trows, :] = hi[:, n * LANES:(n + 1) * LANES]
                hro_ref[srows, glanes] = hr
                hio_ref[srows, glanes] = hi
        y_halves.append(_dot(_load_lane_blocks(bur_ref).astype(BF16), cdr_ref[hf])
                        - _dot(_load_lane_blocks(bui_ref).astype(BF16), cdi_ref[hf]))
    o_s5 = _s5_gate(y_halves, xs, dskip_ref, wglu_ref, bglu_ref)

    u, v = _cm_uv(p, lng_ref, lnb_ref)
    vo_ref[...] = v.reshape(nb, tt, CM_WIDTH)
    _store_lane_blocks(us_ref, u)
    _store_lane_blocks(vs_ref, v)
    for k in range(CM_WIDTH // LANES):
        lanes = slice(k * LANES, (k + 1) * LANES)
        v_at = [vs_ref[k, pl.ds(s, nb, stride=tt), :] for s in range(tt)]
        for t in range(tt):
            acc = bmix_ref[t:t + 1, lanes] + wmix_ref[t, 0:1, lanes] * v_at[0]
            for s in range(1, t + 1):
                acc = acc + wmix_ref[t, s:s + 1, lanes] * v_at[s]
            trows = pl.ds(t, nb, stride=tt)
            oc_ref[k, trows, :] = us_ref[k, trows, :] * acc
    o_cm = _load_lane_blocks(oc_ref)

    x1_ref[...] = _out_proj(x, o_s5, o_cm, gs5_ref, gcm_ref, wout_ref)


def _const_spec(shape):
    nd = len(shape)
    return pl.BlockSpec(shape, lambda i, nd=nd: (0,) * nd, pipeline_mode=pl.Buffered(1))


def _mixer_prompt(x3, weights):
    nb, T, _ = x3.shape
    assert nb == SUBLANES and T % CHUNK == 0
    rows = nb * CHUNK
    in_specs = [pl.BlockSpec(memory_space=pl.ANY)] + [_const_spec(w.shape) for w in weights]
    out_shape = (
        jax.ShapeDtypeStruct((T * nb, D_MODEL), F32),
        jax.ShapeDtypeStruct((nb, N_STATES), F32),
        jax.ShapeDtypeStruct((nb, N_STATES), F32),
        jax.ShapeDtypeStruct((nb, CHUNK, CM_WIDTH), F32),
    )
    out_specs = (
        pl.BlockSpec((rows, D_MODEL), lambda i: (i, 0)),
        pl.BlockSpec((nb, N_STATES), lambda i: (0, 0)),
        pl.BlockSpec((nb, N_STATES), lambda i: (0, 0)),
        pl.BlockSpec((nb, CHUNK, CM_WIDTH), lambda i: (0, 0, 0)),
    )
    scratch = [
        pltpu.VMEM((2, CHUNK, nb, D_MODEL), F32),
        pltpu.SemaphoreType.DMA((2, nb)),
        pltpu.VMEM((rows, HALF_STATES), F32),
        pltpu.VMEM((rows, HALF_STATES), F32),
        pltpu.VMEM((nb, N_STATES), F32),
        pltpu.VMEM((nb, N_STATES), F32),
        pltpu.VMEM((CM_WIDTH // LANES, rows, LANES), F32),
        pltpu.VMEM((CM_WIDTH // LANES, rows, LANES), F32),
    ]
    return pl.pallas_call(
        functools.partial(_mixer_prompt_kernel, nb),
        grid=(T // CHUNK,),
        in_specs=in_specs,
        out_specs=out_specs,
        out_shape=out_shape,
        scratch_shapes=scratch,
        compiler_params=pltpu.CompilerParams(
            dimension_semantics=("arbitrary",), vmem_limit_bytes=VMEM_LIMIT_BYTES),
        name="mixer_prompt",
    )(x3, *weights)


def _mixer_sample(x3, h0, weights):
    nb, tt, _ = x3.shape
    assert nb % SUBLANES == 0 and tt <= SUBLANES
    rows = nb * tt
    args = [x3, *h0, *weights]
    out_shape = (
        jax.ShapeDtypeStruct((rows, D_MODEL), F32),
        jax.ShapeDtypeStruct((nb, N_STATES), F32),
        jax.ShapeDtypeStruct((nb, N_STATES), F32),
        jax.ShapeDtypeStruct((nb, tt, CM_WIDTH), F32),
    )
    scratch = ([pltpu.VMEM((HALF_STATES // LANES, rows, LANES), F32)] * 2
               + [pltpu.VMEM((CM_WIDTH // LANES, rows, LANES), F32)] * 3)
    return pl.pallas_call(
        functools.partial(_mixer_sample_kernel, nb, tt),
        grid=(1,),
        in_specs=[_const_spec(a.shape) for a in args],
        out_specs=tuple(_const_spec(o.shape) for o in out_shape),
        out_shape=out_shape,
        scratch_shapes=scratch,
        compiler_params=pltpu.CompilerParams(
            dimension_semantics=("arbitrary",), vmem_limit_bytes=VMEM_LIMIT_BYTES),
        name="mixer_sample",
    )(*args)


def _ffn_rows(x, g2_ref, wg_ref, wu_ref, wd_ref, gf_ref):
    h = _rms(x, g2_ref[...]).astype(BF16)
    acc = x
    for c in range(D_FF // FF_CHUNK):
        cols = slice(c * FF_CHUNK, (c + 1) * FF_CHUNK)
        gate = _dot(h, wg_ref[:, cols])
        up = _dot(h, wu_ref[:, cols])
        acc = acc + _dot((jax.nn.silu(gate) * up).astype(BF16), wd_ref[cols, :])
    return _rms(acc, gf_ref[...])


def _ffn_kernel(x_ref, g2_ref, wg_ref, wu_ref, wd_ref, gf_ref, y_ref):
    y_ref[...] = _ffn_rows(x_ref[...], g2_ref, wg_ref, wu_ref, wd_ref, gf_ref)


def _ffn_prompt_kernel(nb, x_ref, g2_ref, wg_ref, wu_ref, wd_ref, gf_ref, y_hbm, yt_ref, out_sem):
    tt = FFN_TILE // nb
    step = pl.program_id(0)
    n_steps = pl.num_programs(0)
    slot = step % 2

    def writeback(s, sl):
        return [pltpu.make_async_copy(yt_ref.at[sl, :, b, :], y_hbm.at[b, pl.ds(s * tt, tt), :],
                                      out_sem.at[sl, b]) for b in range(nb)]

    @pl.when(step >= 2)
    def _():
        for cp in writeback(step - 2, slot):
            cp.wait()

    y = _ffn_rows(x_ref[...], g2_ref, wg_ref, wu_ref, wd_ref, gf_ref)
    yt_ref[slot] = y.reshape(tt, nb, D_MODEL)
    for cp in writeback(step, slot):
        cp.start()

    @pl.when(step == n_steps - 1)
    def _():
        for cp in writeback(step - 1, 1 - slot):
            cp.wait()
        for cp in writeback(step, slot):
            cp.wait()


def _ffn(x2, ffn_w, *, name):
    n = x2.shape[0]
    assert n % FFN_TILE == 0
    return pl.pallas_call(
        _ffn_kernel,
        grid=(n // FFN_TILE,),
        in_specs=[pl.BlockSpec((FFN_TILE, D_MODEL), lambda i: (i, 0))]
        + [_const_spec(w.shape) for w in ffn_w],
        out_specs=pl.BlockSpec((FFN_TILE, D_MODEL), lambda i: (i, 0)),
        out_shape=jax.ShapeDtypeStruct((n, D_MODEL), F32),
        compiler_params=pltpu.CompilerParams(
            dimension_semantics=("arbitrary",), vmem_limit_bytes=VMEM_LIMIT_BYTES),
        name=name,
    )(x2, *ffn_w)


def _ffn_prompt(x2, ffn_w, *, nb):
    n = x2.shape[0]
    assert n % FFN_TILE == 0 and FFN_TILE % nb == 0 and n // FFN_TILE >= 2
    return pl.pallas_call(
        functools.partial(_ffn_prompt_kernel, nb),
        grid=(n // FFN_TILE,),
        in_specs=[pl.BlockSpec((FFN_TILE, D_MODEL), lambda i: (i, 0))]
        + [_const_spec(w.shape) for w in ffn_w],
        out_specs=pl.BlockSpec(memory_space=pl.ANY),
        out_shape=jax.ShapeDtypeStruct((nb, n // nb, D_MODEL), F32),
        scratch_shapes=[pltpu.VMEM((2, FFN_TILE // nb, nb, D_MODEL), F32),
                        pltpu.SemaphoreType.DMA((2, nb))],
        compiler_params=pltpu.CompilerParams(
            dimension_semantics=("arbitrary",), vmem_limit_bytes=VMEM_LIMIT_BYTES),
        name="ffn_prompt",
    )(x2, *ffn_w)


def _block_diag(blocks):
    hv, gl, r, c = blocks.shape
    eye = jnp.eye(gl, dtype=blocks.dtype)
    out = blocks[:, :, :, None, :] * eye[None, :, None, :, None]
    return out.reshape(hv, gl * r, gl * c)


def kernel(x_prompt, x_sample, state_s5_re, state_s5_im, norm1, w_in, lam_re, lam_im, log_dt, b_re, b_im, c_re, c_im, d_skip, w_glu, b_glu, cm_ln_g, cm_ln_b, w_s, b_s, g_s5, g_cm, w_out, norm2, w_gate, w_up, w_down, norm_f):
    depth = norm1.shape[0]
    assert depth == 1
    l = 0
    batch, seq, _ = x_prompt.shape
    dec_batch, dec_seq, _ = x_sample.shape
    gph = S5_GROUPS // S5_HALVES

    ar, ai, bbr, bbi = pl.pallas_call(
        _prep_kernel,
        out_shape=(
            jax.ShapeDtypeStruct((S5_GROUPS, S5_STATE), F32),
            jax.ShapeDtypeStruct((S5_GROUPS, S5_STATE), F32),
            jax.ShapeDtypeStruct((S5_GROUPS, S5_GROUP, S5_STATE), F32),
            jax.ShapeDtypeStruct((S5_GROUPS, S5_GROUP, S5_STATE), F32),
        ),
        name="s5_discretise",
    )(lam_re[l], lam_im[l], log_dt[l][:, None],
      jnp.swapaxes(b_re[l], 1, 2), jnp.swapaxes(b_im[l], 1, 2))
    bdr = _block_diag(bbr.reshape(S5_HALVES, gph, S5_GROUP, S5_STATE)).astype(BF16)
    bdi = _block_diag(bbi.reshape(S5_HALVES, gph, S5_GROUP, S5_STATE)).astype(BF16)
    cdr = _block_diag(jnp.swapaxes(c_re[l], 1, 2).reshape(S5_HALVES, gph, S5_STATE, S5_GROUP)).astype(BF16)
    cdi = _block_diag(jnp.swapaxes(c_im[l], 1, 2).reshape(S5_HALVES, gph, S5_STATE, S5_GROUP)).astype(BF16)

    row = lambda a: a.reshape(1, -1)
    common_head = [row(norm1[l]), w_in[l].astype(BF16), bdr, bdi, cdr, cdi,
                   row(ar), row(ai), row(d_skip[l]), w_glu[l].astype(BF16), row(b_glu[l]),
                   row(cm_ln_g[l]), row(cm_ln_b[l])]
    common_tail = [row(g_s5[l]), row(g_cm[l]), w_out[l].astype(BF16)]
    ffn_w = (row(norm2[l]), w_gate[l].astype(BF16), w_up[l].astype(BF16),
             w_down[l].astype(BF16), row(norm_f))

    wmix_p = w_s[l].reshape(CM_HEADS // 2, 2 * CHUNK, CHUNK)
    bmix_p = jnp.repeat(b_s[l].T, CM_HEAD_DIM, axis=1)
    xp1, pr, pi_, pv = _mixer_prompt(x_prompt, common_head + [wmix_p, bmix_p] + common_tail)
    y_prompt = _ffn_prompt(xp1, ffn_w, nb=batch)

    wmix_s = jnp.repeat(jnp.transpose(w_s[l][:, :dec_seq, :dec_seq], (1, 2, 0)), CM_HEAD_DIM, axis=2)
    bmix_s = jnp.repeat(b_s[l][:, :dec_seq].T, CM_HEAD_DIM, axis=1)
    h0 = (state_s5_re[l].reshape(dec_batch, N_STATES), state_s5_im[l].reshape(dec_batch, N_STATES))
    xs1, sr, si, sv = _mixer_sample(x_sample, h0, common_head + [wmix_s, bmix_s] + common_tail)
    y_sample = _ffn(xs1, ffn_w, name="ffn_sample")

    st = lambda a, n: a.reshape(1, n, S5_GROUPS, S5_STATE)
    return (y_prompt, y_sample.reshape(dec_batch, dec_seq, D_MODEL),
            st(pr, batch), st(pi_, batch), pv[None],
            st(sr, dec_batch), st(si, dec_batch), sv[None])
```

```python
import functools

import jax
import jax.numpy as jnp
from jax import lax
from jax.experimental import pallas as pl
from jax.experimental.pallas import tpu as pltpu

F32 = jnp.float32
BF16 = jnp.bfloat16
HIGHEST = lax.Precision.HIGHEST

D_MODEL = 1024
S5_WIDTH = 512
S5_GROUP = 16
S5_GROUPS = 32
S5_STATE = 64
N_STATES = S5_GROUPS * S5_STATE
CM_WIDTH = 512
CM_HEADS = 8
CM_HEAD_DIM = 64
CHUNK = 128
IN_WIDTH = S5_WIDTH + 2 * CM_WIDTH
D_FF = 2816
EPS = 1e-6

S5_HALVES = 2
HALF_GROUPS = S5_GROUPS // S5_HALVES
HALF_IN = S5_WIDTH // S5_HALVES
HALF_STATES = N_STATES // S5_HALVES
SUBLANES = 8
LANES = 128
SCAN_LANES = 512
SCAN_BLOCKS = SCAN_LANES // LANES
MXU_COLS = 256
TILE_ROWS = SUBLANES * CHUNK
FF_CHUNK = 256
FFN_TILE = 512
CAST_ROWS = 128
VMEM_LIMIT_BYTES = 58 * 1024 * 1024


def _rms(x, g):
    r = lax.rsqrt(jnp.mean(x * x, axis=-1, keepdims=True) + EPS)
    return x * r * g


def _dot(a, b):
    return jnp.dot(a, b, preferred_element_type=F32)


def _store_lane_blocks(ref, val):
    for k in range(val.shape[1] // LANES):
        ref[k] = val[:, k * LANES:(k + 1) * LANES]


def _load_lane_blocks(ref, n_blocks):
    return jnp.concatenate([ref[k] for k in range(n_blocks)], axis=1)


def _cast_weight(w_hbm, w_bf_ref, stage_ref, sem):
    n_rows, n_cols = w_hbm.shape
    assert n_rows % CAST_ROWS == 0
    n_chunks = n_rows // CAST_ROWS

    def chunk_copy(c, sl):
        return pltpu.make_async_copy(w_hbm.at[pl.ds(c * CAST_ROWS, CAST_ROWS), :],
                                     stage_ref.at[sl, :, pl.ds(0, n_cols)], sem.at[sl])

    chunk_copy(0, 0).start()
    for c in range(n_chunks):
        sl = c % 2
        if c + 1 < n_chunks:
            chunk_copy(c + 1, 1 - sl).start()
        chunk_copy(c, sl).wait()
        w_bf_ref[c * CAST_ROWS:(c + 1) * CAST_ROWS, :] = stage_ref[sl, :, :n_cols].astype(BF16)


def _prep_kernel(lr_ref, li_ref, ldt_ref, br_ref, bi_ref, cr_ref, ci_ref,
                 ar_ref, ai_ref, bdr_ref, bdi_ref, cdr_ref, cdi_ref):
    dt = jnp.exp(ldt_ref[...])
    lr = lr_ref[...]
    li = li_ref[...]
    mag = jnp.exp(lr * dt)
    ar = mag * jnp.cos(li * dt)
    ai = mag * jnp.sin(li * dt)
    den = lr * lr + li * li
    qr = ((ar - 1.0) * lr + ai * li) / den
    qi = (ai * lr - (ar - 1.0) * li) / den
    ar_ref[...] = ar
    ai_ref[...] = ai
    qr_t = qr.T
    qi_t = qi.T

    def iota(shape, dim):
        return lax.broadcasted_iota(jnp.int32, shape, dim)

    rep_b = (iota((S5_GROUP, HALF_IN), 1) % S5_GROUP == iota((S5_GROUP, HALF_IN), 0)).astype(F32)
    rep_c = (iota((S5_STATE, HALF_STATES), 1) % S5_STATE == iota((S5_STATE, HALF_STATES), 0)).astype(F32)
    diag_b = iota((HALF_STATES, HALF_IN), 0) // S5_STATE == iota((HALF_STATES, HALF_IN), 1) // S5_GROUP
    diag_c = iota((HALF_IN, HALF_STATES), 0) // S5_GROUP == iota((HALF_IN, HALF_STATES), 1) // S5_STATE
    for hv in range(S5_HALVES):
        groups = range(hv * HALF_GROUPS, (hv + 1) * HALF_GROUPS)
        bb_r = jnp.concatenate([qr_t[:, g:g + 1] * br_ref[g] - qi_t[:, g:g + 1] * bi_ref[g]
                                for g in groups], axis=0)
        bb_i = jnp.concatenate([qr_t[:, g:g + 1] * bi_ref[g] + qi_t[:, g:g + 1] * br_ref[g]
                                for g in groups], axis=0)
        for bb, o_ref in ((bb_r, bdr_ref), (bb_i, bdi_ref)):
            wide = jnp.dot(bb, rep_b, precision=HIGHEST, preferred_element_type=F32)
            o_ref[hv] = jnp.where(diag_b, wide, 0.0).T.astype(BF16)
        for c_ref, o_ref in ((cr_ref, cdr_ref), (ci_ref, cdi_ref)):
            c_half = c_ref[hv * HALF_IN:(hv + 1) * HALF_IN, :]
            wide = jnp.dot(c_half, rep_c, precision=HIGHEST, preferred_element_type=F32)
            o_ref[hv] = jnp.where(diag_c, wide, 0.0).T.astype(BF16)


def _s5_gate(y_halves, xs, dskip_ref, wglu_ref, bglu_ref):
    y = jnp.concatenate(y_halves, axis=1) + dskip_ref[...] * xs
    g = jax.nn.gelu(y)
    return g * jax.nn.sigmoid(_dot(g.astype(BF16), wglu_ref[...]) + bglu_ref[...])


def _cm_uv(p, lng_ref, lnb_ref):
    uv = jax.nn.gelu(p[:, S5_WIDTH:])
    u = uv[:, :CM_WIDTH]
    vv = uv[:, CM_WIDTH:]
    mu = jnp.mean(vv, axis=-1, keepdims=True)
    vc = vv - mu
    v = vc * lax.rsqrt(jnp.mean(vc * vc, axis=-1, keepdims=True) + EPS) * lng_ref[...] + lnb_ref[...]
    return u, v


def _out_proj(x, o_s5, o_cm, gs5_ref, gcm_ref, wout_ref):
    n_s5 = _rms(o_s5, gs5_ref[...]).astype(BF16)
    n_cm = _rms(o_cm, gcm_ref[...]).astype(BF16)
    return x + _dot(n_s5, wout_ref[:S5_WIDTH, :]) + _dot(n_cm, wout_ref[S5_WIDTH:, :])


def _scan_lanes(bur_ref, bui_ref, ar_ref, ai_ref, hf, lc, h_init, rows_at, n_steps, unrolled):
    blocks = range(lc * SCAN_BLOCKS, (lc + 1) * SCAN_BLOCKS)
    glanes = slice(hf * HALF_STATES + lc * SCAN_LANES, hf * HALF_STATES + (lc + 1) * SCAN_LANES)
    a_r = jnp.broadcast_to(ar_ref[:, glanes], (SUBLANES, SCAN_LANES))
    a_i = jnp.broadcast_to(ai_ref[:, glanes], (SUBLANES, SCAN_LANES))

    def scan_step(t, carry):
        hr, hi = carry
        trows = rows_at(t)
        b_r = jnp.concatenate([bur_ref[k, trows, :] for k in blocks], axis=1)
        b_i = jnp.concatenate([bui_ref[k, trows, :] for k in blocks], axis=1)
        nr = a_r * hr - a_i * hi + b_r
        ni = a_r * hi + a_i * hr + b_i
        for n, k in enumerate(blocks):
            bur_ref[k, trows, :] = nr[:, n * LANES:(n + 1) * LANES]
            bui_ref[k, trows, :] = ni[:, n * LANES:(n + 1) * LANES]
        return nr, ni

    if unrolled:
        carry = h_init
        for t in range(n_steps):
            carry = scan_step(t, carry)
        return carry
    return lax.fori_loop(0, n_steps, scan_step, h_init, unroll=8)


def _mixer_kernel(n_prompt, dec_seq,
                  xp_hbm, xs_hbm, h0r_hbm, h0i_hbm, win_hbm, wout_hbm, wglu_hbm,
                  g1_ref, bdr_ref, bdi_ref, cdr_ref, cdi_ref, ar_ref, ai_ref, dskip_ref, bglu_ref,
                  lng_ref, lnb_ref, gs5_ref, gcm_ref, wmixp_ref, bmixp_ref, wmixs_ref, bmixs_ref,
                  x1_ref, pr_ref, pi_ref, pv_ref, sr_ref, si_ref, sv_ref,
                  xt_ref, in_sem, bur_ref, bui_ref, hsr_ref, hsi_ref, cma_ref, cmb_ref, cmc_ref,
                  win_ref, wout_ref, wglu_ref, stage_ref, w_sem, h0_sem):
    nb = SUBLANES
    rows = TILE_ROWS
    step = pl.program_id(0)
    slot = step % 2

    def fetch_prompt(s, sl):
        return [pltpu.make_async_copy(xp_hbm.at[b, pl.ds(s * CHUNK, CHUNK), :], xt_ref.at[sl, :, b, :],
                                      in_sem.at[sl, b]) for b in range(nb)]

    def fetch_sample(sl):
        return pltpu.make_async_copy(xs_hbm, xt_ref.at[sl], in_sem.at[sl, 0])

    @pl.when(step == 0)
    def _():
        for cp in fetch_prompt(0, 0):
            cp.start()
        _cast_weight(win_hbm, win_ref, stage_ref, w_sem)
        _cast_weight(wout_hbm, wout_ref, stage_ref, w_sem)
        _cast_weight(wglu_hbm, wglu_ref, stage_ref, w_sem)
        h0_copies = [pltpu.make_async_copy(h0r_hbm, sr_ref, h0_sem.at[0]),
                     pltpu.make_async_copy(h0i_hbm, si_ref, h0_sem.at[1])]
        for cp in h0_copies:
            cp.start()
        for cp in h0_copies:
            cp.wait()
        hsr_ref[...] = jnp.zeros_like(hsr_ref)
        hsi_ref[...] = jnp.zeros_like(hsi_ref)

    @pl.when(step + 1 < n_prompt)
    def _():
        for cp in fetch_prompt(step + 1, 1 - slot):
            cp.start()

    @pl.when(step + 1 == n_prompt)
    def _():
        fetch_sample(1 - slot).start()

    @pl.when(step < n_prompt)
    def _():
        for cp in fetch_prompt(step, slot):
            cp.wait()

    @pl.when(step == n_prompt)
    def _():
        fetch_sample(slot).wait()

    is_prompt = step < n_prompt
    is_sample = step == n_prompt
    cm_blocks = CM_WIDTH // LANES
    us_ref, vs_ref, mx_ref = cma_ref, cmb_ref, cmc_ref

    x = xt_ref[slot].reshape(rows, D_MODEL)
    p = _dot(_rms(x, g1_ref[...]).astype(BF16), win_ref[...])
    xs = p[:, :S5_WIDTH]
    u, v = _cm_uv(p, lng_ref, lnb_ref)
    _store_lane_blocks(us_ref, u)
    _store_lane_blocks(vs_ref, v)

    y_halves = []
    for hf in range(S5_HALVES):
        xs_h = xs[:, hf * HALF_IN:(hf + 1) * HALF_IN].astype(BF16)
        for n in range(HALF_STATES // MXU_COLS):
            cols = slice(n * MXU_COLS, (n + 1) * MXU_COLS)
            b_r = _dot(xs_h, bdr_ref[hf, :, cols])
            b_i = _dot(xs_h, bdi_ref[hf, :, cols])
            for q in range(MXU_COLS // LANES):
                k = n * (MXU_COLS // LANES) + q
                bur_ref[k] = b_r[:, q * LANES:(q + 1) * LANES]
                bui_ref[k] = b_i[:, q * LANES:(q + 1) * LANES]

        @pl.when(is_prompt)
        def _(hf=hf):
            for lc in range(HALF_STATES // SCAN_LANES):
                glanes = slice(hf * HALF_STATES + lc * SCAN_LANES, hf * HALF_STATES + (lc + 1) * SCAN_LANES)
                hr, hi = _scan_lanes(bur_ref, bui_ref, ar_ref, ai_ref, hf, lc,
                                     (hsr_ref[:, glanes], hsi_ref[:, glanes]),
                                     lambda t: pl.ds(pl.multiple_of(t * nb, nb), nb), CHUNK, False)
                hsr_ref[:, glanes] = hr
                hsi_ref[:, glanes] = hi

        @pl.when(is_sample)
        def _(hf=hf):
            tt = dec_seq
            for sg in range(rows // tt // SUBLANES):
                srows = slice(sg * SUBLANES, (sg + 1) * SUBLANES)
                for lc in range(HALF_STATES // SCAN_LANES):
                    glanes = slice(hf * HALF_STATES + lc * SCAN_LANES,
                                   hf * HALF_STATES + (lc + 1) * SCAN_LANES)
                    hr, hi = _scan_lanes(bur_ref, bui_ref, ar_ref, ai_ref, hf, lc,
                                         (sr_ref[srows, glanes], si_ref[srows, glanes]),
                                         lambda t, sg=sg: pl.ds(sg * SUBLANES * tt + t, SUBLANES, stride=tt),
                                         tt, True)
                    sr_ref[srows, glanes] = hr
                    si_ref[srows, glanes] = hi

        y_h = None
        for n in range(HALF_STATES // MXU_COLS):
            krows = slice(n * MXU_COLS, (n + 1) * MXU_COLS)
            blocks = range(n * (MXU_COLS // LANES), (n + 1) * (MXU_COLS // LANES))
            h_r = jnp.concatenate([bur_ref[k] for k in blocks], axis=1).astype(BF16)
            h_i = jnp.concatenate([bui_ref[k] for k in blocks], axis=1).astype(BF16)
            part = _dot(h_r, cdr_ref[hf, krows, :]) - _dot(h_i, cdi_ref[hf, krows, :])
            y_h = part if y_h is None else y_h + part
        y_halves.append(y_h)
    o_s5 = _s5_gate(y_halves, xs, dskip_ref, wglu_ref, bglu_ref)

    @pl.when(is_prompt)
    def _():
        pr_ref[...] = hsr_ref[...]
        pi_ref[...] = hsi_ref[...]
        r_idx = lax.broadcasted_iota(jnp.int32, (2 * CHUNK, CHUNK), 0)
        c_idx = lax.broadcasted_iota(jnp.int32, (2 * CHUNK, CHUNK), 1)
        causal = (r_idx % CHUNK) >= c_idx
        first_head = lax.broadcasted_iota(jnp.int32, (CHUNK, LANES), 1) < CM_HEAD_DIM
        for j in range(CM_HEADS // 2):
            w_pair = jnp.where(causal, wmixp_ref[j], 0.0).astype(BF16)
            bias = bmixp_ref[:, j * LANES:(j + 1) * LANES]
            for b in range(nb):
                seq_rows = pl.ds(b, CHUNK, stride=nb)
                v_seq = vs_ref[j, seq_rows, :]
                pv_ref[b, :, j * LANES:(j + 1) * LANES] = v_seq
                r = _dot(w_pair, v_seq.astype(BF16))
                mixed = jnp.where(first_head, r[:CHUNK], r[CHUNK:]) + bias
                mx_ref[j, seq_rows, :] = us_ref[j, seq_rows, :] * mixed

    @pl.when(is_sample)
    def _():
        tt = dec_seq
        n_seq = rows // tt
        sv_ref[...] = _load_lane_blocks(vs_ref, cm_blocks).reshape(n_seq, tt, CM_WIDTH)
        for k in range(cm_blocks):
            lanes = slice(k * LANES, (k + 1) * LANES)
            v_at = [vs_ref[k, pl.ds(s, n_seq, stride=tt), :] for s in range(tt)]
            for t in range(tt):
                acc = bmixs_ref[t:t + 1, lanes] + wmixs_ref[t, 0:1, lanes] * v_at[0]
                for s in range(1, t + 1):
                    acc = acc + wmixs_ref[t, s:s + 1, lanes] * v_at[s]
                trows = pl.ds(t, n_seq, stride=tt)
                mx_ref[k, trows, :] = us_ref[k, trows, :] * acc

    o_cm = _load_lane_blocks(mx_ref, cm_blocks)
    x_res = xt_ref[slot].reshape(rows, D_MODEL)
    x1_ref[...] = _out_proj(x_res, o_s5, o_cm, gs5_ref, gcm_ref, wout_ref)


def _const_spec(shape):
    nd = len(shape)
    return pl.BlockSpec(shape, lambda i, nd=nd: (0,) * nd, pipeline_mode=pl.Buffered(1))


_ANY_SPEC = pl.BlockSpec(memory_space=pl.ANY)


def _mixer(x_prompt, x_sample, h0, big_w, small_w):
    nb, seq, _ = x_prompt.shape
    dec_batch, dec_seq, _ = x_sample.shape
    assert nb == SUBLANES and seq % CHUNK == 0
    assert dec_batch * dec_seq == TILE_ROWS and dec_seq <= SUBLANES and dec_batch == CHUNK
    n_prompt = seq // CHUNK
    w_in, w_out, w_glu = big_w
    in_specs = [_ANY_SPEC] * 7 + [_const_spec(a.shape) for a in small_w]
    out_shape = (
        jax.ShapeDtypeStruct(((n_prompt + 1) * TILE_ROWS, D_MODEL), F32),
        jax.ShapeDtypeStruct((nb, N_STATES), F32),
        jax.ShapeDtypeStruct((nb, N_STATES), F32),
        jax.ShapeDtypeStruct((nb, CHUNK, CM_WIDTH), F32),
        jax.ShapeDtypeStruct((dec_batch, N_STATES), F32),
        jax.ShapeDtypeStruct((dec_batch, N_STATES), F32),
        jax.ShapeDtypeStruct((dec_batch, dec_seq, CM_WIDTH), F32),
    )
    out_specs = (pl.BlockSpec((TILE_ROWS, D_MODEL), lambda i: (i, 0)),) + tuple(
        _const_spec(o.shape) for o in out_shape[1:])
    scratch = [
        pltpu.VMEM((2, CHUNK, nb, D_MODEL), F32),
        pltpu.SemaphoreType.DMA((2, nb)),
        pltpu.VMEM((HALF_STATES // LANES, TILE_ROWS, LANES), F32),
        pltpu.VMEM((HALF_STATES // LANES, TILE_ROWS, LANES), F32),
        pltpu.VMEM((nb, N_STATES), F32),
        pltpu.VMEM((nb, N_STATES), F32),
        pltpu.VMEM((CM_WIDTH // LANES, TILE_ROWS, LANES), F32),
        pltpu.VMEM((CM_WIDTH // LANES, TILE_ROWS, LANES), F32),
        pltpu.VMEM((CM_WIDTH // LANES, TILE_ROWS, LANES), F32),
        pltpu.VMEM(w_in.shape, BF16),
        pltpu.VMEM(w_out.shape, BF16),
        pltpu.VMEM(w_glu.shape, BF16),
        pltpu.VMEM((2, CAST_ROWS, IN_WIDTH), F32),
        pltpu.SemaphoreType.DMA((2,)),
        pltpu.SemaphoreType.DMA((2,)),
    ]
    return pl.pallas_call(
        functools.partial(_mixer_kernel, n_prompt, dec_seq),
        grid=(n_prompt + 1,),
        in_specs=in_specs,
        out_specs=out_specs,
        out_shape=out_shape,
        scratch_shapes=scratch,
        compiler_params=pltpu.CompilerParams(
            dimension_semantics=("arbitrary",), vmem_limit_bytes=VMEM_LIMIT_BYTES),
        name="mixer",
    )(x_prompt, x_sample, *h0, w_in, w_out, w_glu, *small_w)


def _ffn_kernel(nb, n_prompt, x_ref, g2_ref, gf_ref, wg_hbm, wu_hbm, wd_hbm, yp_hbm, ys_hbm,
                yt_ref, out_sem, wg_ref, wu_ref, wd_ref, stage_ref, w_sem):
    tt = FFN_TILE // nb
    step = pl.program_id(0)
    n_steps = pl.num_programs(0)
    slot = step % 2

    def prompt_writeback(s, sl):
        return [pltpu.make_async_copy(yt_ref.at[sl, :, b, :], yp_hbm.at[b, pl.ds(s * tt, tt), :],
                                      out_sem.at[sl, b]) for b in range(nb)]

    def sample_writeback(s, sl):
        return [pltpu.make_async_copy(yt_ref.at[sl], ys_hbm.at[s - n_prompt], out_sem.at[sl, 0])]

    def wait_writeback(s, sl):
        @pl.when(s < n_prompt)
        def _():
            for cp in prompt_writeback(s, sl):
                cp.wait()

        @pl.when(s >= n_prompt)
        def _():
            for cp in sample_writeback(s, sl):
                cp.wait()

    @pl.when(step == 0)
    def _():
        _cast_weight(wg_hbm, wg_ref, stage_ref, w_sem)
        _cast_weight(wu_hbm, wu_ref, stage_ref, w_sem)
        _cast_weight(wd_hbm, wd_ref, stage_ref, w_sem)

    @pl.when(step >= 2)
    def _():
        wait_writeback(step - 2, slot)

    x = x_ref[...]
    h = _rms(x, g2_ref[...]).astype(BF16)
    acc = x
    for c in range(D_FF // FF_CHUNK):
        cols = slice(c * FF_CHUNK, (c + 1) * FF_CHUNK)
        gate = _dot(h, wg_ref[:, cols])
        up = _dot(h, wu_ref[:, cols])
        acc = acc + _dot((jax.nn.silu(gate) * up).astype(BF16), wd_ref[cols, :])
    yt_ref[slot] = _rms(acc, gf_ref[...]).reshape(tt, nb, D_MODEL)

    @pl.when(step < n_prompt)
    def _():
        for cp in prompt_writeback(step, slot):
            cp.start()

    @pl.when(step >= n_prompt)
    def _():
        for cp in sample_writeback(step, slot):
            cp.start()

    @pl.when(step == n_steps - 1)
    def _():
        wait_writeback(step - 1, 1 - slot)
        wait_writeback(step, slot)


def _ffn(x1, g2, gf, ffn_w, *, nb, seq, n_sample_rows):
    n = x1.shape[0]
    n_prompt = nb * seq // FFN_TILE
    n_sample = n_sample_rows // FFN_TILE
    assert n == (n_prompt + n_sample) * FFN_TILE and FFN_TILE % nb == 0 and n_prompt + n_sample >= 2
    tt = FFN_TILE // nb
    w_gate, w_up, w_down = ffn_w
    return pl.pallas_call(
        functools.partial(_ffn_kernel, nb, n_prompt),
        grid=(n_prompt + n_sample,),
        in_specs=[pl.BlockSpec((FFN_TILE, D_MODEL), lambda i: (i, 0)),
                  _const_spec(g2.shape), _const_spec(gf.shape)] + [_ANY_SPEC] * 3,
        out_specs=(_ANY_SPEC, _ANY_SPEC),
        out_shape=(jax.ShapeDtypeStruct((nb, seq, D_MODEL), F32),
                   jax.ShapeDtypeStruct((n_sample, tt, nb, D_MODEL), F32)),
        scratch_shapes=[pltpu.VMEM((2, tt, nb, D_MODEL), F32),
                        pltpu.SemaphoreType.DMA((2, nb)),
                        pltpu.VMEM(w_gate.shape, BF16),
                        pltpu.VMEM(w_up.shape, BF16),
                        pltpu.VMEM(w_down.shape, BF16),
                        pltpu.VMEM((2, CAST_ROWS, D_FF), F32),
                        pltpu.SemaphoreType.DMA((2,))],
        compiler_params=pltpu.CompilerParams(
            dimension_semantics=("arbitrary",), vmem_limit_bytes=VMEM_LIMIT_BYTES),
        name="ffn",
    )(x1, g2, gf, w_gate, w_up, w_down)


def kernel(x_prompt, x_sample, state_s5_re, state_s5_im, norm1, w_in, lam_re, lam_im, log_dt, b_re, b_im, c_re, c_im, d_skip, w_glu, b_glu, cm_ln_g, cm_ln_b, w_s, b_s, g_s5, g_cm, w_out, norm2, w_gate, w_up, w_down, norm_f):
    depth = norm1.shape[0]
    assert depth == 1
    l = 0
    batch, seq, _ = x_prompt.shape
    dec_batch, dec_seq, _ = x_sample.shape

    ar, ai, bdr, bdi, cdr, cdi = pl.pallas_call(
        _prep_kernel,
        out_shape=(
            jax.ShapeDtypeStruct((S5_GROUPS, S5_STATE), F32),
            jax.ShapeDtypeStruct((S5_GROUPS, S5_STATE), F32),
            jax.ShapeDtypeStruct((S5_HALVES, HALF_IN, HALF_STATES), BF16),
            jax.ShapeDtypeStruct((S5_HALVES, HALF_IN, HALF_STATES), BF16),
            jax.ShapeDtypeStruct((S5_HALVES, HALF_STATES, HALF_IN), BF16),
            jax.ShapeDtypeStruct((S5_HALVES, HALF_STATES, HALF_IN), BF16),
        ),
        name="s5_prepare",
    )(lam_re[l], lam_im[l], log_dt[l][:, None], b_re[l], b_im[l],
      c_re[l].reshape(S5_WIDTH, S5_STATE), c_im[l].reshape(S5_WIDTH, S5_STATE))

    row = lambda a: a.reshape(1, -1)
    wmix_p = w_s[l].reshape(CM_HEADS // 2, 2 * CHUNK, CHUNK)
    bmix_p = jnp.repeat(b_s[l].T, CM_HEAD_DIM, axis=1)
    wmix_s = jnp.repeat(jnp.transpose(w_s[l][:, :dec_seq, :dec_seq], (1, 2, 0)), CM_HEAD_DIM, axis=2)
    bmix_s = bmix_p[:dec_seq]
    small_w = [row(norm1[l]), bdr, bdi, cdr, cdi, row(ar), row(ai), row(d_skip[l]), row(b_glu[l]),
               row(cm_ln_g[l]), row(cm_ln_b[l]), row(g_s5[l]), row(g_cm[l]),
               wmix_p, bmix_p, wmix_s, bmix_s]
    h0 = (state_s5_re[l].reshape(dec_batch, N_STATES), state_s5_im[l].reshape(dec_batch, N_STATES))

    x1, pr, pi_, pv, sr, si, sv = _mixer(x_prompt, x_sample, h0, (w_in[l], w_out[l], w_glu[l]), small_w)
    y_prompt, y_sample = _ffn(x1, row(norm2[l]), row(norm_f), (w_gate[l], w_up[l], w_down[l]),
                              nb=batch, seq=seq, n_sample_rows=dec_batch * dec_seq)

    st = lambda a, n: a.reshape(1, n, S5_GROUPS, S5_STATE)
    return (y_prompt, y_sample.reshape(dec_batch, dec_seq, D_MODEL),
            st(pr, batch), st(pi_, batch), pv[None],
            st(sr, dec_batch), st(si, dec_batch), sv[None])
```

```python
import functools

import jax
import jax.numpy as jnp
from jax import lax
from jax.experimental import pallas as pl
from jax.experimental.pallas import tpu as pltpu

F32 = jnp.float32
BF16 = jnp.bfloat16
HIGHEST = lax.Precision.HIGHEST

D_MODEL = 1024
S5_WIDTH = 512
S5_GROUP = 16
S5_GROUPS = 32
S5_STATE = 64
N_STATES = S5_GROUPS * S5_STATE
CM_WIDTH = 512
CM_HEADS = 8
CM_HEAD_DIM = 64
CHUNK = 128
IN_WIDTH = S5_WIDTH + 2 * CM_WIDTH
D_FF = 2816
EPS = 1e-6

S5_HALVES = 2
HALF_GROUPS = S5_GROUPS // S5_HALVES
HALF_IN = S5_WIDTH // S5_HALVES
HALF_STATES = N_STATES // S5_HALVES
SUBLANES = 8
LANES = 128
SCAN_LANES = 512
SCAN_BLOCKS = SCAN_LANES // LANES
MXU_COLS = 256
TILE_ROWS = SUBLANES * CHUNK
ROW_BLOCK = 256
FF_CHUNK = 256
CAST_ROWS = 128
VMEM_LIMIT_BYTES = 58 * 1024 * 1024


def _rms(x, g):
    r = lax.rsqrt(jnp.mean(x * x, axis=-1, keepdims=True) + EPS)
    return x * r * g


def _dot(a, b):
    return jnp.dot(a, b, preferred_element_type=F32)


def _load_lane_blocks(ref, n_blocks):
    return jnp.concatenate([ref[k] for k in range(n_blocks)], axis=1)


def _cast_weight(w_hbm, w_bf_ref, stage_ref, sem):
    n_rows, n_cols = w_hbm.shape
    assert n_rows % CAST_ROWS == 0
    n_chunks = n_rows // CAST_ROWS

    def chunk_copy(c, sl):
        return pltpu.make_async_copy(w_hbm.at[pl.ds(c * CAST_ROWS, CAST_ROWS), :],
                                     stage_ref.at[sl, :, pl.ds(0, n_cols)], sem.at[sl])

    chunk_copy(0, 0).start()
    for c in range(n_chunks):
        sl = c % 2
        if c + 1 < n_chunks:
            chunk_copy(c + 1, 1 - sl).start()
        chunk_copy(c, sl).wait()
        w_bf_ref[c * CAST_ROWS:(c + 1) * CAST_ROWS, :] = stage_ref[sl, :, :n_cols].astype(BF16)


def _prep_kernel(lr_ref, li_ref, ldt_ref, br_ref, bi_ref, cr_ref, ci_ref,
                 ar_ref, ai_ref, bdr_ref, bdi_ref, cdr_ref, cdi_ref):
    dt = jnp.exp(ldt_ref[...])
    lr = lr_ref[...]
    li = li_ref[...]
    mag = jnp.exp(lr * dt)
    ar = mag * jnp.cos(li * dt)
    ai = mag * jnp.sin(li * dt)
    den = lr * lr + li * li
    qr = ((ar - 1.0) * lr + ai * li) / den
    qi = (ai * lr - (ar - 1.0) * li) / den
    ar_ref[...] = ar
    ai_ref[...] = ai
    qr_t = qr.T
    qi_t = qi.T

    def iota(shape, dim):
        return lax.broadcasted_iota(jnp.int32, shape, dim)

    rep_b = (iota((S5_GROUP, HALF_IN), 1) % S5_GROUP == iota((S5_GROUP, HALF_IN), 0)).astype(F32)
    rep_c = (iota((S5_STATE, HALF_STATES), 1) % S5_STATE == iota((S5_STATE, HALF_STATES), 0)).astype(F32)
    diag_b = iota((HALF_STATES, HALF_IN), 0) // S5_STATE == iota((HALF_STATES, HALF_IN), 1) // S5_GROUP
    diag_c = iota((HALF_IN, HALF_STATES), 0) // S5_GROUP == iota((HALF_IN, HALF_STATES), 1) // S5_STATE
    for hv in range(S5_HALVES):
        groups = range(hv * HALF_GROUPS, (hv + 1) * HALF_GROUPS)
        bb_r = jnp.concatenate([qr_t[:, g:g + 1] * br_ref[g] - qi_t[:, g:g + 1] * bi_ref[g]
                                for g in groups], axis=0)
        bb_i = jnp.concatenate([qr_t[:, g:g + 1] * bi_ref[g] + qi_t[:, g:g + 1] * br_ref[g]
                                for g in groups], axis=0)
        for bb, o_ref in ((bb_r, bdr_ref), (bb_i, bdi_ref)):
            wide = jnp.dot(bb, rep_b, precision=HIGHEST, preferred_element_type=F32)
            o_ref[hv] = jnp.where(diag_b, wide, 0.0).T.astype(BF16)
        for c_ref, o_ref in ((cr_ref, cdr_ref), (ci_ref, cdi_ref)):
            c_half = c_ref[hv * HALF_IN:(hv + 1) * HALF_IN, :]
            wide = jnp.dot(c_half, rep_c, precision=HIGHEST, preferred_element_type=F32)
            o_ref[hv] = jnp.where(diag_c, wide, 0.0).T.astype(BF16)


def _cm_uv(p_cm, lng_ref, lnb_ref):
    uv = jax.nn.gelu(p_cm)
    u = uv[:, :CM_WIDTH]
    vv = uv[:, CM_WIDTH:]
    mu = jnp.mean(vv, axis=-1, keepdims=True)
    vc = vv - mu
    v = vc * lax.rsqrt(jnp.mean(vc * vc, axis=-1, keepdims=True) + EPS) * lng_ref[...] + lnb_ref[...]
    return u, v


def _scan_lanes(bur_ref, bui_ref, ar_ref, ai_ref, hf, lc, h_init, rows_at, n_steps, unrolled):
    blocks = range(lc * SCAN_BLOCKS, (lc + 1) * SCAN_BLOCKS)
    glanes = slice(hf * HALF_STATES + lc * SCAN_LANES, hf * HALF_STATES + (lc + 1) * SCAN_LANES)
    a_r = jnp.broadcast_to(ar_ref[:, glanes], (SUBLANES, SCAN_LANES))
    a_i = jnp.broadcast_to(ai_ref[:, glanes], (SUBLANES, SCAN_LANES))

    def scan_step(t, carry):
        hr, hi = carry
        trows = rows_at(t)
        b_r = jnp.concatenate([bur_ref[k, trows, :] for k in blocks], axis=1)
        b_i = jnp.concatenate([bui_ref[k, trows, :] for k in blocks], axis=1)
        nr = a_r * hr - a_i * hi + b_r
        ni = a_r * hi + a_i * hr + b_i
        for n, k in enumerate(blocks):
            bur_ref[k, trows, :] = nr[:, n * LANES:(n + 1) * LANES]
            bui_ref[k, trows, :] = ni[:, n * LANES:(n + 1) * LANES]
        return nr, ni

    if unrolled:
        carry = h_init
        for t in range(n_steps):
            carry = scan_step(t, carry)
        return carry
    return lax.fori_loop(0, n_steps, scan_step, h_init, unroll=8)


def _mixer_kernel(n_prompt, dec_seq,
                  xp_hbm, xs_hbm, h0r_hbm, h0i_hbm, win_hbm, wout_hbm, wglu_hbm,
                  g1_ref, bdr_ref, bdi_ref, cdr_ref, cdi_ref, ar_ref, ai_ref, dskip_ref, bglu_ref,
                  lng_ref, lnb_ref, gs5_ref, gcm_ref, wmixp_ref, bmixp_ref, wmixs_ref, bmixs_ref,
                  x1_ref, pr_ref, pi_ref, pv_ref, sr_ref, si_ref, sv_ref,
                  xt_ref, in_sem, bur_ref, bui_ref, hsr_ref, hsi_ref, cma_ref, cmb_ref, cmc_ref,
                  xs_ref, y0_ref, win_ref, wout_ref, wglu_ref, stage_ref, w_sem, h0_sem):
    nb = SUBLANES
    rows = TILE_ROWS
    step = pl.program_id(0)
    slot = step % 2

    def fetch_prompt(s, sl):
        return [pltpu.make_async_copy(xp_hbm.at[b, pl.ds(s * CHUNK, CHUNK), :], xt_ref.at[sl, :, b, :],
                                      in_sem.at[sl, b]) for b in range(nb)]

    def fetch_sample(sl):
        return pltpu.make_async_copy(xs_hbm, xt_ref.at[sl], in_sem.at[sl, 0])

    @pl.when(step == 0)
    def _():
        for cp in fetch_prompt(0, 0):
            cp.start()
        _cast_weight(win_hbm, win_ref, stage_ref, w_sem)
        _cast_weight(wout_hbm, wout_ref, stage_ref, w_sem)
        _cast_weight(wglu_hbm, wglu_ref, stage_ref, w_sem)
        h0_copies = [pltpu.make_async_copy(h0r_hbm, sr_ref, h0_sem.at[0]),
                     pltpu.make_async_copy(h0i_hbm, si_ref, h0_sem.at[1])]
        for cp in h0_copies:
            cp.start()
        for cp in h0_copies:
            cp.wait()
        hsr_ref[...] = jnp.zeros_like(hsr_ref)
        hsi_ref[...] = jnp.zeros_like(hsi_ref)

    @pl.when(step + 1 < n_prompt)
    def _():
        for cp in fetch_prompt(step + 1, 1 - slot):
            cp.start()

    @pl.when(step + 1 == n_prompt)
    def _():
        fetch_sample(1 - slot).start()

    @pl.when(step < n_prompt)
    def _():
        for cp in fetch_prompt(step, slot):
            cp.wait()

    @pl.when(step == n_prompt)
    def _():
        fetch_sample(slot).wait()

    is_prompt = step < n_prompt
    is_sample = step == n_prompt
    cm_blocks = CM_WIDTH // LANES
    mxu_blocks = MXU_COLS // LANES
    us_ref, vs_ref, mx_ref = cma_ref, cmb_ref, cmc_ref
    row_blocks = [slice(r * ROW_BLOCK, (r + 1) * ROW_BLOCK) for r in range(rows // ROW_BLOCK)]

    def x_rows(rsl):
        groups = slice(rsl.start // nb, rsl.stop // nb)
        return xt_ref[slot, groups].reshape(ROW_BLOCK, D_MODEL)

    def b_proj(hf, xs_h, rsl):
        for n in range(HALF_STATES // MXU_COLS):
            cols = slice(n * MXU_COLS, (n + 1) * MXU_COLS)
            b_r = _dot(xs_h, bdr_ref[hf, :, cols])
            b_i = _dot(xs_h, bdi_ref[hf, :, cols])
            for q in range(mxu_blocks):
                bur_ref[n * mxu_blocks + q, rsl, :] = b_r[:, q * LANES:(q + 1) * LANES]
                bui_ref[n * mxu_blocks + q, rsl, :] = b_i[:, q * LANES:(q + 1) * LANES]

    def c_proj(hf, rsl):
        y_h = None
        for n in range(HALF_STATES // MXU_COLS):
            krows = slice(n * MXU_COLS, (n + 1) * MXU_COLS)
            blocks = range(n * mxu_blocks, (n + 1) * mxu_blocks)
            h_r = jnp.concatenate([bur_ref[k, rsl, :] for k in blocks], axis=1).astype(BF16)
            h_i = jnp.concatenate([bui_ref[k, rsl, :] for k in blocks], axis=1).astype(BF16)
            part = _dot(h_r, cdr_ref[hf, krows, :]) - _dot(h_i, cdi_ref[hf, krows, :])
            y_h = part if y_h is None else y_h + part
        return y_h

    def scan_half(hf):
        @pl.when(is_prompt)
        def _():
            for lc in range(HALF_STATES // SCAN_LANES):
                glanes = slice(hf * HALF_STATES + lc * SCAN_LANES, hf * HALF_STATES + (lc + 1) * SCAN_LANES)
                hr, hi = _scan_lanes(bur_ref, bui_ref, ar_ref, ai_ref, hf, lc,
                                     (hsr_ref[:, glanes], hsi_ref[:, glanes]),
                                     lambda t: pl.ds(pl.multiple_of(t * nb, nb), nb), CHUNK, False)
                hsr_ref[:, glanes] = hr
                hsi_ref[:, glanes] = hi
                pr_ref[:, glanes] = hr
                pi_ref[:, glanes] = hi

        @pl.when(is_sample)
        def _():
            tt = dec_seq
            for sg in range(rows // tt // SUBLANES):
                srows = slice(sg * SUBLANES, (sg + 1) * SUBLANES)
                for lc in range(HALF_STATES // SCAN_LANES):
                    glanes = slice(hf * HALF_STATES + lc * SCAN_LANES,
                                   hf * HALF_STATES + (lc + 1) * SCAN_LANES)
                    hr, hi = _scan_lanes(bur_ref, bui_ref, ar_ref, ai_ref, hf, lc,
                                         (sr_ref[srows, glanes], si_ref[srows, glanes]),
                                         lambda t, sg=sg: pl.ds(sg * SUBLANES * tt + t, SUBLANES, stride=tt),
                                         tt, True)
                    sr_ref[srows, glanes] = hr
                    si_ref[srows, glanes] = hi

    for rsl in row_blocks:
        h = _rms(x_rows(rsl), g1_ref[...]).astype(BF16)
        u, v = _cm_uv(_dot(h, win_ref[:, S5_WIDTH:]), lng_ref, lnb_ref)
        for k in range(cm_blocks):
            us_ref[k, rsl, :] = u[:, k * LANES:(k + 1) * LANES]
            vs_ref[k, rsl, :] = v[:, k * LANES:(k + 1) * LANES]
        xs = _dot(h, win_ref[:, :S5_WIDTH])
        xs_ref[rsl, :] = xs
        b_proj(0, xs[:, :HALF_IN].astype(BF16), rsl)

    @pl.when(is_prompt)
    def _():
        r_idx = lax.broadcasted_iota(jnp.int32, (2 * CHUNK, CHUNK), 0)
        c_idx = lax.broadcasted_iota(jnp.int32, (2 * CHUNK, CHUNK), 1)
        causal = (r_idx % CHUNK) >= c_idx
        first_head = lax.broadcasted_iota(jnp.int32, (CHUNK, LANES), 1) < CM_HEAD_DIM
        for j in range(CM_HEADS // 2):
            w_pair = jnp.where(causal, wmixp_ref[j], 0.0).astype(BF16)
            bias = bmixp_ref[:, j * LANES:(j + 1) * LANES]
            for b in range(nb):
                seq_rows = pl.ds(b, CHUNK, stride=nb)
                v_seq = vs_ref[j, seq_rows, :]
                pv_ref[b, :, j * LANES:(j + 1) * LANES] = v_seq
                r = _dot(w_pair, v_seq.astype(BF16))
                mixed = jnp.where(first_head, r[:CHUNK], r[CHUNK:]) + bias
                mx_ref[j, seq_rows, :] = us_ref[j, seq_rows, :] * mixed

    @pl.when(is_sample)
    def _():
        tt = dec_seq
        n_seq = rows // tt
        sv_ref[...] = _load_lane_blocks(vs_ref, cm_blocks).reshape(n_seq, tt, CM_WIDTH)
        for k in range(cm_blocks):
            lanes = slice(k * LANES, (k + 1) * LANES)
            v_at = [vs_ref[k, pl.ds(s, n_seq, stride=tt), :] for s in range(tt)]
            for t in range(tt):
                acc = bmixs_ref[t:t + 1, lanes] + wmixs_ref[t, 0:1, lanes] * v_at[0]
                for s in range(1, t + 1):
                    acc = acc + wmixs_ref[t, s:s + 1, lanes] * v_at[s]
                trows = pl.ds(t, n_seq, stride=tt)
                mx_ref[k, trows, :] = us_ref[k, trows, :] * acc

    scan_half(0)
    for rsl in row_blocks:
        y0_ref[rsl, :] = c_proj(0, rsl)
        b_proj(1, xs_ref[rsl, HALF_IN:].astype(BF16), rsl)
        o_cm = jnp.concatenate([mx_ref[k, rsl, :] for k in range(cm_blocks)], axis=1)
        n_cm = _rms(o_cm, gcm_ref[...]).astype(BF16)
        x1_ref[rsl, :] = x_rows(rsl) + _dot(n_cm, wout_ref[S5_WIDTH:, :])
    scan_half(1)
    for rsl in row_blocks:
        y = jnp.concatenate([y0_ref[rsl, :], c_proj(1, rsl)], axis=1) + dskip_ref[...] * xs_ref[rsl, :]
        g = jax.nn.gelu(y)
        o_s5 = g * jax.nn.sigmoid(_dot(g.astype(BF16), wglu_ref[...]) + bglu_ref[...])
        n_s5 = _rms(o_s5, gs5_ref[...]).astype(BF16)
        x1_ref[rsl, :] = x1_ref[rsl, :] + _dot(n_s5, wout_ref[:S5_WIDTH, :])


def _const_spec(shape):
    nd = len(shape)
    return pl.BlockSpec(shape, lambda i, nd=nd: (0,) * nd, pipeline_mode=pl.Buffered(1))


_ANY_SPEC = pl.BlockSpec(memory_space=pl.ANY)


def _mixer(x_prompt, x_sample, h0, big_w, small_w):
    nb, seq, _ = x_prompt.shape
    dec_batch, dec_seq, _ = x_sample.shape
    assert nb == SUBLANES and seq % CHUNK == 0
    assert dec_batch * dec_seq == TILE_ROWS and dec_seq <= SUBLANES and dec_batch == CHUNK
    n_prompt = seq // CHUNK
    w_in, w_out, w_glu = big_w
    in_specs = [_ANY_SPEC] * 7 + [_const_spec(a.shape) for a in small_w]
    out_shape = (
        jax.ShapeDtypeStruct(((n_prompt + 1) * TILE_ROWS, D_MODEL), F32),
        jax.ShapeDtypeStruct((nb, N_STATES), F32),
        jax.ShapeDtypeStruct((nb, N_STATES), F32),
        jax.ShapeDtypeStruct((nb, CHUNK, CM_WIDTH), F32),
        jax.ShapeDtypeStruct((dec_batch, N_STATES), F32),
        jax.ShapeDtypeStruct((dec_batch, N_STATES), F32),
        jax.ShapeDtypeStruct((dec_batch, dec_seq, CM_WIDTH), F32),
    )
    out_specs = (pl.BlockSpec((TILE_ROWS, D_MODEL), lambda i: (i, 0)),) + tuple(
        _const_spec(o.shape) for o in out_shape[1:])
    scratch = [
        pltpu.VMEM((2, CHUNK, nb, D_MODEL), F32),
        pltpu.SemaphoreType.DMA((2, nb)),
        pltpu.VMEM((HALF_STATES // LANES, TILE_ROWS, LANES), F32),
        pltpu.VMEM((HALF_STATES // LANES, TILE_ROWS, LANES), F32),
        pltpu.VMEM((nb, N_STATES), F32),
        pltpu.VMEM((nb, N_STATES), F32),
        pltpu.VMEM((CM_WIDTH // LANES, TILE_ROWS, LANES), F32),
        pltpu.VMEM((CM_WIDTH // LANES, TILE_ROWS, LANES), F32),
        pltpu.VMEM((CM_WIDTH // LANES, TILE_ROWS, LANES), F32),
        pltpu.VMEM((TILE_ROWS, S5_WIDTH), F32),
        pltpu.VMEM((TILE_ROWS, HALF_IN), F32),
        pltpu.VMEM(w_in.shape, BF16),
        pltpu.VMEM(w_out.shape, BF16),
        pltpu.VMEM(w_glu.shape, BF16),
        pltpu.VMEM((2, CAST_ROWS, IN_WIDTH), F32),
        pltpu.SemaphoreType.DMA((2,)),
        pltpu.SemaphoreType.DMA((2,)),
    ]
    return pl.pallas_call(
        functools.partial(_mixer_kernel, n_prompt, dec_seq),
        grid=(n_prompt + 1,),
        in_specs=in_specs,
        out_specs=out_specs,
        out_shape=out_shape,
        scratch_shapes=scratch,
        compiler_params=pltpu.CompilerParams(
            dimension_semantics=("arbitrary",), vmem_limit_bytes=VMEM_LIMIT_BYTES),
        name="mixer",
    )(x_prompt, x_sample, *h0, w_in, w_out, w_glu, *small_w)


def _ffn_kernel(nb, n_prompt, x_ref, g2_ref, gf_ref, wg_hbm, wu_hbm, wd_hbm, yp_hbm, ys_hbm,
                yt_ref, out_sem, wg_ref, wu_ref, wd_ref, stage_gu_ref, stage_d_ref, w_sem):
    tt = TILE_ROWS // nb
    step = pl.program_id(0)
    n_steps = pl.num_programs(0)
    slot = step % 2
    n_chunks = D_FF // FF_CHUNK

    def prompt_writeback(s, sl):
        return [pltpu.make_async_copy(yt_ref.at[sl, :, b, :], yp_hbm.at[b, pl.ds(s * tt, tt), :],
                                      out_sem.at[sl, b]) for b in range(nb)]

    def sample_writeback(sl):
        return [pltpu.make_async_copy(yt_ref.at[sl], ys_hbm, out_sem.at[sl, 0])]

    def wait_writeback(s, sl):
        @pl.when(s < n_prompt)
        def _():
            for cp in prompt_writeback(s, sl):
                cp.wait()

        @pl.when(s >= n_prompt)
        def _():
            for cp in sample_writeback(sl):
                cp.wait()

    def weight_chunk_copies(c, sl):
        cols = pl.ds(c * FF_CHUNK, FF_CHUNK)
        return [pltpu.make_async_copy(wg_hbm.at[:, cols], stage_gu_ref.at[sl, 0], w_sem.at[sl, 0]),
                pltpu.make_async_copy(wu_hbm.at[:, cols], stage_gu_ref.at[sl, 1], w_sem.at[sl, 1]),
                pltpu.make_async_copy(wd_hbm.at[cols, :], stage_d_ref.at[sl], w_sem.at[sl, 2])]

    def stage_in_chunk(c):
        sl = c % 2
        cols = slice(c * FF_CHUNK, (c + 1) * FF_CHUNK)
        for cp in weight_chunk_copies(c, sl):
            cp.wait()
        wg_ref[:, cols] = stage_gu_ref[sl, 0].astype(BF16)
        wu_ref[:, cols] = stage_gu_ref[sl, 1].astype(BF16)
        wd_ref[cols, :] = stage_d_ref[sl].astype(BF16)
        if c + 2 < n_chunks:
            for cp in weight_chunk_copies(c + 2, sl):
                cp.start()

    def ffn_tile(before_chunk):
        halves = [slice(0, TILE_ROWS // 2), slice(TILE_ROWS // 2, TILE_ROWS)]
        hs = [_rms(x_ref[rsl, :], g2_ref[...]).astype(BF16) for rsl in halves]
        accs = [x_ref[rsl, :] for rsl in halves]
        for c in range(n_chunks):
            cols = slice(c * FF_CHUNK, (c + 1) * FF_CHUNK)
            if before_chunk is not None:
                before_chunk(c)
            for i in range(len(halves)):
                gate = _dot(hs[i], wg_ref[:, cols])
                up = _dot(hs[i], wu_ref[:, cols])
                accs[i] = accs[i] + _dot((jax.nn.silu(gate) * up).astype(BF16), wd_ref[cols, :])
        for i, rsl in enumerate(halves):
            groups = slice(rsl.start // nb, rsl.stop // nb)
            yt_ref[slot, groups] = _rms(accs[i], gf_ref[...]).reshape(groups.stop - groups.start, nb, D_MODEL)

    @pl.when(step == 0)
    def _():
        for c in range(2):
            for cp in weight_chunk_copies(c, c):
                cp.start()
        ffn_tile(stage_in_chunk)

    @pl.when(step >= 2)
    def _():
        wait_writeback(step - 2, slot)

    @pl.when(step > 0)
    def _():
        ffn_tile(None)

    @pl.when(step < n_prompt)
    def _():
        for cp in prompt_writeback(step, slot):
            cp.start()

    @pl.when(step >= n_prompt)
    def _():
        for cp in sample_writeback(slot):
            cp.start()

    @pl.when(step == n_steps - 1)
    def _():
        wait_writeback(step - 1, 1 - slot)
        wait_writeback(step, slot)


def _ffn(x1, g2, gf, ffn_w, *, nb, seq):
    n = x1.shape[0]
    n_prompt = nb * seq // TILE_ROWS
    assert n == (n_prompt + 1) * TILE_ROWS and TILE_ROWS % nb == 0 and n_prompt >= 1
    tt = TILE_ROWS // nb
    w_gate, w_up, w_down = ffn_w
    return pl.pallas_call(
        functools.partial(_ffn_kernel, nb, n_prompt),
        grid=(n_prompt + 1,),
        in_specs=[pl.BlockSpec((TILE_ROWS, D_MODEL), lambda i: (i, 0)),
                  _const_spec(g2.shape), _const_spec(gf.shape)] + [_ANY_SPEC] * 3,
        out_specs=(_ANY_SPEC, _ANY_SPEC),
        out_shape=(jax.ShapeDtypeStruct((nb, seq, D_MODEL), F32),
                   jax.ShapeDtypeStruct((tt, nb, D_MODEL), F32)),
        scratch_shapes=[pltpu.VMEM((2, tt, nb, D_MODEL), F32),
                        pltpu.SemaphoreType.DMA((2, nb)),
                        pltpu.VMEM(w_gate.shape, BF16),
                        pltpu.VMEM(w_up.shape, BF16),
                        pltpu.VMEM(w_down.shape, BF16),
                        pltpu.VMEM((2, 2, D_MODEL, FF_CHUNK), F32),
                        pltpu.VMEM((2, FF_CHUNK, D_MODEL), F32),
                        pltpu.SemaphoreType.DMA((2, 3))],
        compiler_params=pltpu.CompilerParams(
            dimension_semantics=("arbitrary",), vmem_limit_bytes=VMEM_LIMIT_BYTES),
        name="ffn",
    )(x1, g2, gf, w_gate, w_up, w_down)


def kernel(x_prompt, x_sample, state_s5_re, state_s5_im, norm1, w_in, lam_re, lam_im, log_dt, b_re, b_im, c_re, c_im, d_skip, w_glu, b_glu, cm_ln_g, cm_ln_b, w_s, b_s, g_s5, g_cm, w_out, norm2, w_gate, w_up, w_down, norm_f):
    depth = norm1.shape[0]
    assert depth == 1
    l = 0
    batch, seq, _ = x_prompt.shape
    dec_batch, dec_seq, _ = x_sample.shape

    ar, ai, bdr, bdi, cdr, cdi = pl.pallas_call(
        _prep_kernel,
        out_shape=(
            jax.ShapeDtypeStruct((S5_GROUPS, S5_STATE), F32),
            jax.ShapeDtypeStruct((S5_GROUPS, S5_STATE), F32),
            jax.ShapeDtypeStruct((S5_HALVES, HALF_IN, HALF_STATES), BF16),
            jax.ShapeDtypeStruct((S5_HALVES, HALF_IN, HALF_STATES), BF16),
            jax.ShapeDtypeStruct((S5_HALVES, HALF_STATES, HALF_IN), BF16),
            jax.ShapeDtypeStruct((S5_HALVES, HALF_STATES, HALF_IN), BF16),
        ),
        name="s5_prepare",
    )(lam_re[l], lam_im[l], log_dt[l][:, None], b_re[l], b_im[l],
      c_re[l].reshape(S5_WIDTH, S5_STATE), c_im[l].reshape(S5_WIDTH, S5_STATE))

    row = lambda a: a.reshape(1, -1)
    wmix_p = w_s[l].reshape(CM_HEADS // 2, 2 * CHUNK, CHUNK)
    bmix_p = jnp.repeat(b_s[l].T, CM_HEAD_DIM, axis=1)
    wmix_s = jnp.repeat(jnp.transpose(w_s[l][:, :dec_seq, :dec_seq], (1, 2, 0)), CM_HEAD_DIM, axis=2)
    bmix_s = bmix_p[:dec_seq]
    small_w = [row(norm1[l]), bdr, bdi, cdr, cdi, row(ar), row(ai), row(d_skip[l]), row(b_glu[l]),
               row(cm_ln_g[l]), row(cm_ln_b[l]), row(g_s5[l]), row(g_cm[l]),
               wmix_p, bmix_p, wmix_s, bmix_s]
    h0 = (state_s5_re[l].reshape(dec_batch, N_STATES), state_s5_im[l].reshape(dec_batch, N_STATES))

    x1, pr, pi_, pv, sr, si, sv = _mixer(x_prompt, x_sample, h0, (w_in[l], w_out[l], w_glu[l]), small_w)
    y_prompt, y_sample = _ffn(x1, row(norm2[l]), row(norm_f), (w_gate[l], w_up[l], w_down[l]),
                              nb=batch, seq=seq)

    st = lambda a, n: a.reshape(1, n, S5_GROUPS, S5_STATE)
    return (y_prompt, y_sample.reshape(dec_batch, dec_seq, D_MODEL),
            st(pr, batch), st(pi_, batch), pv[None],
            st(sr, dec_batch), st(si, dec_batch), sv[None])
```

```python
import functools

import jax
import jax.numpy as jnp
from jax import lax
from jax.experimental import pallas as pl
from jax.experimental.pallas import tpu as pltpu

F32 = jnp.float32
BF16 = jnp.bfloat16
HIGHEST = lax.Precision.HIGHEST

D_MODEL = 1024
S5_WIDTH = 512
S5_GROUP = 16
S5_GROUPS = 32
S5_STATE = 64
N_STATES = S5_GROUPS * S5_STATE
CM_WIDTH = 512
CM_HEADS = 8
CM_HEAD_DIM = 64
CHUNK = 128
IN_WIDTH = S5_WIDTH + 2 * CM_WIDTH
D_FF = 2816
EPS = 1e-6

S5_HALVES = 2
HALF_GROUPS = S5_GROUPS // S5_HALVES
HALF_IN = S5_WIDTH // S5_HALVES
HALF_STATES = N_STATES // S5_HALVES
SUBLANES = 8
LANES = 128
SCAN_LANES = 512
SCAN_BLOCKS = SCAN_LANES // LANES
MXU_COLS = 256
TILE_ROWS = SUBLANES * CHUNK
ROW_BLOCK = 256
FF_CHUNK = 256
FF_STAGE_SLOTS = 4
VMEM_LIMIT_BYTES = 58 * 1024 * 1024


def _rms(x, g):
    r = lax.rsqrt(jnp.mean(x * x, axis=-1, keepdims=True) + EPS)
    return x * r * g


def _dot(a, b):
    return jnp.dot(a, b, preferred_element_type=F32)


def _load_lane_blocks(ref, n_blocks):
    return jnp.concatenate([ref[k] for k in range(n_blocks)], axis=1)


def _cast_weights(weights, slots, sems):
    plan = []
    for w_hbm, w_bf_ref in weights:
        n_rows, n_cols = w_hbm.shape
        for k in range(n_cols // LANES):
            i = len(plan)
            stage = slots[i].at[pl.ds(0, n_rows), :]
            copy = pltpu.make_async_copy(w_hbm.at[:, pl.ds(k * LANES, LANES)], stage, sems.at[i])
            plan.append((copy, stage, w_bf_ref, k))
    for copy, _, _, _ in plan:
        copy.start()
    for copy, stage, w_bf_ref, k in plan:
        copy.wait()
        w_bf_ref[:, k * LANES:(k + 1) * LANES] = stage[...].astype(BF16)


def _prep_kernel(lr_ref, li_ref, ldt_ref, br_ref, bi_ref, cr_ref, ci_ref,
                 ar_ref, ai_ref, bdr_ref, bdi_ref, cdr_ref, cdi_ref):
    dt = jnp.exp(ldt_ref[...])
    lr = lr_ref[...]
    li = li_ref[...]
    mag = jnp.exp(lr * dt)
    ar = mag * jnp.cos(li * dt)
    ai = mag * jnp.sin(li * dt)
    den = lr * lr + li * li
    qr = ((ar - 1.0) * lr + ai * li) / den
    qi = (ai * lr - (ar - 1.0) * li) / den
    ar_ref[...] = ar
    ai_ref[...] = ai
    qr_t = qr.T
    qi_t = qi.T

    def iota(shape, dim):
        return lax.broadcasted_iota(jnp.int32, shape, dim)

    rep_b = (iota((S5_GROUP, HALF_IN), 1) % S5_GROUP == iota((S5_GROUP, HALF_IN), 0)).astype(F32)
    rep_c = (iota((S5_STATE, HALF_STATES), 1) % S5_STATE == iota((S5_STATE, HALF_STATES), 0)).astype(F32)
    diag_b = iota((HALF_STATES, HALF_IN), 0) // S5_STATE == iota((HALF_STATES, HALF_IN), 1) // S5_GROUP
    diag_c = iota((HALF_IN, HALF_STATES), 0) // S5_GROUP == iota((HALF_IN, HALF_STATES), 1) // S5_STATE
    for hv in range(S5_HALVES):
        groups = range(hv * HALF_GROUPS, (hv + 1) * HALF_GROUPS)
        bb_r = jnp.concatenate([qr_t[:, g:g + 1] * br_ref[g] - qi_t[:, g:g + 1] * bi_ref[g]
                                for g in groups], axis=0)
        bb_i = jnp.concatenate([qr_t[:, g:g + 1] * bi_ref[g] + qi_t[:, g:g + 1] * br_ref[g]
                                for g in groups], axis=0)
        for bb, o_ref in ((bb_r, bdr_ref), (bb_i, bdi_ref)):
            wide = jnp.dot(bb, rep_b, precision=HIGHEST, preferred_element_type=F32)
            o_ref[hv] = jnp.where(diag_b, wide, 0.0).T.astype(BF16)
        for c_ref, o_ref in ((cr_ref, cdr_ref), (ci_ref, cdi_ref)):
            c_half = c_ref[hv * HALF_IN:(hv + 1) * HALF_IN, :]
            wide = jnp.dot(c_half, rep_c, precision=HIGHEST, preferred_element_type=F32)
            o_ref[hv] = jnp.where(diag_c, wide, 0.0).T.astype(BF16)


def _cm_uv(p_cm, lng_ref, lnb_ref):
    uv = jax.nn.gelu(p_cm)
    u = uv[:, :CM_WIDTH]
    vv = uv[:, CM_WIDTH:]
    mu = jnp.mean(vv, axis=-1, keepdims=True)
    vc = vv - mu
    v = vc * lax.rsqrt(jnp.mean(vc * vc, axis=-1, keepdims=True) + EPS) * lng_ref[...] + lnb_ref[...]
    return u, v


def _scan_lanes(bur_ref, bui_ref, ar_ref, ai_ref, hf, lc, h_init, rows_at, n_steps, unrolled):
    blocks = range(lc * SCAN_BLOCKS, (lc + 1) * SCAN_BLOCKS)
    glanes = slice(hf * HALF_STATES + lc * SCAN_LANES, hf * HALF_STATES + (lc + 1) * SCAN_LANES)
    a_r = jnp.broadcast_to(ar_ref[:, glanes], (SUBLANES, SCAN_LANES))
    a_i = jnp.broadcast_to(ai_ref[:, glanes], (SUBLANES, SCAN_LANES))

    def scan_step(t, carry):
        hr, hi = carry
        trows = rows_at(t)
        b_r = jnp.concatenate([bur_ref[k, trows, :] for k in blocks], axis=1)
        b_i = jnp.concatenate([bui_ref[k, trows, :] for k in blocks], axis=1)
        nr = a_r * hr - a_i * hi + b_r
        ni = a_r * hi + a_i * hr + b_i
        for n, k in enumerate(blocks):
            bur_ref[k, trows, :] = nr[:, n * LANES:(n + 1) * LANES]
            bui_ref[k, trows, :] = ni[:, n * LANES:(n + 1) * LANES]
        return nr, ni

    if unrolled:
        carry = h_init
        for t in range(n_steps):
            carry = scan_step(t, carry)
        return carry
    return lax.fori_loop(0, n_steps, scan_step, h_init, unroll=8)


def _mixer_kernel(n_prompt, dec_seq,
                  xp_hbm, xs_hbm, h0r_hbm, h0i_hbm, win_hbm, wout_hbm, wglu_hbm,
                  g1_ref, bdr_ref, bdi_ref, cdr_ref, cdi_ref, ar_ref, ai_ref, dskip_ref, bglu_ref,
                  lng_ref, lnb_ref, gs5_ref, gcm_ref, wmixp_ref, bmixp_ref, wmixs_ref, bmixs_ref,
                  x1_ref, pr_ref, pi_ref, pv_ref, sr_ref, si_ref, sv_ref,
                  xt_ref, in_sem, bur_ref, bui_ref, hsr_ref, hsi_ref, cma_ref, cmb_ref, cmc_ref,
                  xs_ref, y0_ref, win_ref, wout_ref, wglu_ref, w_sem, h0_sem):
    nb = SUBLANES
    rows = TILE_ROWS
    step = pl.program_id(0)
    slot = step % 2

    def fetch_prompt(s, sl):
        return [pltpu.make_async_copy(xp_hbm.at[b, pl.ds(s * CHUNK, CHUNK), :], xt_ref.at[sl, :, b, :],
                                      in_sem.at[sl, b]) for b in range(nb)]

    def fetch_sample(sl):
        return pltpu.make_async_copy(xs_hbm, xt_ref.at[sl], in_sem.at[sl, 0])

    @pl.when(step == 0)
    def _():
        for cp in fetch_prompt(0, 0):
            cp.start()
        staging = [ref.at[k] for ref in (bur_ref, bui_ref, cma_ref, cmb_ref, cmc_ref)
                   for k in range(ref.shape[0])]
        _cast_weights([(win_hbm, win_ref), (wout_hbm, wout_ref), (wglu_hbm, wglu_ref)], staging, w_sem)
        h0_copies = [pltpu.make_async_copy(h0r_hbm, sr_ref, h0_sem.at[0]),
                     pltpu.make_async_copy(h0i_hbm, si_ref, h0_sem.at[1])]
        for cp in h0_copies:
            cp.start()
        for cp in h0_copies:
            cp.wait()
        hsr_ref[...] = jnp.zeros_like(hsr_ref)
        hsi_ref[...] = jnp.zeros_like(hsi_ref)

    @pl.when(step + 1 < n_prompt)
    def _():
        for cp in fetch_prompt(step + 1, 1 - slot):
            cp.start()

    @pl.when(step + 1 == n_prompt)
    def _():
        fetch_sample(1 - slot).start()

    @pl.when(step < n_prompt)
    def _():
        for cp in fetch_prompt(step, slot):
            cp.wait()

    @pl.when(step == n_prompt)
    def _():
        fetch_sample(slot).wait()

    is_prompt = step < n_prompt
    is_sample = step == n_prompt
    cm_blocks = CM_WIDTH // LANES
    mxu_blocks = MXU_COLS // LANES
    us_ref, vs_ref, mx_ref = cma_ref, cmb_ref, cmc_ref
    row_blocks = [slice(r * ROW_BLOCK, (r + 1) * ROW_BLOCK) for r in range(rows // ROW_BLOCK)]

    def x_rows(rsl):
        groups = slice(rsl.start // nb, rsl.stop // nb)
        return xt_ref[slot, groups].reshape(ROW_BLOCK, D_MODEL)

    def b_proj(hf, xs_h, rsl):
        for n in range(HALF_STATES // MXU_COLS):
            cols = slice(n * MXU_COLS, (n + 1) * MXU_COLS)
            b_r = _dot(xs_h, bdr_ref[hf, :, cols])
            b_i = _dot(xs_h, bdi_ref[hf, :, cols])
            for q in range(mxu_blocks):
                bur_ref[n * mxu_blocks + q, rsl, :] = b_r[:, q * LANES:(q + 1) * LANES]
                bui_ref[n * mxu_blocks + q, rsl, :] = b_i[:, q * LANES:(q + 1) * LANES]

    def c_proj(hf, rsl):
        y_h = None
        for n in range(HALF_STATES // MXU_COLS):
            krows = slice(n * MXU_COLS, (n + 1) * MXU_COLS)
            blocks = range(n * mxu_blocks, (n + 1) * mxu_blocks)
            h_r = jnp.concatenate([bur_ref[k, rsl, :] for k in blocks], axis=1).astype(BF16)
            h_i = jnp.concatenate([bui_ref[k, rsl, :] for k in blocks], axis=1).astype(BF16)
            part = _dot(h_r, cdr_ref[hf, krows, :]) - _dot(h_i, cdi_ref[hf, krows, :])
            y_h = part if y_h is None else y_h + part
        return y_h

    def scan_half(hf):
        @pl.when(is_prompt)
        def _():
            for lc in range(HALF_STATES // SCAN_LANES):
                glanes = slice(hf * HALF_STATES + lc * SCAN_LANES, hf * HALF_STATES + (lc + 1) * SCAN_LANES)
                hr, hi = _scan_lanes(bur_ref, bui_ref, ar_ref, ai_ref, hf, lc,
                                     (hsr_ref[:, glanes], hsi_ref[:, glanes]),
                                     lambda t: pl.ds(pl.multiple_of(t * nb, nb), nb), CHUNK, False)
                hsr_ref[:, glanes] = hr
                hsi_ref[:, glanes] = hi
                pr_ref[:, glanes] = hr
                pi_ref[:, glanes] = hi

        @pl.when(is_sample)
        def _():
            tt = dec_seq
            for sg in range(rows // tt // SUBLANES):
                srows = slice(sg * SUBLANES, (sg + 1) * SUBLANES)
                for lc in range(HALF_STATES // SCAN_LANES):
                    glanes = slice(hf * HALF_STATES + lc * SCAN_LANES,
                                   hf * HALF_STATES + (lc + 1) * SCAN_LANES)
                    hr, hi = _scan_lanes(bur_ref, bui_ref, ar_ref, ai_ref, hf, lc,
                                         (sr_ref[srows, glanes], si_ref[srows, glanes]),
                                         lambda t, sg=sg: pl.ds(sg * SUBLANES * tt + t, SUBLANES, stride=tt),
                                         tt, True)
                    sr_ref[srows, glanes] = hr
                    si_ref[srows, glanes] = hi

    for rsl in row_blocks:
        h = _rms(x_rows(rsl), g1_ref[...]).astype(BF16)
        u, v = _cm_uv(_dot(h, win_ref[:, S5_WIDTH:]), lng_ref, lnb_ref)
        for k in range(cm_blocks):
            us_ref[k, rsl, :] = u[:, k * LANES:(k + 1) * LANES]
            vs_ref[k, rsl, :] = v[:, k * LANES:(k + 1) * LANES]
        xs = _dot(h, win_ref[:, :S5_WIDTH])
        xs_ref[rsl, :] = xs
        b_proj(0, xs[:, :HALF_IN].astype(BF16), rsl)

    @pl.when(is_prompt)
    def _():
        r_idx = lax.broadcasted_iota(jnp.int32, (2 * CHUNK, CHUNK), 0)
        c_idx = lax.broadcasted_iota(jnp.int32, (2 * CHUNK, CHUNK), 1)
        causal = (r_idx % CHUNK) >= c_idx
        first_head = lax.broadcasted_iota(jnp.int32, (CHUNK, LANES), 1) < CM_HEAD_DIM
        for j in range(CM_HEADS // 2):
            w_pair = jnp.where(causal, wmixp_ref[j], 0.0).astype(BF16)
            bias = bmixp_ref[:, j * LANES:(j + 1) * LANES]
            for b in range(nb):
                seq_rows = pl.ds(b, CHUNK, stride=nb)
                v_seq = vs_ref[j, seq_rows, :]
                pv_ref[b, :, j * LANES:(j + 1) * LANES] = v_seq
                r = _dot(w_pair, v_seq.astype(BF16))
                mixed = jnp.where(first_head, r[:CHUNK], r[CHUNK:]) + bias
                mx_ref[j, seq_rows, :] = us_ref[j, seq_rows, :] * mixed

    @pl.when(is_sample)
    def _():
        tt = dec_seq
        n_seq = rows // tt
        sv_ref[...] = _load_lane_blocks(vs_ref, cm_blocks).reshape(n_seq, tt, CM_WIDTH)
        for k in range(cm_blocks):
            lanes = slice(k * LANES, (k + 1) * LANES)
            v_at = [vs_ref[k, pl.ds(s, n_seq, stride=tt), :] for s in range(tt)]
            for t in range(tt):
                acc = bmixs_ref[t:t + 1, lanes] + wmixs_ref[t, 0:1, lanes] * v_at[0]
                for s in range(1, t + 1):
                    acc = acc + wmixs_ref[t, s:s + 1, lanes] * v_at[s]
                trows = pl.ds(t, n_seq, stride=tt)
                mx_ref[k, trows, :] = us_ref[k, trows, :] * acc

    scan_half(0)
    for rsl in row_blocks:
        y0_ref[rsl, :] = c_proj(0, rsl)
        b_proj(1, xs_ref[rsl, HALF_IN:].astype(BF16), rsl)
        o_cm = jnp.concatenate([mx_ref[k, rsl, :] for k in range(cm_blocks)], axis=1)
        n_cm = _rms(o_cm, gcm_ref[...]).astype(BF16)
        x1_ref[rsl, :] = x_rows(rsl) + _dot(n_cm, wout_ref[S5_WIDTH:, :])
    scan_half(1)
    for rsl in row_blocks:
        y = jnp.concatenate([y0_ref[rsl, :], c_proj(1, rsl)], axis=1) + dskip_ref[...] * xs_ref[rsl, :]
        g = jax.nn.gelu(y)
        o_s5 = g * jax.nn.sigmoid(_dot(g.astype(BF16), wglu_ref[...]) + bglu_ref[...])
        n_s5 = _rms(o_s5, gs5_ref[...]).astype(BF16)
        x1_ref[rsl, :] = x1_ref[rsl, :] + _dot(n_s5, wout_ref[:S5_WIDTH, :])


def _const_spec(shape):
    nd = len(shape)
    return pl.BlockSpec(shape, lambda i, nd=nd: (0,) * nd, pipeline_mode=pl.Buffered(1))


_ANY_SPEC = pl.BlockSpec(memory_space=pl.ANY)


def _mixer(x_prompt, x_sample, h0, big_w, small_w):
    nb, seq, _ = x_prompt.shape
    dec_batch, dec_seq, _ = x_sample.shape
    assert nb == SUBLANES and seq % CHUNK == 0
    assert dec_batch * dec_seq == TILE_ROWS and dec_seq <= SUBLANES and dec_batch == CHUNK
    n_prompt = seq // CHUNK
    w_in, w_out, w_glu = big_w
    in_specs = [_ANY_SPEC] * 7 + [_const_spec(a.shape) for a in small_w]
    out_shape = (
        jax.ShapeDtypeStruct(((n_prompt + 1) * TILE_ROWS, D_MODEL), F32),
        jax.ShapeDtypeStruct((nb, N_STATES), F32),
        jax.ShapeDtypeStruct((nb, N_STATES), F32),
        jax.ShapeDtypeStruct((nb, CHUNK, CM_WIDTH), F32),
        jax.ShapeDtypeStruct((dec_batch, N_STATES), F32),
        jax.ShapeDtypeStruct((dec_batch, N_STATES), F32),
        jax.ShapeDtypeStruct((dec_batch, dec_seq, CM_WIDTH), F32),
    )
    out_specs = (pl.BlockSpec((TILE_ROWS, D_MODEL), lambda i: (i, 0)),) + tuple(
        _const_spec(o.shape) for o in out_shape[1:])
    scratch = [
        pltpu.VMEM((2, CHUNK, nb, D_MODEL), F32),
        pltpu.SemaphoreType.DMA((2, nb)),
        pltpu.VMEM((HALF_STATES // LANES, TILE_ROWS, LANES), F32),
        pltpu.VMEM((HALF_STATES // LANES, TILE_ROWS, LANES), F32),
        pltpu.VMEM((nb, N_STATES), F32),
        pltpu.VMEM((nb, N_STATES), F32),
        pltpu.VMEM((CM_WIDTH // LANES, TILE_ROWS, LANES), F32),
        pltpu.VMEM((CM_WIDTH // LANES, TILE_ROWS, LANES), F32),
        pltpu.VMEM((CM_WIDTH // LANES, TILE_ROWS, LANES), F32),
        pltpu.VMEM((TILE_ROWS, S5_WIDTH), F32),
        pltpu.VMEM((TILE_ROWS, HALF_IN), F32),
        pltpu.VMEM(w_in.shape, BF16),
        pltpu.VMEM(w_out.shape, BF16),
        pltpu.VMEM(w_glu.shape, BF16),
        pltpu.SemaphoreType.DMA(((w_in.shape[1] + w_out.shape[1] + w_glu.shape[1]) // LANES,)),
        pltpu.SemaphoreType.DMA((2,)),
    ]
    return pl.pallas_call(
        functools.partial(_mixer_kernel, n_prompt, dec_seq),
        grid=(n_prompt + 1,),
        in_specs=in_specs,
        out_specs=out_specs,
        out_shape=out_shape,
        scratch_shapes=scratch,
        compiler_params=pltpu.CompilerParams(
            dimension_semantics=("arbitrary",), vmem_limit_bytes=VMEM_LIMIT_BYTES),
        name="mixer",
    )(x_prompt, x_sample, *h0, w_in, w_out, w_glu, *small_w)


def _ffn_kernel(nb, n_prompt, x_ref, g2_ref, gf_ref, wg_hbm, wu_hbm, wd_hbm, yp_hbm, ys_hbm,
                yt_ref, out_sem, wg_ref, wu_ref, wd_ref, stage_gu_ref, stage_d_ref, w_sem):
    tt = TILE_ROWS // nb
    step = pl.program_id(0)
    n_steps = pl.num_programs(0)
    slot = step % 2
    n_chunks = D_FF // FF_CHUNK

    def prompt_writeback(s, sl):
        return [pltpu.make_async_copy(yt_ref.at[sl, :, b, :], yp_hbm.at[b, pl.ds(s * tt, tt), :],
                                      out_sem.at[sl, b]) for b in range(nb)]

    def sample_writeback(sl):
        return [pltpu.make_async_copy(yt_ref.at[sl], ys_hbm, out_sem.at[sl, 0])]

    def wait_writeback(s, sl):
        @pl.when(s < n_prompt)
        def _():
            for cp in prompt_writeback(s, sl):
                cp.wait()

        @pl.when(s >= n_prompt)
        def _():
            for cp in sample_writeback(sl):
                cp.wait()

    def weight_chunk_copies(c, sl):
        cols = pl.ds(c * FF_CHUNK, FF_CHUNK)
        return [pltpu.make_async_copy(wg_hbm.at[:, cols], stage_gu_ref.at[sl, 0], w_sem.at[sl, 0]),
                pltpu.make_async_copy(wu_hbm.at[:, cols], stage_gu_ref.at[sl, 1], w_sem.at[sl, 1]),
                pltpu.make_async_copy(wd_hbm.at[cols, :], stage_d_ref.at[sl], w_sem.at[sl, 2])]

    def stage_in_chunk(c):
        sl = c % FF_STAGE_SLOTS
        cols = slice(c * FF_CHUNK, (c + 1) * FF_CHUNK)
        for cp in weight_chunk_copies(c, sl):
            cp.wait()
        wg_ref[:, cols] = stage_gu_ref[sl, 0].astype(BF16)
        wu_ref[:, cols] = stage_gu_ref[sl, 1].astype(BF16)
        wd_ref[cols, :] = stage_d_ref[sl].astype(BF16)
        if c + FF_STAGE_SLOTS < n_chunks:
            for cp in weight_chunk_copies(c + FF_STAGE_SLOTS, sl):
                cp.start()

    def ffn_tile(before_chunk):
        halves = [slice(0, TILE_ROWS // 2), slice(TILE_ROWS // 2, TILE_ROWS)]
        hs = [_rms(x_ref[rsl, :], g2_ref[...]).astype(BF16) for rsl in halves]
        accs = [x_ref[rsl, :] for rsl in halves]
        for c in range(n_chunks):
            cols = slice(c * FF_CHUNK, (c + 1) * FF_CHUNK)
            if before_chunk is not None:
                before_chunk(c)
            for i in range(len(halves)):
                gate = _dot(hs[i], wg_ref[:, cols])
                up = _dot(hs[i], wu_ref[:, cols])
                accs[i] = accs[i] + _dot((jax.nn.silu(gate) * up).astype(BF16), wd_ref[cols, :])
        for i, rsl in enumerate(halves):
            groups = slice(rsl.start // nb, rsl.stop // nb)
            yt_ref[slot, groups] = _rms(accs[i], gf_ref[...]).reshape(groups.stop - groups.start, nb, D_MODEL)

    @pl.when(step == 0)
    def _():
        for c in range(FF_STAGE_SLOTS):
            for cp in weight_chunk_copies(c, c):
                cp.start()
        ffn_tile(stage_in_chunk)

    @pl.when(step >= 2)
    def _():
        wait_writeback(step - 2, slot)

    @pl.when(step > 0)
    def _():
        ffn_tile(None)

    @pl.when(step < n_prompt)
    def _():
        for cp in prompt_writeback(step, slot):
            cp.start()

    @pl.when(step >= n_prompt)
    def _():
        for cp in sample_writeback(slot):
            cp.start()

    @pl.when(step == n_steps - 1)
    def _():
        wait_writeback(step - 1, 1 - slot)
        wait_writeback(step, slot)


def _ffn(x1, g2, gf, ffn_w, *, nb, seq):
    n = x1.shape[0]
    n_prompt = nb * seq // TILE_ROWS
    assert n == (n_prompt + 1) * TILE_ROWS and TILE_ROWS % nb == 0 and n_prompt >= 1
    tt = TILE_ROWS // nb
    w_gate, w_up, w_down = ffn_w
    return pl.pallas_call(
        functools.partial(_ffn_kernel, nb, n_prompt),
        grid=(n_prompt + 1,),
        in_specs=[pl.BlockSpec((TILE_ROWS, D_MODEL), lambda i: (i, 0)),
                  _const_spec(g2.shape), _const_spec(gf.shape)] + [_ANY_SPEC] * 3,
        out_specs=(_ANY_SPEC, _ANY_SPEC),
        out_shape=(jax.ShapeDtypeStruct((nb, seq, D_MODEL), F32),
                   jax.ShapeDtypeStruct((tt, nb, D_MODEL), F32)),
        scratch_shapes=[pltpu.VMEM((2, tt, nb, D_MODEL), F32),
                        pltpu.SemaphoreType.DMA((2, nb)),
                        pltpu.VMEM(w_gate.shape, BF16),
                        pltpu.VMEM(w_up.shape, BF16),
                        pltpu.VMEM(w_down.shape, BF16),
                        pltpu.VMEM((FF_STAGE_SLOTS, 2, D_MODEL, FF_CHUNK), F32),
                        pltpu.VMEM((FF_STAGE_SLOTS, FF_CHUNK, D_MODEL), F32),
                        pltpu.SemaphoreType.DMA((FF_STAGE_SLOTS, 3))],
        compiler_params=pltpu.CompilerParams(
            dimension_semantics=("arbitrary",), vmem_limit_bytes=VMEM_LIMIT_BYTES),
        name="ffn",
    )(x1, g2, gf, w_gate, w_up, w_down)


def kernel(x_prompt, x_sample, state_s5_re, state_s5_im, norm1, w_in, lam_re, lam_im, log_dt, b_re, b_im, c_re, c_im, d_skip, w_glu, b_glu, cm_ln_g, cm_ln_b, w_s, b_s, g_s5, g_cm, w_out, norm2, w_gate, w_up, w_down, norm_f):
    depth = norm1.shape[0]
    assert depth == 1
    l = 0
    batch, seq, _ = x_prompt.shape
    dec_batch, dec_seq, _ = x_sample.shape

    ar, ai, bdr, bdi, cdr, cdi = pl.pallas_call(
        _prep_kernel,
        out_shape=(
            jax.ShapeDtypeStruct((S5_GROUPS, S5_STATE), F32),
            jax.ShapeDtypeStruct((S5_GROUPS, S5_STATE), F32),
            jax.ShapeDtypeStruct((S5_HALVES, HALF_IN, HALF_STATES), BF16),
            jax.ShapeDtypeStruct((S5_HALVES, HALF_IN, HALF_STATES), BF16),
            jax.ShapeDtypeStruct((S5_HALVES, HALF_STATES, HALF_IN), BF16),
            jax.ShapeDtypeStruct((S5_HALVES, HALF_STATES, HALF_IN), BF16),
        ),
        name="s5_prepare",
    )(lam_re[l], lam_im[l], log_dt[l][:, None], b_re[l], b_im[l],
      c_re[l].reshape(S5_WIDTH, S5_STATE), c_im[l].reshape(S5_WIDTH, S5_STATE))

    row = lambda a: a.reshape(1, -1)
    wmix_p = w_s[l].reshape(CM_HEADS // 2, 2 * CHUNK, CHUNK)
    bmix_p = jnp.repeat(b_s[l].T, CM_HEAD_DIM, axis=1)
    wmix_s = jnp.repeat(jnp.transpose(w_s[l][:, :dec_seq, :dec_seq], (1, 2, 0)), CM_HEAD_DIM, axis=2)
    bmix_s = bmix_p[:dec_seq]
    small_w = [row(norm1[l]), bdr, bdi, cdr, cdi, row(ar), row(ai), row(d_skip[l]), row(b_glu[l]),
               row(cm_ln_g[l]), row(cm_ln_b[l]), row(g_s5[l]), row(g_cm[l]),
               wmix_p, bmix_p, wmix_s, bmix_s]
    h0 = (state_s5_re[l].reshape(dec_batch, N_STATES), state_s5_im[l].reshape(dec_batch, N_STATES))

    x1, pr, pi_, pv, sr, si, sv = _mixer(x_prompt, x_sample, h0, (w_in[l], w_out[l], w_glu[l]), small_w)
    y_prompt, y_sample = _ffn(x1, row(norm2[l]), row(norm_f), (w_gate[l], w_up[l], w_down[l]),
                              nb=batch, seq=seq)

    st = lambda a, n: a.reshape(1, n, S5_GROUPS, S5_STATE)
    return (y_prompt, y_sample.reshape(dec_batch, dec_seq, D_MODEL),
            st(pr, batch), st(pi_, batch), pv[None],
            st(sr, dec_batch), st(si, dec_batch), sv[None])
```

```python
import functools

import jax
import jax.numpy as jnp
from jax import lax
from jax.experimental import pallas as pl
from jax.experimental.pallas import tpu as pltpu

F32 = jnp.float32
BF16 = jnp.bfloat16

D_MODEL = 1024
S5_WIDTH = 512
S5_GROUP = 16
S5_GROUPS = 32
S5_STATE = 64
N_STATES = S5_GROUPS * S5_STATE
CM_WIDTH = 512
CM_HEADS = 8
CM_HEAD_DIM = 64
CHUNK = 128
IN_WIDTH = S5_WIDTH + 2 * CM_WIDTH
D_FF = 2816
EPS = 1e-6

S5_HALVES = 2
HALF_GROUPS = S5_GROUPS // S5_HALVES
HALF_IN = S5_WIDTH // S5_HALVES
HALF_STATES = N_STATES // S5_HALVES
SUBLANES = 8
LANES = 128
SCAN_LANES = 512
SCAN_BLOCKS = SCAN_LANES // LANES
MXU_COLS = 256
TILE_ROWS = SUBLANES * CHUNK
ROW_BLOCK = 256
FF_CHUNK = 256
FF_STAGE_SLOTS = 4
VMEM_LIMIT_BYTES = 58 * 1024 * 1024


def _rms(x, g):
    r = lax.rsqrt(jnp.mean(x * x, axis=-1, keepdims=True) + EPS)
    return x * r * g


def _dot(a, b):
    return jnp.dot(a, b, preferred_element_type=F32)


def _load_lane_blocks(ref, n_blocks):
    return jnp.concatenate([ref[k] for k in range(n_blocks)], axis=1)


def _cast_weights(weights, slots, sems):
    plan = []
    for w_hbm, w_bf_ref in weights:
        n_rows, n_cols = w_hbm.shape
        for k in range(n_cols // LANES):
            i = len(plan)
            stage = slots[i].at[pl.ds(0, n_rows), :]
            copy = pltpu.make_async_copy(w_hbm.at[:, pl.ds(k * LANES, LANES)], stage, sems.at[i])
            plan.append((copy, stage, w_bf_ref, k))
    for copy, _, _, _ in plan:
        copy.start()
    for copy, stage, w_bf_ref, k in plan:
        copy.wait()
        w_bf_ref[:, k * LANES:(k + 1) * LANES] = stage[...].astype(BF16)


def _prep_kernel(lr_ref, li_ref, ldt_ref, br_ref, bi_ref, cr_ref, ci_ref,
                 ar_ref, ai_ref, bdr_ref, bdi_ref, cdr_ref, cdi_ref):
    dt = jnp.exp(ldt_ref[...])
    lr = lr_ref[...]
    li = li_ref[...]
    mag = jnp.exp(lr * dt)
    ar = mag * jnp.cos(li * dt)
    ai = mag * jnp.sin(li * dt)
    den = lr * lr + li * li
    qr = ((ar - 1.0) * lr + ai * li) / den
    qi = (ai * lr - (ar - 1.0) * li) / den
    ar_ref[...] = ar
    ai_ref[...] = ai
    qr_t = qr.T
    qi_t = qi.T

    def iota(shape, dim):
        return lax.broadcasted_iota(jnp.int32, shape, dim)

    rep_b = (iota((S5_GROUP, HALF_IN), 1) % S5_GROUP == iota((S5_GROUP, HALF_IN), 0)).astype(BF16)
    rep_c = (iota((S5_STATE, HALF_STATES), 1) % S5_STATE == iota((S5_STATE, HALF_STATES), 0)).astype(BF16)

    def repeat_lanes(x, rep):
        hi = x.astype(BF16)
        rest = x - hi.astype(F32)
        mid = rest.astype(BF16)
        lo = (rest - mid.astype(F32)).astype(BF16)
        return _dot(hi, rep) + _dot(mid, rep) + _dot(lo, rep)

    diag_b = iota((HALF_STATES, HALF_IN), 0) // S5_STATE == iota((HALF_STATES, HALF_IN), 1) // S5_GROUP
    diag_c = iota((HALF_IN, HALF_STATES), 0) // S5_GROUP == iota((HALF_IN, HALF_STATES), 1) // S5_STATE
    for hv in range(S5_HALVES):
        groups = range(hv * HALF_GROUPS, (hv + 1) * HALF_GROUPS)
        bb_r = jnp.concatenate([qr_t[:, g:g + 1] * br_ref[g] - qi_t[:, g:g + 1] * bi_ref[g]
                                for g in groups], axis=0)
        bb_i = jnp.concatenate([qr_t[:, g:g + 1] * bi_ref[g] + qi_t[:, g:g + 1] * br_ref[g]
                                for g in groups], axis=0)
        for bb, o_ref in ((bb_r, bdr_ref), (bb_i, bdi_ref)):
            wide = repeat_lanes(bb, rep_b)
            o_ref[hv] = jnp.where(diag_b, wide, 0.0).T.astype(BF16)
        for c_ref, o_ref in ((cr_ref, cdr_ref), (ci_ref, cdi_ref)):
            c_half = c_ref[hv * HALF_IN:(hv + 1) * HALF_IN, :]
            wide = repeat_lanes(c_half, rep_c)
            o_ref[hv] = jnp.where(diag_c, wide, 0.0).T.astype(BF16)


def _cm_uv(p_cm, lng_ref, lnb_ref):
    uv = jax.nn.gelu(p_cm)
    u = uv[:, :CM_WIDTH]
    vv = uv[:, CM_WIDTH:]
    mu = jnp.mean(vv, axis=-1, keepdims=True)
    vc = vv - mu
    v = vc * lax.rsqrt(jnp.mean(vc * vc, axis=-1, keepdims=True) + EPS) * lng_ref[...] + lnb_ref[...]
    return u, v


def _scan_lanes(bur_ref, bui_ref, ar_ref, ai_ref, hf, lc, h_init, rows_at, n_steps, unrolled):
    blocks = range(lc * SCAN_BLOCKS, (lc + 1) * SCAN_BLOCKS)
    glanes = slice(hf * HALF_STATES + lc * SCAN_LANES, hf * HALF_STATES + (lc + 1) * SCAN_LANES)
    a_r = jnp.broadcast_to(ar_ref[:, glanes], (SUBLANES, SCAN_LANES))
    a_i = jnp.broadcast_to(ai_ref[:, glanes], (SUBLANES, SCAN_LANES))

    def scan_step(t, carry):
        hr, hi = carry
        trows = rows_at(t)
        b_r = jnp.concatenate([bur_ref[k, trows, :] for k in blocks], axis=1)
        b_i = jnp.concatenate([bui_ref[k, trows, :] for k in blocks], axis=1)
        nr = a_r * hr - a_i * hi + b_r
        ni = a_r * hi + a_i * hr + b_i
        for n, k in enumerate(blocks):
            bur_ref[k, trows, :] = nr[:, n * LANES:(n + 1) * LANES]
            bui_ref[k, trows, :] = ni[:, n * LANES:(n + 1) * LANES]
        return nr, ni

    if unrolled:
        carry = h_init
        for t in range(n_steps):
            carry = scan_step(t, carry)
        return carry
    return lax.fori_loop(0, n_steps, scan_step, h_init, unroll=8)


def _mixer_kernel(n_prompt, dec_seq,
                  xp_hbm, xs_hbm, h0r_hbm, h0i_hbm, win_hbm, wout_hbm, wglu_hbm,
                  g1_ref, bdr_ref, bdi_ref, cdr_ref, cdi_ref, ar_ref, ai_ref, dskip_ref, bglu_ref,
                  lng_ref, lnb_ref, gs5_ref, gcm_ref, wmixp_ref, bmixp_ref, wmixs_ref, bmixs_ref,
                  x1_ref, pr_ref, pi_ref, pv_ref, sr_ref, si_ref, sv_ref,
                  xt_ref, in_sem, bur_ref, bui_ref, hsr_ref, hsi_ref, cma_ref, cmb_ref, cmc_ref,
                  xs_ref, y0_ref, win_ref, wout_ref, wglu_ref, w_sem, h0_sem):
    nb = SUBLANES
    rows = TILE_ROWS
    step = pl.program_id(0)
    slot = step % 2

    def fetch_prompt(s, sl):
        return [pltpu.make_async_copy(xp_hbm.at[b, pl.ds(s * CHUNK, CHUNK), :], xt_ref.at[sl, :, b, :],
                                      in_sem.at[sl, b]) for b in range(nb)]

    def fetch_sample(sl):
        return pltpu.make_async_copy(xs_hbm, xt_ref.at[sl], in_sem.at[sl, 0])

    @pl.when(step == 0)
    def _():
        for cp in fetch_prompt(0, 0):
            cp.start()
        staging = [ref.at[k] for ref in (bur_ref, bui_ref, cma_ref, cmb_ref, cmc_ref)
                   for k in range(ref.shape[0])]
        _cast_weights([(win_hbm, win_ref), (wout_hbm, wout_ref), (wglu_hbm, wglu_ref)], staging, w_sem)
        h0_copies = [pltpu.make_async_copy(h0r_hbm, sr_ref, h0_sem.at[0]),
                     pltpu.make_async_copy(h0i_hbm, si_ref, h0_sem.at[1])]
        for cp in h0_copies:
            cp.start()
        for cp in h0_copies:
            cp.wait()
        hsr_ref[...] = jnp.zeros_like(hsr_ref)
        hsi_ref[...] = jnp.zeros_like(hsi_ref)

    @pl.when(step + 1 < n_prompt)
    def _():
        for cp in fetch_prompt(step + 1, 1 - slot):
            cp.start()

    @pl.when(step + 1 == n_prompt)
    def _():
        fetch_sample(1 - slot).start()

    @pl.when(step < n_prompt)
    def _():
        for cp in fetch_prompt(step, slot):
            cp.wait()

    @pl.when(step == n_prompt)
    def _():
        fetch_sample(slot).wait()

    is_prompt = step < n_prompt
    is_sample = step == n_prompt
    cm_blocks = CM_WIDTH // LANES
    mxu_blocks = MXU_COLS // LANES
    us_ref, vs_ref, mx_ref = cma_ref, cmb_ref, cmc_ref
    row_blocks = [slice(r * ROW_BLOCK, (r + 1) * ROW_BLOCK) for r in range(rows // ROW_BLOCK)]

    def x_rows(rsl):
        groups = slice(rsl.start // nb, rsl.stop // nb)
        return xt_ref[slot, groups].reshape(ROW_BLOCK, D_MODEL)

    def b_proj(hf, xs_h, rsl):
        for n in range(HALF_STATES // MXU_COLS):
            cols = slice(n * MXU_COLS, (n + 1) * MXU_COLS)
            b_r = _dot(xs_h, bdr_ref[hf, :, cols])
            b_i = _dot(xs_h, bdi_ref[hf, :, cols])
            for q in range(mxu_blocks):
                bur_ref[n * mxu_blocks + q, rsl, :] = b_r[:, q * LANES:(q + 1) * LANES]
                bui_ref[n * mxu_blocks + q, rsl, :] = b_i[:, q * LANES:(q + 1) * LANES]

    def c_proj(hf, rsl):
        y_h = None
        for n in range(HALF_STATES // MXU_COLS):
            krows = slice(n * MXU_COLS, (n + 1) * MXU_COLS)
            blocks = range(n * mxu_blocks, (n + 1) * mxu_blocks)
            h_r = jnp.concatenate([bur_ref[k, rsl, :] for k in blocks], axis=1).astype(BF16)
            h_i = jnp.concatenate([bui_ref[k, rsl, :] for k in blocks], axis=1).astype(BF16)
            part = _dot(h_r, cdr_ref[hf, krows, :]) - _dot(h_i, cdi_ref[hf, krows, :])
            y_h = part if y_h is None else y_h + part
        return y_h

    def scan_half(hf):
        @pl.when(is_prompt)
        def _():
            for lc in range(HALF_STATES // SCAN_LANES):
                glanes = slice(hf * HALF_STATES + lc * SCAN_LANES, hf * HALF_STATES + (lc + 1) * SCAN_LANES)
                hr, hi = _scan_lanes(bur_ref, bui_ref, ar_ref, ai_ref, hf, lc,
                                     (hsr_ref[:, glanes], hsi_ref[:, glanes]),
                                     lambda t: pl.ds(pl.multiple_of(t * nb, nb), nb), CHUNK, False)
                hsr_ref[:, glanes] = hr
                hsi_ref[:, glanes] = hi
                pr_ref[:, glanes] = hr
                pi_ref[:, glanes] = hi

        @pl.when(is_sample)
        def _():
            tt = dec_seq
            for sg in range(rows // tt // SUBLANES):
                srows = slice(sg * SUBLANES, (sg + 1) * SUBLANES)
                for lc in range(HALF_STATES // SCAN_LANES):
                    glanes = slice(hf * HALF_STATES + lc * SCAN_LANES,
                                   hf * HALF_STATES + (lc + 1) * SCAN_LANES)
                    hr, hi = _scan_lanes(bur_ref, bui_ref, ar_ref, ai_ref, hf, lc,
                                         (sr_ref[srows, glanes], si_ref[srows, glanes]),
                                         lambda t, sg=sg: pl.ds(sg * SUBLANES * tt + t, SUBLANES, stride=tt),
                                         tt, True)
                    sr_ref[srows, glanes] = hr
                    si_ref[srows, glanes] = hi

    for rsl in row_blocks:
        h = _rms(x_rows(rsl), g1_ref[...]).astype(BF16)
        u, v = _cm_uv(_dot(h, win_ref[:, S5_WIDTH:]), lng_ref, lnb_ref)
        for k in range(cm_blocks):
            us_ref[k, rsl, :] = u[:, k * LANES:(k + 1) * LANES]
            vs_ref[k, rsl, :] = v[:, k * LANES:(k + 1) * LANES]
        xs = _dot(h, win_ref[:, :S5_WIDTH])
        xs_ref[rsl, :] = xs
        b_proj(0, xs[:, :HALF_IN].astype(BF16), rsl)

    @pl.when(is_prompt)
    def _():
        r_idx = lax.broadcasted_iota(jnp.int32, (2 * CHUNK, CHUNK), 0)
        c_idx = lax.broadcasted_iota(jnp.int32, (2 * CHUNK, CHUNK), 1)
        causal = (r_idx % CHUNK) >= c_idx
        first_head = lax.broadcasted_iota(jnp.int32, (CHUNK, LANES), 1) < CM_HEAD_DIM
        for j in range(CM_HEADS // 2):
            w_pair = jnp.where(causal, wmixp_ref[j], 0.0).astype(BF16)
            bias = bmixp_ref[:, j * LANES:(j + 1) * LANES]
            for b in range(nb):
                seq_rows = pl.ds(b, CHUNK, stride=nb)
                v_seq = vs_ref[j, seq_rows, :]
                pv_ref[b, :, j * LANES:(j + 1) * LANES] = v_seq
                r = _dot(w_pair, v_seq.astype(BF16))
                mixed = jnp.where(first_head, r[:CHUNK], r[CHUNK:]) + bias
                mx_ref[j, seq_rows, :] = us_ref[j, seq_rows, :] * mixed

    @pl.when(is_sample)
    def _():
        tt = dec_seq
        n_seq = rows // tt
        sv_ref[...] = _load_lane_blocks(vs_ref, cm_blocks).reshape(n_seq, tt, CM_WIDTH)
        for k in range(cm_blocks):
            lanes = slice(k * LANES, (k + 1) * LANES)
            v_at = [vs_ref[k, pl.ds(s, n_seq, stride=tt), :] for s in range(tt)]
            for t in range(tt):
                acc = bmixs_ref[t:t + 1, lanes] + wmixs_ref[t, 0:1, lanes] * v_at[0]
                for s in range(1, t + 1):
                    acc = acc + wmixs_ref[t, s:s + 1, lanes] * v_at[s]
                trows = pl.ds(t, n_seq, stride=tt)
                mx_ref[k, trows, :] = us_ref[k, trows, :] * acc

    scan_half(0)
    for rsl in row_blocks:
        y0_ref[rsl, :] = c_proj(0, rsl)
        b_proj(1, xs_ref[rsl, HALF_IN:].astype(BF16), rsl)
        o_cm = jnp.concatenate([mx_ref[k, rsl, :] for k in range(cm_blocks)], axis=1)
        n_cm = _rms(o_cm, gcm_ref[...]).astype(BF16)
        x1_ref[rsl, :] = x_rows(rsl) + _dot(n_cm, wout_ref[S5_WIDTH:, :])
    scan_half(1)
    for rsl in row_blocks:
        y = jnp.concatenate([y0_ref[rsl, :], c_proj(1, rsl)], axis=1) + dskip_ref[...] * xs_ref[rsl, :]
        g = jax.nn.gelu(y)
        o_s5 = g * jax.nn.sigmoid(_dot(g.astype(BF16), wglu_ref[...]) + bglu_ref[...])
        n_s5 = _rms(o_s5, gs5_ref[...]).astype(BF16)
        x1_ref[rsl, :] = x1_ref[rsl, :] + _dot(n_s5, wout_ref[:S5_WIDTH, :])


def _const_spec(shape):
    nd = len(shape)
    return pl.BlockSpec(shape, lambda i, nd=nd: (0,) * nd, pipeline_mode=pl.Buffered(1))


_ANY_SPEC = pl.BlockSpec(memory_space=pl.ANY)


def _mixer(x_prompt, x_sample, h0, big_w, small_w):
    nb, seq, _ = x_prompt.shape
    dec_batch, dec_seq, _ = x_sample.shape
    assert nb == SUBLANES and seq % CHUNK == 0
    assert dec_batch * dec_seq == TILE_ROWS and dec_seq <= SUBLANES and dec_batch == CHUNK
    n_prompt = seq // CHUNK
    w_in, w_out, w_glu = big_w
    in_specs = [_ANY_SPEC] * 7 + [_const_spec(a.shape) for a in small_w]
    out_shape = (
        jax.ShapeDtypeStruct(((n_prompt + 1) * TILE_ROWS, D_MODEL), F32),
        jax.ShapeDtypeStruct((nb, N_STATES), F32),
        jax.ShapeDtypeStruct((nb, N_STATES), F32),
        jax.ShapeDtypeStruct((nb, CHUNK, CM_WIDTH), F32),
        jax.ShapeDtypeStruct((dec_batch, N_STATES), F32),
        jax.ShapeDtypeStruct((dec_batch, N_STATES), F32),
        jax.ShapeDtypeStruct((dec_batch, dec_seq, CM_WIDTH), F32),
    )
    out_specs = (pl.BlockSpec((TILE_ROWS, D_MODEL), lambda i: (i, 0)),) + tuple(
        _const_spec(o.shape) for o in out_shape[1:])
    scratch = [
        pltpu.VMEM((2, CHUNK, nb, D_MODEL), F32),
        pltpu.SemaphoreType.DMA((2, nb)),
        pltpu.VMEM((HALF_STATES // LANES, TILE_ROWS, LANES), F32),
        pltpu.VMEM((HALF_STATES // LANES, TILE_ROWS, LANES), F32),
        pltpu.VMEM((nb, N_STATES), F32),
        pltpu.VMEM((nb, N_STATES), F32),
        pltpu.VMEM((CM_WIDTH // LANES, TILE_ROWS, LANES), F32),
        pltpu.VMEM((CM_WIDTH // LANES, TILE_ROWS, LANES), F32),
        pltpu.VMEM((CM_WIDTH // LANES, TILE_ROWS, LANES), F32),
        pltpu.VMEM((TILE_ROWS, S5_WIDTH), F32),
        pltpu.VMEM((TILE_ROWS, HALF_IN), F32),
        pltpu.VMEM(w_in.shape, BF16),
        pltpu.VMEM(w_out.shape, BF16),
        pltpu.VMEM(w_glu.shape, BF16),
        pltpu.SemaphoreType.DMA(((w_in.shape[1] + w_out.shape[1] + w_glu.shape[1]) // LANES,)),
        pltpu.SemaphoreType.DMA((2,)),
    ]
    return pl.pallas_call(
        functools.partial(_mixer_kernel, n_prompt, dec_seq),
        grid=(n_prompt + 1,),
        in_specs=in_specs,
        out_specs=out_specs,
        out_shape=out_shape,
        scratch_shapes=scratch,
        compiler_params=pltpu.CompilerParams(
            dimension_semantics=("arbitrary",), vmem_limit_bytes=VMEM_LIMIT_BYTES),
        name="mixer",
    )(x_prompt, x_sample, *h0, w_in, w_out, w_glu, *small_w)


def _ffn_kernel(nb, n_prompt, x_ref, g2_ref, gf_ref, wg_hbm, wu_hbm, wd_hbm, yp_hbm, ys_hbm,
                yt_ref, out_sem, wg_ref, wu_ref, wd_ref, stage_gu_ref, stage_d_ref, w_sem):
    tt = TILE_ROWS // nb
    step = pl.program_id(0)
    n_steps = pl.num_programs(0)
    slot = step % 2
    n_chunks = D_FF // FF_CHUNK

    def prompt_writeback(s, sl):
        return [pltpu.make_async_copy(yt_ref.at[sl, :, b, :], yp_hbm.at[b, pl.ds(s * tt, tt), :],
                                      out_sem.at[sl, b]) for b in range(nb)]

    def sample_writeback(sl):
        return [pltpu.make_async_copy(yt_ref.at[sl], ys_hbm, out_sem.at[sl, 0])]

    def wait_writeback(s, sl):
        @pl.when(s < n_prompt)
        def _():
            for cp in prompt_writeback(s, sl):
                cp.wait()

        @pl.when(s >= n_prompt)
        def _():
            for cp in sample_writeback(sl):
                cp.wait()

    def weight_chunk_copies(c, sl):
        cols = pl.ds(c * FF_CHUNK, FF_CHUNK)
        return [pltpu.make_async_copy(wg_hbm.at[:, cols], stage_gu_ref.at[sl, 0], w_sem.at[sl, 0]),
                pltpu.make_async_copy(wu_hbm.at[:, cols], stage_gu_ref.at[sl, 1], w_sem.at[sl, 1]),
                pltpu.make_async_copy(wd_hbm.at[cols, :], stage_d_ref.at[sl], w_sem.at[sl, 2])]

    def stage_in_chunk(c):
        sl = c % FF_STAGE_SLOTS
        cols = slice(c * FF_CHUNK, (c + 1) * FF_CHUNK)
        for cp in weight_chunk_copies(c, sl):
            cp.wait()
        wg_ref[:, cols] = stage_gu_ref[sl, 0].astype(BF16)
        wu_ref[:, cols] = stage_gu_ref[sl, 1].astype(BF16)
        wd_ref[cols, :] = stage_d_ref[sl].astype(BF16)
        if c + FF_STAGE_SLOTS < n_chunks:
            for cp in weight_chunk_copies(c + FF_STAGE_SLOTS, sl):
                cp.start()

    @pl.when(step == 0)
    def _():
        for c in range(FF_STAGE_SLOTS):
            for cp in weight_chunk_copies(c, c):
                cp.start()
        for c in range(n_chunks):
            stage_in_chunk(c)

    @pl.when(step >= 2)
    def _():
        wait_writeback(step - 2, slot)

    halves = [slice(0, TILE_ROWS // 2), slice(TILE_ROWS // 2, TILE_ROWS)]
    hs = [_rms(x_ref[rsl, :], g2_ref[...]).astype(BF16) for rsl in halves]
    accs = [x_ref[rsl, :] for rsl in halves]
    for c in range(n_chunks):
        cols = slice(c * FF_CHUNK, (c + 1) * FF_CHUNK)
        for i in range(len(halves)):
            gate = _dot(hs[i], wg_ref[:, cols])
            up = _dot(hs[i], wu_ref[:, cols])
            accs[i] = accs[i] + _dot((jax.nn.silu(gate) * up).astype(BF16), wd_ref[cols, :])
    for i, rsl in enumerate(halves):
        groups = slice(rsl.start // nb, rsl.stop // nb)
        yt_ref[slot, groups] = _rms(accs[i], gf_ref[...]).reshape(groups.stop - groups.start, nb, D_MODEL)

    @pl.when(step < n_prompt)
    def _():
        for cp in prompt_writeback(step, slot):
            cp.start()

    @pl.when(step >= n_prompt)
    def _():
        for cp in sample_writeback(slot):
            cp.start()

    @pl.when(step == n_steps - 1)
    def _():
        wait_writeback(step - 1, 1 - slot)
        wait_writeback(step, slot)


def _ffn(x1, g2, gf, ffn_w, *, nb, seq):
    n = x1.shape[0]
    n_prompt = nb * seq // TILE_ROWS
    assert n == (n_prompt + 1) * TILE_ROWS and TILE_ROWS % nb == 0 and n_prompt >= 1
    tt = TILE_ROWS // nb
    w_gate, w_up, w_down = ffn_w
    return pl.pallas_call(
        functools.partial(_ffn_kernel, nb, n_prompt),
        grid=(n_prompt + 1,),
        in_specs=[pl.BlockSpec((TILE_ROWS, D_MODEL), lambda i: (i, 0)),
                  _const_spec(g2.shape), _const_spec(gf.shape)] + [_ANY_SPEC] * 3,
        out_specs=(_ANY_SPEC, _ANY_SPEC),
        out_shape=(jax.ShapeDtypeStruct((nb, seq, D_MODEL), F32),
                   jax.ShapeDtypeStruct((tt, nb, D_MODEL), F32)),
        scratch_shapes=[pltpu.VMEM((2, tt, nb, D_MODEL), F32),
                        pltpu.SemaphoreType.DMA((2, nb)),
                        pltpu.VMEM(w_gate.shape, BF16),
                        pltpu.VMEM(w_up.shape, BF16),
                        pltpu.VMEM(w_down.shape, BF16),
                        pltpu.VMEM((FF_STAGE_SLOTS, 2, D_MODEL, FF_CHUNK), F32),
                        pltpu.VMEM((FF_STAGE_SLOTS, FF_CHUNK, D_MODEL), F32),
                        pltpu.SemaphoreType.DMA((FF_STAGE_SLOTS, 3))],
        compiler_params=pltpu.CompilerParams(
            dimension_semantics=("arbitrary",), vmem_limit_bytes=VMEM_LIMIT_BYTES),
        name="ffn",
    )(x1, g2, gf, w_gate, w_up, w_down)


def kernel(x_prompt, x_sample, state_s5_re, state_s5_im, norm1, w_in, lam_re, lam_im, log_dt, b_re, b_im, c_re, c_im, d_skip, w_glu, b_glu, cm_ln_g, cm_ln_b, w_s, b_s, g_s5, g_cm, w_out, norm2, w_gate, w_up, w_down, norm_f):
    depth = norm1.shape[0]
    assert depth == 1
    l = 0
    batch, seq, _ = x_prompt.shape
    dec_batch, dec_seq, _ = x_sample.shape

    ar, ai, bdr, bdi, cdr, cdi = pl.pallas_call(
        _prep_kernel,
        out_shape=(
            jax.ShapeDtypeStruct((S5_GROUPS, S5_STATE), F32),
            jax.ShapeDtypeStruct((S5_GROUPS, S5_STATE), F32),
            jax.ShapeDtypeStruct((S5_HALVES, HALF_IN, HALF_STATES), BF16),
            jax.ShapeDtypeStruct((S5_HALVES, HALF_IN, HALF_STATES), BF16),
            jax.ShapeDtypeStruct((S5_HALVES, HALF_STATES, HALF_IN), BF16),
            jax.ShapeDtypeStruct((S5_HALVES, HALF_STATES, HALF_IN), BF16),
        ),
        name="s5_prepare",
    )(lam_re[l], lam_im[l], log_dt[l][:, None], b_re[l], b_im[l],
      c_re[l].reshape(S5_WIDTH, S5_STATE), c_im[l].reshape(S5_WIDTH, S5_STATE))

    row = lambda a: a.reshape(1, -1)
    wmix_p = w_s[l].reshape(CM_HEADS // 2, 2 * CHUNK, CHUNK)
    bmix_p = jnp.repeat(b_s[l].T, CM_HEAD_DIM, axis=1)
    wmix_s = jnp.repeat(jnp.transpose(w_s[l][:, :dec_seq, :dec_seq], (1, 2, 0)), CM_HEAD_DIM, axis=2)
    bmix_s = bmix_p[:dec_seq]
    small_w = [row(norm1[l]), bdr, bdi, cdr, cdi, row(ar), row(ai), row(d_skip[l]), row(b_glu[l]),
               row(cm_ln_g[l]), row(cm_ln_b[l]), row(g_s5[l]), row(g_cm[l]),
               wmix_p, bmix_p, wmix_s, bmix_s]
    h0 = (state_s5_re[l].reshape(dec_batch, N_STATES), state_s5_im[l].reshape(dec_batch, N_STATES))

    x1, pr, pi_, pv, sr, si, sv = _mixer(x_prompt, x_sample, h0, (w_in[l], w_out[l], w_glu[l]), small_w)
    y_prompt, y_sample = _ffn(x1, row(norm2[l]), row(norm_f), (w_gate[l], w_up[l], w_down[l]),
                              nb=batch, seq=seq)

    st = lambda a, n: a.reshape(1, n, S5_GROUPS, S5_STATE)
    return (y_prompt, y_sample.reshape(dec_batch, dec_seq, D_MODEL),
            st(pr, batch), st(pi_, batch), pv[None],
            st(sr, dec_batch), st(si, dec_batch), sv[None])
```

```python
import functools

import jax
import jax.numpy as jnp
from jax import lax
from jax.experimental import pallas as pl
from jax.experimental.pallas import tpu as pltpu

F32 = jnp.float32
BF16 = jnp.bfloat16

D_MODEL = 1024
S5_WIDTH = 512
S5_GROUP = 16
S5_GROUPS = 32
S5_STATE = 64
N_STATES = S5_GROUPS * S5_STATE
CM_WIDTH = 512
CM_HEADS = 8
CM_HEAD_DIM = 64
CHUNK = 128
IN_WIDTH = S5_WIDTH + 2 * CM_WIDTH
D_FF = 2816
EPS = 1e-6

S5_HALVES = 2
HALF_GROUPS = S5_GROUPS // S5_HALVES
HALF_IN = S5_WIDTH // S5_HALVES
HALF_STATES = N_STATES // S5_HALVES
SUBLANES = 8
LANES = 128
SCAN_LANES = 512
SCAN_BLOCKS = SCAN_LANES // LANES
MXU_COLS = 256
TILE_ROWS = SUBLANES * CHUNK
ROW_BLOCK = 256
FF_CHUNK = 256
FF_STAGE_SLOTS = 4
VMEM_LIMIT_BYTES = 58 * 1024 * 1024


def _rms(x, g):
    r = lax.rsqrt(jnp.mean(x * x, axis=-1, keepdims=True) + EPS)
    return x * r * g


def _dot(a, b):
    return jnp.dot(a, b, preferred_element_type=F32)


def _load_lane_blocks(ref, n_blocks):
    return jnp.concatenate([ref[k] for k in range(n_blocks)], axis=1)


def _cast_weights(weights, slots, sems):
    plan = []
    for w_hbm, w_bf_ref in weights:
        n_rows, n_cols = w_hbm.shape
        for k in range(n_cols // LANES):
            i = len(plan)
            stage = slots[i].at[pl.ds(0, n_rows), :]
            copy = pltpu.make_async_copy(w_hbm.at[:, pl.ds(k * LANES, LANES)], stage, sems.at[i])
            plan.append((copy, stage, w_bf_ref, k))
    for copy, _, _, _ in plan:
        copy.start()
    for copy, stage, w_bf_ref, k in plan:
        copy.wait()
        w_bf_ref[:, k * LANES:(k + 1) * LANES] = stage[...].astype(BF16)


def _prep_kernel(lr_ref, li_ref, ldt_ref, br_ref, bi_ref, cr_ref, ci_ref,
                 ar_ref, ai_ref, bdr_ref, bdi_ref, cdr_ref, cdi_ref):
    dt = jnp.exp(ldt_ref[...])
    lr = lr_ref[...]
    li = li_ref[...]
    mag = jnp.exp(lr * dt)
    ar = mag * jnp.cos(li * dt)
    ai = mag * jnp.sin(li * dt)
    den = lr * lr + li * li
    qr = ((ar - 1.0) * lr + ai * li) / den
    qi = (ai * lr - (ar - 1.0) * li) / den
    ar_ref[...] = ar
    ai_ref[...] = ai
    qr_t = qr.T
    qi_t = qi.T

    def iota(shape, dim):
        return lax.broadcasted_iota(jnp.int32, shape, dim)

    rep_b = (iota((S5_GROUP, HALF_IN), 1) % S5_GROUP == iota((S5_GROUP, HALF_IN), 0)).astype(BF16)
    rep_c = (iota((S5_STATE, HALF_STATES), 1) % S5_STATE == iota((S5_STATE, HALF_STATES), 0)).astype(BF16)

    def repeat_lanes(x, rep):
        hi = x.astype(BF16)
        rest = x - hi.astype(F32)
        mid = rest.astype(BF16)
        lo = (rest - mid.astype(F32)).astype(BF16)
        return _dot(hi, rep) + _dot(mid, rep) + _dot(lo, rep)

    diag_b = iota((HALF_STATES, HALF_IN), 0) // S5_STATE == iota((HALF_STATES, HALF_IN), 1) // S5_GROUP
    diag_c = iota((HALF_IN, HALF_STATES), 0) // S5_GROUP == iota((HALF_IN, HALF_STATES), 1) // S5_STATE
    for hv in range(S5_HALVES):
        groups = range(hv * HALF_GROUPS, (hv + 1) * HALF_GROUPS)
        bb_r = jnp.concatenate([qr_t[:, g:g + 1] * br_ref[g] - qi_t[:, g:g + 1] * bi_ref[g]
                                for g in groups], axis=0)
        bb_i = jnp.concatenate([qr_t[:, g:g + 1] * bi_ref[g] + qi_t[:, g:g + 1] * br_ref[g]
                                for g in groups], axis=0)
        for bb, o_ref in ((bb_r, bdr_ref), (bb_i, bdi_ref)):
            wide = repeat_lanes(bb, rep_b)
            o_ref[hv] = jnp.where(diag_b, wide, 0.0).T.astype(BF16)
        for c_ref, o_ref in ((cr_ref, cdr_ref), (ci_ref, cdi_ref)):
            c_half = c_ref[hv * HALF_IN:(hv + 1) * HALF_IN, :]
            wide = repeat_lanes(c_half, rep_c)
            o_ref[hv] = jnp.where(diag_c, wide, 0.0).T.astype(BF16)


def _cm_uv(p_cm, lng_ref, lnb_ref):
    uv = jax.nn.gelu(p_cm)
    u = uv[:, :CM_WIDTH]
    vv = uv[:, CM_WIDTH:]
    mu = jnp.mean(vv, axis=-1, keepdims=True)
    vc = vv - mu
    v = vc * lax.rsqrt(jnp.mean(vc * vc, axis=-1, keepdims=True) + EPS) * lng_ref[...] + lnb_ref[...]
    return u, v


def _scan_lanes(bur_ref, bui_ref, ar_ref, ai_ref, hf, lc, h_init, rows_at, n_steps, unrolled):
    blocks = range(lc * SCAN_BLOCKS, (lc + 1) * SCAN_BLOCKS)
    glanes = slice(hf * HALF_STATES + lc * SCAN_LANES, hf * HALF_STATES + (lc + 1) * SCAN_LANES)
    a_r = jnp.broadcast_to(ar_ref[:, glanes], (SUBLANES, SCAN_LANES))
    a_i = jnp.broadcast_to(ai_ref[:, glanes], (SUBLANES, SCAN_LANES))

    def scan_step(t, carry):
        hr, hi = carry
        trows = rows_at(t)
        b_r = jnp.concatenate([bur_ref[k, trows, :] for k in blocks], axis=1)
        b_i = jnp.concatenate([bui_ref[k, trows, :] for k in blocks], axis=1)
        nr = a_r * hr - a_i * hi + b_r
        ni = a_r * hi + a_i * hr + b_i
        for n, k in enumerate(blocks):
            bur_ref[k, trows, :] = nr[:, n * LANES:(n + 1) * LANES]
            bui_ref[k, trows, :] = ni[:, n * LANES:(n + 1) * LANES]
        return nr, ni

    if unrolled:
        carry = h_init
        for t in range(n_steps):
            carry = scan_step(t, carry)
        return carry
    return lax.fori_loop(0, n_steps, scan_step, h_init, unroll=8)


def _mixer_kernel(n_prompt, dec_seq,
                  xp_hbm, xs_hbm, h0r_hbm, h0i_hbm, win_hbm, wout_hbm, wglu_hbm,
                  g1_ref, bdr_ref, bdi_ref, cdr_ref, cdi_ref, ar_ref, ai_ref, dskip_ref, bglu_ref,
                  lng_ref, lnb_ref, gs5_ref, gcm_ref, wmixp_ref, bmixp_ref, wmixs_ref, bmixs_ref,
                  x1_ref, pr_ref, pi_ref, pv_ref, sr_ref, si_ref, sv_ref,
                  xt_ref, in_sem, bur_ref, bui_ref, hsr_ref, hsi_ref, cma_ref, cmb_ref, cmc_ref,
                  xs_ref, y0_ref, win_ref, wout_ref, wglu_ref, w_sem, h0_sem):
    nb = SUBLANES
    rows = TILE_ROWS
    step = pl.program_id(0)
    slot = step % 2

    def fetch_prompt(s, sl):
        return [pltpu.make_async_copy(xp_hbm.at[b, pl.ds(s * CHUNK, CHUNK), :], xt_ref.at[sl, :, b, :],
                                      in_sem.at[sl, b]) for b in range(nb)]

    def fetch_sample(sl):
        return pltpu.make_async_copy(xs_hbm, xt_ref.at[sl], in_sem.at[sl, 0])

    @pl.when(step == 0)
    def _():
        for cp in fetch_prompt(0, 0):
            cp.start()
        staging = [ref.at[k] for ref in (bur_ref, bui_ref, cma_ref, cmb_ref, cmc_ref)
                   for k in range(ref.shape[0])]
        _cast_weights([(win_hbm, win_ref), (wout_hbm, wout_ref), (wglu_hbm, wglu_ref)], staging, w_sem)
        h0_copies = [pltpu.make_async_copy(h0r_hbm, sr_ref, h0_sem.at[0]),
                     pltpu.make_async_copy(h0i_hbm, si_ref, h0_sem.at[1])]
        for cp in h0_copies:
            cp.start()
        for cp in h0_copies:
            cp.wait()
        hsr_ref[...] = jnp.zeros_like(hsr_ref)
        hsi_ref[...] = jnp.zeros_like(hsi_ref)

    @pl.when(step + 1 < n_prompt)
    def _():
        for cp in fetch_prompt(step + 1, 1 - slot):
            cp.start()

    @pl.when(step + 1 == n_prompt)
    def _():
        fetch_sample(1 - slot).start()

    @pl.when(step < n_prompt)
    def _():
        for cp in fetch_prompt(step, slot):
            cp.wait()

    @pl.when(step == n_prompt)
    def _():
        fetch_sample(slot).wait()

    is_prompt = step < n_prompt
    is_sample = step == n_prompt
    cm_blocks = CM_WIDTH // LANES
    mxu_blocks = MXU_COLS // LANES
    us_ref, vs_ref, mx_ref = cma_ref, cmb_ref, cmc_ref
    row_blocks = [slice(r * ROW_BLOCK, (r + 1) * ROW_BLOCK) for r in range(rows // ROW_BLOCK)]

    def x_rows(rsl):
        groups = slice(rsl.start // nb, rsl.stop // nb)
        return xt_ref[slot, groups].reshape(ROW_BLOCK, D_MODEL)

    def b_proj(hf, xs_h, rsl):
        for n in range(HALF_STATES // MXU_COLS):
            cols = slice(n * MXU_COLS, (n + 1) * MXU_COLS)
            b_r = _dot(xs_h, bdr_ref[hf, :, cols])
            b_i = _dot(xs_h, bdi_ref[hf, :, cols])
            for q in range(mxu_blocks):
                bur_ref[n * mxu_blocks + q, rsl, :] = b_r[:, q * LANES:(q + 1) * LANES]
                bui_ref[n * mxu_blocks + q, rsl, :] = b_i[:, q * LANES:(q + 1) * LANES]

    def c_proj(hf, rsl):
        y_h = None
        for n in range(HALF_STATES // MXU_COLS):
            krows = slice(n * MXU_COLS, (n + 1) * MXU_COLS)
            blocks = range(n * mxu_blocks, (n + 1) * mxu_blocks)
            h_r = jnp.concatenate([bur_ref[k, rsl, :] for k in blocks], axis=1).astype(BF16)
            h_i = jnp.concatenate([bui_ref[k, rsl, :] for k in blocks], axis=1).astype(BF16)
            part = _dot(h_r, cdr_ref[hf, krows, :]) - _dot(h_i, cdi_ref[hf, krows, :])
            y_h = part if y_h is None else y_h + part
        return y_h

    def scan_half(hf):
        @pl.when(is_prompt)
        def _():
            for lc in range(HALF_STATES // SCAN_LANES):
                glanes = slice(hf * HALF_STATES + lc * SCAN_LANES, hf * HALF_STATES + (lc + 1) * SCAN_LANES)
                hr, hi = _scan_lanes(bur_ref, bui_ref, ar_ref, ai_ref, hf, lc,
                                     (hsr_ref[:, glanes], hsi_ref[:, glanes]),
                                     lambda t: pl.ds(pl.multiple_of(t * nb, nb), nb), CHUNK, False)
                hsr_ref[:, glanes] = hr
                hsi_ref[:, glanes] = hi
                pr_ref[:, glanes] = hr
                pi_ref[:, glanes] = hi

        @pl.when(is_sample)
        def _():
            tt = dec_seq

            def scan_group(sg, carry):
                srows = pl.ds(pl.multiple_of(sg * SUBLANES, SUBLANES), SUBLANES)
                for lc in range(HALF_STATES // SCAN_LANES):
                    glanes = slice(hf * HALF_STATES + lc * SCAN_LANES,
                                   hf * HALF_STATES + (lc + 1) * SCAN_LANES)
                    hr, hi = _scan_lanes(bur_ref, bui_ref, ar_ref, ai_ref, hf, lc,
                                         (sr_ref[srows, glanes], si_ref[srows, glanes]),
                                         lambda t: pl.ds(sg * (SUBLANES * tt) + t, SUBLANES, stride=tt),
                                         tt, True)
                    sr_ref[srows, glanes] = hr
                    si_ref[srows, glanes] = hi
                return carry

            lax.fori_loop(0, rows // tt // SUBLANES, scan_group, 0)

    for rsl in row_blocks:
        h = _rms(x_rows(rsl), g1_ref[...]).astype(BF16)
        u, v = _cm_uv(_dot(h, win_ref[:, S5_WIDTH:]), lng_ref, lnb_ref)
        for k in range(cm_blocks):
            us_ref[k, rsl, :] = u[:, k * LANES:(k + 1) * LANES]
            vs_ref[k, rsl, :] = v[:, k * LANES:(k + 1) * LANES]
        xs = _dot(h, win_ref[:, :S5_WIDTH])
        xs_ref[rsl, :] = xs
        b_proj(0, xs[:, :HALF_IN].astype(BF16), rsl)

    @pl.when(is_prompt)
    def _():
        r_idx = lax.broadcasted_iota(jnp.int32, (2 * CHUNK, CHUNK), 0)
        c_idx = lax.broadcasted_iota(jnp.int32, (2 * CHUNK, CHUNK), 1)
        causal = (r_idx % CHUNK) >= c_idx
        first_head = lax.broadcasted_iota(jnp.int32, (CHUNK, LANES), 1) < CM_HEAD_DIM
        for j in range(CM_HEADS // 2):
            w_pair = jnp.where(causal, wmixp_ref[j], 0.0).astype(BF16)
            bias = bmixp_ref[:, j * LANES:(j + 1) * LANES]
            for b in range(nb):
                seq_rows = pl.ds(b, CHUNK, stride=nb)
                v_seq = vs_ref[j, seq_rows, :]
                pv_ref[b, :, j * LANES:(j + 1) * LANES] = v_seq
                r = _dot(w_pair, v_seq.astype(BF16))
                mixed = jnp.where(first_head, r[:CHUNK], r[CHUNK:]) + bias
                mx_ref[j, seq_rows, :] = us_ref[j, seq_rows, :] * mixed

    @pl.when(is_sample)
    def _():
        tt = dec_seq
        n_seq = rows // tt
        sv_ref[...] = _load_lane_blocks(vs_ref, cm_blocks).reshape(n_seq, tt, CM_WIDTH)
        def mix_block(k, carry):
            v_at = [vs_ref[k, pl.ds(s, n_seq, stride=tt), :] for s in range(tt)]
            for t in range(tt):
                acc = bmixs_ref[k, t:t + 1, :] + wmixs_ref[k, t, 0:1, :] * v_at[0]
                for s in range(1, t + 1):
                    acc = acc + wmixs_ref[k, t, s:s + 1, :] * v_at[s]
                trows = pl.ds(t, n_seq, stride=tt)
                mx_ref[k, trows, :] = us_ref[k, trows, :] * acc
            return carry

        lax.fori_loop(0, cm_blocks, mix_block, 0)

    scan_half(0)
    for rsl in row_blocks:
        y0_ref[rsl, :] = c_proj(0, rsl)
        b_proj(1, xs_ref[rsl, HALF_IN:].astype(BF16), rsl)
        o_cm = jnp.concatenate([mx_ref[k, rsl, :] for k in range(cm_blocks)], axis=1)
        n_cm = _rms(o_cm, gcm_ref[...]).astype(BF16)
        x1_ref[rsl, :] = x_rows(rsl) + _dot(n_cm, wout_ref[S5_WIDTH:, :])
    scan_half(1)
    for rsl in row_blocks:
        y = jnp.concatenate([y0_ref[rsl, :], c_proj(1, rsl)], axis=1) + dskip_ref[...] * xs_ref[rsl, :]
        g = jax.nn.gelu(y)
        o_s5 = g * jax.nn.sigmoid(_dot(g.astype(BF16), wglu_ref[...]) + bglu_ref[...])
        n_s5 = _rms(o_s5, gs5_ref[...]).astype(BF16)
        x1_ref[rsl, :] = x1_ref[rsl, :] + _dot(n_s5, wout_ref[:S5_WIDTH, :])


def _const_spec(shape):
    nd = len(shape)
    return pl.BlockSpec(shape, lambda i, nd=nd: (0,) * nd, pipeline_mode=pl.Buffered(1))


_ANY_SPEC = pl.BlockSpec(memory_space=pl.ANY)


def _mixer(x_prompt, x_sample, h0, big_w, small_w):
    nb, seq, _ = x_prompt.shape
    dec_batch, dec_seq, _ = x_sample.shape
    assert nb == SUBLANES and seq % CHUNK == 0
    assert dec_batch * dec_seq == TILE_ROWS and dec_seq <= SUBLANES and dec_batch == CHUNK
    n_prompt = seq // CHUNK
    w_in, w_out, w_glu = big_w
    in_specs = [_ANY_SPEC] * 7 + [_const_spec(a.shape) for a in small_w]
    out_shape = (
        jax.ShapeDtypeStruct(((n_prompt + 1) * TILE_ROWS, D_MODEL), F32),
        jax.ShapeDtypeStruct((nb, N_STATES), F32),
        jax.ShapeDtypeStruct((nb, N_STATES), F32),
        jax.ShapeDtypeStruct((nb, CHUNK, CM_WIDTH), F32),
        jax.ShapeDtypeStruct((dec_batch, N_STATES), F32),
        jax.ShapeDtypeStruct((dec_batch, N_STATES), F32),
        jax.ShapeDtypeStruct((dec_batch, dec_seq, CM_WIDTH), F32),
    )
    out_specs = (pl.BlockSpec((TILE_ROWS, D_MODEL), lambda i: (i, 0)),) + tuple(
        _const_spec(o.shape) for o in out_shape[1:])
    scratch = [
        pltpu.VMEM((2, CHUNK, nb, D_MODEL), F32),
        pltpu.SemaphoreType.DMA((2, nb)),
        pltpu.VMEM((HALF_STATES // LANES, TILE_ROWS, LANES), F32),
        pltpu.VMEM((HALF_STATES // LANES, TILE_ROWS, LANES), F32),
        pltpu.VMEM((nb, N_STATES), F32),
        pltpu.VMEM((nb, N_STATES), F32),
        pltpu.VMEM((CM_WIDTH // LANES, TILE_ROWS, LANES), F32),
        pltpu.VMEM((CM_WIDTH // LANES, TILE_ROWS, LANES), F32),
        pltpu.VMEM((CM_WIDTH // LANES, TILE_ROWS, LANES), F32),
        pltpu.VMEM((TILE_ROWS, S5_WIDTH), F32),
        pltpu.VMEM((TILE_ROWS, HALF_IN), F32),
        pltpu.VMEM(w_in.shape, BF16),
        pltpu.VMEM(w_out.shape, BF16),
        pltpu.VMEM(w_glu.shape, BF16),
        pltpu.SemaphoreType.DMA(((w_in.shape[1] + w_out.shape[1] + w_glu.shape[1]) // LANES,)),
        pltpu.SemaphoreType.DMA((2,)),
    ]
    return pl.pallas_call(
        functools.partial(_mixer_kernel, n_prompt, dec_seq),
        grid=(n_prompt + 1,),
        in_specs=in_specs,
        out_specs=out_specs,
        out_shape=out_shape,
        scratch_shapes=scratch,
        compiler_params=pltpu.CompilerParams(
            dimension_semantics=("arbitrary",), vmem_limit_bytes=VMEM_LIMIT_BYTES),
        name="mixer",
    )(x_prompt, x_sample, *h0, w_in, w_out, w_glu, *small_w)


def _ffn_kernel(nb, n_prompt, x_ref, g2_ref, gf_ref, wg_hbm, wu_hbm, wd_hbm, yp_hbm, ys_hbm,
                yt_ref, out_sem, wg_ref, wu_ref, wd_ref, stage_gu_ref, stage_d_ref, w_sem):
    tt = TILE_ROWS // nb
    step = pl.program_id(0)
    n_steps = pl.num_programs(0)
    slot = step % 2
    n_chunks = D_FF // FF_CHUNK

    def prompt_writeback(s, sl):
        return [pltpu.make_async_copy(yt_ref.at[sl, :, b, :], yp_hbm.at[b, pl.ds(s * tt, tt), :],
                                      out_sem.at[sl, b]) for b in range(nb)]

    def sample_writeback(sl):
        return [pltpu.make_async_copy(yt_ref.at[sl], ys_hbm, out_sem.at[sl, 0])]

    def wait_writeback(s, sl):
        @pl.when(s < n_prompt)
        def _():
            for cp in prompt_writeback(s, sl):
                cp.wait()

        @pl.when(s >= n_prompt)
        def _():
            for cp in sample_writeback(sl):
                cp.wait()

    def weight_chunk_copies(c, sl):
        cols = pl.ds(c * FF_CHUNK, FF_CHUNK)
        return [pltpu.make_async_copy(wg_hbm.at[:, cols], stage_gu_ref.at[sl, 0], w_sem.at[sl, 0]),
                pltpu.make_async_copy(wu_hbm.at[:, cols], stage_gu_ref.at[sl, 1], w_sem.at[sl, 1]),
                pltpu.make_async_copy(wd_hbm.at[cols, :], stage_d_ref.at[sl], w_sem.at[sl, 2])]

    def stage_in_chunk(c):
        sl = c % FF_STAGE_SLOTS
        cols = slice(c * FF_CHUNK, (c + 1) * FF_CHUNK)
        for cp in weight_chunk_copies(c, sl):
            cp.wait()
        wg_ref[:, cols] = stage_gu_ref[sl, 0].astype(BF16)
        wu_ref[:, cols] = stage_gu_ref[sl, 1].astype(BF16)
        wd_ref[cols, :] = stage_d_ref[sl].astype(BF16)
        if c + FF_STAGE_SLOTS < n_chunks:
            for cp in weight_chunk_copies(c + FF_STAGE_SLOTS, sl):
                cp.start()

    @pl.when(step == 0)
    def _():
        for c in range(FF_STAGE_SLOTS):
            for cp in weight_chunk_copies(c, c):
                cp.start()
        for c in range(n_chunks):
            stage_in_chunk(c)

    @pl.when(step >= 2)
    def _():
        wait_writeback(step - 2, slot)

    halves = [slice(0, TILE_ROWS // 2), slice(TILE_ROWS // 2, TILE_ROWS)]
    hs = [_rms(x_ref[rsl, :], g2_ref[...]).astype(BF16) for rsl in halves]
    accs = [x_ref[rsl, :] for rsl in halves]
    for c in range(n_chunks):
        cols = slice(c * FF_CHUNK, (c + 1) * FF_CHUNK)
        for i in range(len(halves)):
            gate = _dot(hs[i], wg_ref[:, cols])
            up = _dot(hs[i], wu_ref[:, cols])
            accs[i] = accs[i] + _dot((jax.nn.silu(gate) * up).astype(BF16), wd_ref[cols, :])
    for i, rsl in enumerate(halves):
        groups = slice(rsl.start // nb, rsl.stop // nb)
        yt_ref[slot, groups] = _rms(accs[i], gf_ref[...]).reshape(groups.stop - groups.start, nb, D_MODEL)

    @pl.when(step < n_prompt)
    def _():
        for cp in prompt_writeback(step, slot):
            cp.start()

    @pl.when(step >= n_prompt)
    def _():
        for cp in sample_writeback(slot):
            cp.start()

    @pl.when(step == n_steps - 1)
    def _():
        wait_writeback(step - 1, 1 - slot)
        wait_writeback(step, slot)


def _ffn(x1, g2, gf, ffn_w, *, nb, seq):
    n = x1.shape[0]
    n_prompt = nb * seq // TILE_ROWS
    assert n == (n_prompt + 1) * TILE_ROWS and TILE_ROWS % nb == 0 and n_prompt >= 1
    tt = TILE_ROWS // nb
    w_gate, w_up, w_down = ffn_w
    return pl.pallas_call(
        functools.partial(_ffn_kernel, nb, n_prompt),
        grid=(n_prompt + 1,),
        in_specs=[pl.BlockSpec((TILE_ROWS, D_MODEL), lambda i: (i, 0)),
                  _const_spec(g2.shape), _const_spec(gf.shape)] + [_ANY_SPEC] * 3,
        out_specs=(_ANY_SPEC, _ANY_SPEC),
        out_shape=(jax.ShapeDtypeStruct((nb, seq, D_MODEL), F32),
                   jax.ShapeDtypeStruct((tt, nb, D_MODEL), F32)),
        scratch_shapes=[pltpu.VMEM((2, tt, nb, D_MODEL), F32),
                        pltpu.SemaphoreType.DMA((2, nb)),
                        pltpu.VMEM(w_gate.shape, BF16),
                        pltpu.VMEM(w_up.shape, BF16),
                        pltpu.VMEM(w_down.shape, BF16),
                        pltpu.VMEM((FF_STAGE_SLOTS, 2, D_MODEL, FF_CHUNK), F32),
                        pltpu.VMEM((FF_STAGE_SLOTS, FF_CHUNK, D_MODEL), F32),
                        pltpu.SemaphoreType.DMA((FF_STAGE_SLOTS, 3))],
        compiler_params=pltpu.CompilerParams(
            dimension_semantics=("arbitrary",), vmem_limit_bytes=VMEM_LIMIT_BYTES),
        name="ffn",
    )(x1, g2, gf, w_gate, w_up, w_down)


def kernel(x_prompt, x_sample, state_s5_re, state_s5_im, norm1, w_in, lam_re, lam_im, log_dt, b_re, b_im, c_re, c_im, d_skip, w_glu, b_glu, cm_ln_g, cm_ln_b, w_s, b_s, g_s5, g_cm, w_out, norm2, w_gate, w_up, w_down, norm_f):
    depth = norm1.shape[0]
    assert depth == 1
    l = 0
    batch, seq, _ = x_prompt.shape
    dec_batch, dec_seq, _ = x_sample.shape

    ar, ai, bdr, bdi, cdr, cdi = pl.pallas_call(
        _prep_kernel,
        out_shape=(
            jax.ShapeDtypeStruct((S5_GROUPS, S5_STATE), F32),
            jax.ShapeDtypeStruct((S5_GROUPS, S5_STATE), F32),
            jax.ShapeDtypeStruct((S5_HALVES, HALF_IN, HALF_STATES), BF16),
            jax.ShapeDtypeStruct((S5_HALVES, HALF_IN, HALF_STATES), BF16),
            jax.ShapeDtypeStruct((S5_HALVES, HALF_STATES, HALF_IN), BF16),
            jax.ShapeDtypeStruct((S5_HALVES, HALF_STATES, HALF_IN), BF16),
        ),
        name="s5_prepare",
    )(lam_re[l], lam_im[l], log_dt[l][:, None], b_re[l], b_im[l],
      c_re[l].reshape(S5_WIDTH, S5_STATE), c_im[l].reshape(S5_WIDTH, S5_STATE))

    row = lambda a: a.reshape(1, -1)
    wmix_p = w_s[l].reshape(CM_HEADS // 2, 2 * CHUNK, CHUNK)
    bmix_p = jnp.repeat(b_s[l].T, CM_HEAD_DIM, axis=1)
    lane_blocks = lambda a: jnp.moveaxis(a.reshape(a.shape[:-1] + (CM_WIDTH // LANES, LANES)), -2, 0)
    wmix_s = lane_blocks(jnp.repeat(jnp.transpose(w_s[l][:, :dec_seq, :dec_seq], (1, 2, 0)),
                                    CM_HEAD_DIM, axis=2))
    bmix_s = lane_blocks(bmix_p[:dec_seq])
    small_w = [row(norm1[l]), bdr, bdi, cdr, cdi, row(ar), row(ai), row(d_skip[l]), row(b_glu[l]),
               row(cm_ln_g[l]), row(cm_ln_b[l]), row(g_s5[l]), row(g_cm[l]),
               wmix_p, bmix_p, wmix_s, bmix_s]
    h0 = (state_s5_re[l].reshape(dec_batch, N_STATES), state_s5_im[l].reshape(dec_batch, N_STATES))

    x1, pr, pi_, pv, sr, si, sv = _mixer(x_prompt, x_sample, h0, (w_in[l], w_out[l], w_glu[l]), small_w)
    y_prompt, y_sample = _ffn(x1, row(norm2[l]), row(norm_f), (w_gate[l], w_up[l], w_down[l]),
                              nb=batch, seq=seq)

    st = lambda a, n: a.reshape(1, n, S5_GROUPS, S5_STATE)
    return (y_prompt, y_sample.reshape(dec_batch, dec_seq, D_MODEL),
            st(pr, batch), st(pi_, batch), pv[None],
            st(sr, dec_batch), st(si, dec_batch), sv[None])
```

```python
import functools

import jax
import jax.numpy as jnp
from jax import lax
from jax.experimental import pallas as pl
from jax.experimental.pallas import tpu as pltpu

F32 = jnp.float32
BF16 = jnp.bfloat16

D_MODEL = 1024
S5_WIDTH = 512
S5_GROUP = 16
S5_GROUPS = 32
S5_STATE = 64
N_STATES = S5_GROUPS * S5_STATE
CM_WIDTH = 512
CM_HEADS = 8
CM_HEAD_DIM = 64
CHUNK = 128
IN_WIDTH = S5_WIDTH + 2 * CM_WIDTH
D_FF = 2816
EPS = 1e-6

S5_HALVES = 2
HALF_GROUPS = S5_GROUPS // S5_HALVES
HALF_IN = S5_WIDTH // S5_HALVES
HALF_STATES = N_STATES // S5_HALVES
SUBLANES = 8
LANES = 128
SCAN_LANES = 512
SCAN_BLOCKS = SCAN_LANES // LANES
MXU_COLS = 256
TILE_ROWS = SUBLANES * CHUNK
SEQ_STRIDE = CHUNK + SUBLANES
ROW_BLOCK = 256
FF_CHUNK = 256
FF_STAGE_SLOTS = 4
VMEM_LIMIT_BYTES = 58 * 1024 * 1024


def _rms(x, g):
    r = lax.rsqrt(jnp.mean(x * x, axis=-1, keepdims=True) + EPS)
    return x * r * g


def _dot(a, b):
    return jnp.dot(a, b, preferred_element_type=F32)


def _load_lane_blocks(ref, n_blocks):
    return jnp.concatenate([ref[k] for k in range(n_blocks)], axis=1)


def _cast_weights(weights, slots, sems):
    plan = []
    for w_hbm, w_bf_ref in weights:
        n_rows, n_cols = w_hbm.shape
        for k in range(n_cols // LANES):
            i = len(plan)
            stage = slots[i].at[pl.ds(0, n_rows), :]
            copy = pltpu.make_async_copy(w_hbm.at[:, pl.ds(k * LANES, LANES)], stage, sems.at[i])
            plan.append((copy, stage, w_bf_ref, k))
    for copy, _, _, _ in plan:
        copy.start()
    for copy, stage, w_bf_ref, k in plan:
        copy.wait()
        w_bf_ref[:, k * LANES:(k + 1) * LANES] = stage[...].astype(BF16)


def _prep_kernel(lr_ref, li_ref, ldt_ref, br_ref, bi_ref, cr_ref, ci_ref,
                 ar_ref, ai_ref, bdr_ref, bdi_ref, cdr_ref, cdi_ref):
    dt = jnp.exp(ldt_ref[...])
    lr = lr_ref[...]
    li = li_ref[...]
    mag = jnp.exp(lr * dt)
    ar = mag * jnp.cos(li * dt)
    ai = mag * jnp.sin(li * dt)
    den = lr * lr + li * li
    qr = ((ar - 1.0) * lr + ai * li) / den
    qi = (ai * lr - (ar - 1.0) * li) / den
    ar_ref[...] = ar
    ai_ref[...] = ai
    qr_t = qr.T
    qi_t = qi.T

    def iota(shape, dim):
        return lax.broadcasted_iota(jnp.int32, shape, dim)

    rep_b = (iota((S5_GROUP, HALF_IN), 1) % S5_GROUP == iota((S5_GROUP, HALF_IN), 0)).astype(BF16)
    rep_c = (iota((S5_STATE, HALF_STATES), 1) % S5_STATE == iota((S5_STATE, HALF_STATES), 0)).astype(BF16)

    def repeat_lanes(x, rep):
        hi = x.astype(BF16)
        rest = x - hi.astype(F32)
        mid = rest.astype(BF16)
        lo = (rest - mid.astype(F32)).astype(BF16)
        return _dot(hi, rep) + _dot(mid, rep) + _dot(lo, rep)

    diag_b = iota((HALF_STATES, HALF_IN), 0) // S5_STATE == iota((HALF_STATES, HALF_IN), 1) // S5_GROUP
    diag_c = iota((HALF_IN, HALF_STATES), 0) // S5_GROUP == iota((HALF_IN, HALF_STATES), 1) // S5_STATE
    for hv in range(S5_HALVES):
        groups = range(hv * HALF_GROUPS, (hv + 1) * HALF_GROUPS)
        bb_r = jnp.concatenate([qr_t[:, g:g + 1] * br_ref[g] - qi_t[:, g:g + 1] * bi_ref[g]
                                for g in groups], axis=0)
        bb_i = jnp.concatenate([qr_t[:, g:g + 1] * bi_ref[g] + qi_t[:, g:g + 1] * br_ref[g]
                                for g in groups], axis=0)
        for bb, o_ref in ((bb_r, bdr_ref), (bb_i, bdi_ref)):
            wide = repeat_lanes(bb, rep_b)
            o_ref[hv] = jnp.where(diag_b, wide, 0.0).T.astype(BF16)
        for c_ref, o_ref in ((cr_ref, cdr_ref), (ci_ref, cdi_ref)):
            c_half = c_ref[hv * HALF_IN:(hv + 1) * HALF_IN, :]
            wide = repeat_lanes(c_half, rep_c)
            o_ref[hv] = jnp.where(diag_c, wide, 0.0).T.astype(BF16)


def _cm_uv(p_cm, lng_ref, lnb_ref):
    uv = jax.nn.gelu(p_cm)
    u = uv[:, :CM_WIDTH]
    vv = uv[:, CM_WIDTH:]
    mu = jnp.mean(vv, axis=-1, keepdims=True)
    vc = vv - mu
    v = vc * lax.rsqrt(jnp.mean(vc * vc, axis=-1, keepdims=True) + EPS) * lng_ref[...] + lnb_ref[...]
    return u, v


def _scan_lanes(bur_ref, bui_ref, ar_ref, ai_ref, hf, lc, h_init, rows_at, n_steps, unrolled):
    blocks = range(lc * SCAN_BLOCKS, (lc + 1) * SCAN_BLOCKS)
    glanes = slice(hf * HALF_STATES + lc * SCAN_LANES, hf * HALF_STATES + (lc + 1) * SCAN_LANES)
    a_r = jnp.broadcast_to(ar_ref[:, glanes], (SUBLANES, SCAN_LANES))
    a_i = jnp.broadcast_to(ai_ref[:, glanes], (SUBLANES, SCAN_LANES))

    def scan_step(t, carry):
        hr, hi = carry
        trows = rows_at(t)
        b_r = jnp.concatenate([bur_ref[k, trows, :] for k in blocks], axis=1)
        b_i = jnp.concatenate([bui_ref[k, trows, :] for k in blocks], axis=1)
        nr = a_r * hr - a_i * hi + b_r
        ni = a_r * hi + a_i * hr + b_i
        for n, k in enumerate(blocks):
            bur_ref[k, trows, :] = nr[:, n * LANES:(n + 1) * LANES]
            bui_ref[k, trows, :] = ni[:, n * LANES:(n + 1) * LANES]
        return nr, ni

    if unrolled:
        carry = h_init
        for t in range(n_steps):
            carry = scan_step(t, carry)
        return carry
    return lax.fori_loop(0, n_steps, scan_step, h_init, unroll=8)


def _mixer_kernel(n_prompt, dec_seq,
                  xp_hbm, xs_hbm, h0r_hbm, h0i_hbm, win_hbm, wout_hbm, wglu_hbm,
                  g1_ref, bdr_ref, bdi_ref, cdr_ref, cdi_ref, ar_ref, ai_ref, dskip_ref, bglu_ref,
                  lng_ref, lnb_ref, gs5_ref, gcm_ref, wmixp_ref, bmixp_ref, wmixs_ref, bmixs_ref,
                  x1_ref, pr_ref, pi_ref, pv_ref, sr_ref, si_ref, sv_ref,
                  xt_ref, in_sem, bur_ref, bui_ref, hsr_ref, hsi_ref, cma_ref, cmb_ref, cmc_ref,
                  xs_ref, y0_ref, win_ref, wout_ref, wglu_ref, w_sem, h0_sem):
    nb = SUBLANES
    rows = TILE_ROWS
    step = pl.program_id(0)
    slot = step % 2

    def fetch_prompt(s, sl):
        return [pltpu.make_async_copy(xp_hbm.at[b, pl.ds(s * CHUNK, CHUNK), pl.ds(k * LANES, LANES)],
                                      xt_ref.at[sl, k, pl.ds(b * SEQ_STRIDE, CHUNK), :], in_sem.at[sl])
                for b in range(nb) for k in range(D_MODEL // LANES)]

    def fetch_sample(sl):
        return [pltpu.make_async_copy(xs_hbm.at[:, j, pl.ds(k * LANES, LANES)],
                                      xt_ref.at[sl, k, pl.ds(j * SEQ_STRIDE, CHUNK), :], in_sem.at[sl])
                for j in range(dec_seq) for k in range(D_MODEL // LANES)]

    @pl.when(step == 0)
    def _():
        for cp in fetch_prompt(0, 0):
            cp.start()
        staging = [ref.at[k] for ref in (bur_ref, bui_ref, cma_ref, cmb_ref, cmc_ref)
                   for k in range(ref.shape[0])]
        _cast_weights([(win_hbm, win_ref), (wout_hbm, wout_ref), (wglu_hbm, wglu_ref)], staging, w_sem)
        h0_copies = [pltpu.make_async_copy(h0r_hbm, sr_ref, h0_sem.at[0]),
                     pltpu.make_async_copy(h0i_hbm, si_ref, h0_sem.at[1])]
        for cp in h0_copies:
            cp.start()
        for cp in h0_copies:
            cp.wait()
        hsr_ref[...] = jnp.zeros_like(hsr_ref)
        hsi_ref[...] = jnp.zeros_like(hsi_ref)

    @pl.when(step + 1 < n_prompt)
    def _():
        for cp in fetch_prompt(step + 1, 1 - slot):
            cp.start()

    @pl.when(step + 1 == n_prompt)
    def _():
        for cp in fetch_sample(1 - slot):
            cp.start()

    @pl.when(step < n_prompt)
    def _():
        for cp in fetch_prompt(step, slot):
            cp.wait()

    @pl.when(step == n_prompt)
    def _():
        for cp in fetch_sample(slot):
            cp.wait()

    is_prompt = step < n_prompt
    is_sample = step == n_prompt
    cm_blocks = CM_WIDTH // LANES
    mxu_blocks = MXU_COLS // LANES
    us_ref, vs_ref, mx_ref = cma_ref, cmb_ref, cmc_ref
    row_blocks = [slice(r * ROW_BLOCK, (r + 1) * ROW_BLOCK) for r in range(rows // ROW_BLOCK)]

    def x_rows(rsl):
        groups = range(rsl.start // nb, rsl.stop // nb)
        return jnp.concatenate(
            [jnp.concatenate([xt_ref[slot, k, pl.ds(g, nb, stride=SEQ_STRIDE), :]
                              for k in range(D_MODEL // LANES)], axis=1) for g in groups], axis=0)

    def b_proj(hf, xs_h, rsl):
        for n in range(HALF_STATES // MXU_COLS):
            cols = slice(n * MXU_COLS, (n + 1) * MXU_COLS)
            b_r = _dot(xs_h, bdr_ref[hf, :, cols])
            b_i = _dot(xs_h, bdi_ref[hf, :, cols])
            for q in range(mxu_blocks):
                bur_ref[n * mxu_blocks + q, rsl, :] = b_r[:, q * LANES:(q + 1) * LANES]
                bui_ref[n * mxu_blocks + q, rsl, :] = b_i[:, q * LANES:(q + 1) * LANES]

    def c_proj(hf, rsl):
        y_h = None
        for n in range(HALF_STATES // MXU_COLS):
            krows = slice(n * MXU_COLS, (n + 1) * MXU_COLS)
            blocks = range(n * mxu_blocks, (n + 1) * mxu_blocks)
            h_r = jnp.concatenate([bur_ref[k, rsl, :] for k in blocks], axis=1).astype(BF16)
            h_i = jnp.concatenate([bui_ref[k, rsl, :] for k in blocks], axis=1).astype(BF16)
            part = _dot(h_r, cdr_ref[hf, krows, :]) - _dot(h_i, cdi_ref[hf, krows, :])
            y_h = part if y_h is None else y_h + part
        return y_h

    def scan_half(hf):
        @pl.when(is_prompt)
        def _():
            for lc in range(HALF_STATES // SCAN_LANES):
                glanes = slice(hf * HALF_STATES + lc * SCAN_LANES, hf * HALF_STATES + (lc + 1) * SCAN_LANES)
                hr, hi = _scan_lanes(bur_ref, bui_ref, ar_ref, ai_ref, hf, lc,
                                     (hsr_ref[:, glanes], hsi_ref[:, glanes]),
                                     lambda t: pl.ds(pl.multiple_of(t * nb, nb), nb), CHUNK, False)
                hsr_ref[:, glanes] = hr
                hsi_ref[:, glanes] = hi
                pr_ref[:, glanes] = hr
                pi_ref[:, glanes] = hi

        @pl.when(is_sample)
        def _():
            tt = dec_seq

            def scan_group(sg, carry):
                srows = pl.ds(pl.multiple_of(sg * SUBLANES, SUBLANES), SUBLANES)
                for lc in range(HALF_STATES // SCAN_LANES):
                    glanes = slice(hf * HALF_STATES + lc * SCAN_LANES,
                                   hf * HALF_STATES + (lc + 1) * SCAN_LANES)
                    hr, hi = _scan_lanes(bur_ref, bui_ref, ar_ref, ai_ref, hf, lc,
                                         (sr_ref[srows, glanes], si_ref[srows, glanes]),
                                         lambda t: pl.ds(sg * (SUBLANES * tt) + t, SUBLANES, stride=tt),
                                         tt, True)
                    sr_ref[srows, glanes] = hr
                    si_ref[srows, glanes] = hi
                return carry

            lax.fori_loop(0, rows // tt // SUBLANES, scan_group, 0)

    for rsl in row_blocks:
        h = _rms(x_rows(rsl), g1_ref[...]).astype(BF16)
        u, v = _cm_uv(_dot(h, win_ref[:, S5_WIDTH:]), lng_ref, lnb_ref)
        for k in range(cm_blocks):
            us_ref[k, rsl, :] = u[:, k * LANES:(k + 1) * LANES]
            vs_ref[k, rsl, :] = v[:, k * LANES:(k + 1) * LANES]
        xs = _dot(h, win_ref[:, :S5_WIDTH])
        xs_ref[rsl, :] = xs
        b_proj(0, xs[:, :HALF_IN].astype(BF16), rsl)

    @pl.when(is_prompt)
    def _():
        r_idx = lax.broadcasted_iota(jnp.int32, (2 * CHUNK, CHUNK), 0)
        c_idx = lax.broadcasted_iota(jnp.int32, (2 * CHUNK, CHUNK), 1)
        causal = (r_idx % CHUNK) >= c_idx
        first_head = lax.broadcasted_iota(jnp.int32, (CHUNK, LANES), 1) < CM_HEAD_DIM
        for j in range(CM_HEADS // 2):
            w_pair = jnp.where(causal, wmixp_ref[j], 0.0).astype(BF16)
            bias = bmixp_ref[:, j * LANES:(j + 1) * LANES]
            for b in range(nb):
                seq_rows = pl.ds(b, CHUNK, stride=nb)
                v_seq = vs_ref[j, seq_rows, :]
                pv_ref[b, :, j * LANES:(j + 1) * LANES] = v_seq
                r = _dot(w_pair, v_seq.astype(BF16))
                mixed = jnp.where(first_head, r[:CHUNK], r[CHUNK:]) + bias
                mx_ref[j, seq_rows, :] = us_ref[j, seq_rows, :] * mixed

    @pl.when(is_sample)
    def _():
        tt = dec_seq
        n_seq = rows // tt
        sv_ref[...] = _load_lane_blocks(vs_ref, cm_blocks).reshape(n_seq, tt, CM_WIDTH)
        def mix_block(k, carry):
            v_at = [vs_ref[k, pl.ds(s, n_seq, stride=tt), :] for s in range(tt)]
            for t in range(tt):
                acc = bmixs_ref[k, t:t + 1, :] + wmixs_ref[k, t, 0:1, :] * v_at[0]
                for s in range(1, t + 1):
                    acc = acc + wmixs_ref[k, t, s:s + 1, :] * v_at[s]
                trows = pl.ds(t, n_seq, stride=tt)
                mx_ref[k, trows, :] = us_ref[k, trows, :] * acc
            return carry

        lax.fori_loop(0, cm_blocks, mix_block, 0)

    scan_half(0)
    for rsl in row_blocks:
        y0_ref[rsl, :] = c_proj(0, rsl)
        b_proj(1, xs_ref[rsl, HALF_IN:].astype(BF16), rsl)
        o_cm = jnp.concatenate([mx_ref[k, rsl, :] for k in range(cm_blocks)], axis=1)
        n_cm = _rms(o_cm, gcm_ref[...]).astype(BF16)
        x1_ref[rsl, :] = x_rows(rsl) + _dot(n_cm, wout_ref[S5_WIDTH:, :])
    scan_half(1)
    for rsl in row_blocks:
        y = jnp.concatenate([y0_ref[rsl, :], c_proj(1, rsl)], axis=1) + dskip_ref[...] * xs_ref[rsl, :]
        g = jax.nn.gelu(y)
        o_s5 = g * jax.nn.sigmoid(_dot(g.astype(BF16), wglu_ref[...]) + bglu_ref[...])
        n_s5 = _rms(o_s5, gs5_ref[...]).astype(BF16)
        x1_ref[rsl, :] = x1_ref[rsl, :] + _dot(n_s5, wout_ref[:S5_WIDTH, :])


def _const_spec(shape):
    nd = len(shape)
    return pl.BlockSpec(shape, lambda i, nd=nd: (0,) * nd, pipeline_mode=pl.Buffered(1))


_ANY_SPEC = pl.BlockSpec(memory_space=pl.ANY)


def _mixer(x_prompt, x_sample, h0, big_w, small_w):
    nb, seq, _ = x_prompt.shape
    dec_batch, dec_seq, _ = x_sample.shape
    assert nb == SUBLANES and seq % CHUNK == 0
    assert dec_batch * dec_seq == TILE_ROWS and dec_seq <= SUBLANES and dec_batch == CHUNK
    n_prompt = seq // CHUNK
    w_in, w_out, w_glu = big_w
    in_specs = [_ANY_SPEC] * 7 + [_const_spec(a.shape) for a in small_w]
    out_shape = (
        jax.ShapeDtypeStruct(((n_prompt + 1) * TILE_ROWS, D_MODEL), F32),
        jax.ShapeDtypeStruct((nb, N_STATES), F32),
        jax.ShapeDtypeStruct((nb, N_STATES), F32),
        jax.ShapeDtypeStruct((nb, CHUNK, CM_WIDTH), F32),
        jax.ShapeDtypeStruct((dec_batch, N_STATES), F32),
        jax.ShapeDtypeStruct((dec_batch, N_STATES), F32),
        jax.ShapeDtypeStruct((dec_batch, dec_seq, CM_WIDTH), F32),
    )
    out_specs = (pl.BlockSpec((TILE_ROWS, D_MODEL), lambda i: (i, 0)),) + tuple(
        _const_spec(o.shape) for o in out_shape[1:])
    scratch = [
        pltpu.VMEM((2, D_MODEL // LANES, nb * SEQ_STRIDE, LANES), F32),
        pltpu.SemaphoreType.DMA((2,)),
        pltpu.VMEM((HALF_STATES // LANES, TILE_ROWS, LANES), F32),
        pltpu.VMEM((HALF_STATES // LANES, TILE_ROWS, LANES), F32),
        pltpu.VMEM((nb, N_STATES), F32),
        pltpu.VMEM((nb, N_STATES), F32),
        pltpu.VMEM((CM_WIDTH // LANES, TILE_ROWS, LANES), F32),
        pltpu.VMEM((CM_WIDTH // LANES, TILE_ROWS, LANES), F32),
        pltpu.VMEM((CM_WIDTH // LANES, TILE_ROWS, LANES), F32),
        pltpu.VMEM((TILE_ROWS, S5_WIDTH), F32),
        pltpu.VMEM((TILE_ROWS, HALF_IN), F32),
        pltpu.VMEM(w_in.shape, BF16),
        pltpu.VMEM(w_out.shape, BF16),
        pltpu.VMEM(w_glu.shape, BF16),
        pltpu.SemaphoreType.DMA(((w_in.shape[1] + w_out.shape[1] + w_glu.shape[1]) // LANES,)),
        pltpu.SemaphoreType.DMA((2,)),
    ]
    return pl.pallas_call(
        functools.partial(_mixer_kernel, n_prompt, dec_seq),
        grid=(n_prompt + 1,),
        in_specs=in_specs,
        out_specs=out_specs,
        out_shape=out_shape,
        scratch_shapes=scratch,
        compiler_params=pltpu.CompilerParams(
            dimension_semantics=("arbitrary",), vmem_limit_bytes=VMEM_LIMIT_BYTES),
        name="mixer",
    )(x_prompt, x_sample, *h0, w_in, w_out, w_glu, *small_w)


def _ffn_kernel(nb, n_prompt, x_ref, g2_ref, gf_ref, wg_hbm, wu_hbm, wd_hbm, yp_hbm, ys_hbm,
                yt_ref, out_sem, wg_ref, wu_ref, wd_ref, stage_gu_ref, stage_d_ref, w_sem):
    tt = TILE_ROWS // nb
    step = pl.program_id(0)
    n_steps = pl.num_programs(0)
    slot = step % 2
    n_chunks = D_FF // FF_CHUNK

    def prompt_writeback(s, sl):
        return [pltpu.make_async_copy(yt_ref.at[sl, k, pl.ds(b * SEQ_STRIDE, tt), :],
                                      yp_hbm.at[b, pl.ds(s * tt, tt), pl.ds(k * LANES, LANES)], out_sem.at[sl])
                for b in range(nb) for k in range(D_MODEL // LANES)]

    def sample_writeback(sl):
        return [pltpu.make_async_copy(yt_ref.at[sl, k, pl.ds(j * SEQ_STRIDE, tt), :],
                                      ys_hbm.at[:, j, pl.ds(k * LANES, LANES)], out_sem.at[sl])
                for j in range(nb) for k in range(D_MODEL // LANES)]

    def wait_writeback(s, sl):
        @pl.when(s < n_prompt)
        def _():
            for cp in prompt_writeback(s, sl):
                cp.wait()

        @pl.when(s >= n_prompt)
        def _():
            for cp in sample_writeback(sl):
                cp.wait()

    def weight_chunk_copies(c, sl):
        cols = pl.ds(c * FF_CHUNK, FF_CHUNK)
        return [pltpu.make_async_copy(wg_hbm.at[:, cols], stage_gu_ref.at[sl, 0], w_sem.at[sl, 0]),
                pltpu.make_async_copy(wu_hbm.at[:, cols], stage_gu_ref.at[sl, 1], w_sem.at[sl, 1]),
                pltpu.make_async_copy(wd_hbm.at[cols, :], stage_d_ref.at[sl], w_sem.at[sl, 2])]

    def stage_in_chunk(c):
        sl = c % FF_STAGE_SLOTS
        cols = slice(c * FF_CHUNK, (c + 1) * FF_CHUNK)
        for cp in weight_chunk_copies(c, sl):
            cp.wait()
        wg_ref[:, cols] = stage_gu_ref[sl, 0].astype(BF16)
        wu_ref[:, cols] = stage_gu_ref[sl, 1].astype(BF16)
        wd_ref[cols, :] = stage_d_ref[sl].astype(BF16)
        if c + FF_STAGE_SLOTS < n_chunks:
            for cp in weight_chunk_copies(c + FF_STAGE_SLOTS, sl):
                cp.start()

    @pl.when(step == 0)
    def _():
        for c in range(FF_STAGE_SLOTS):
            for cp in weight_chunk_copies(c, c):
                cp.start()
        for c in range(n_chunks):
            stage_in_chunk(c)

    @pl.when(step >= 2)
    def _():
        wait_writeback(step - 2, slot)

    halves = [slice(0, TILE_ROWS // 2), slice(TILE_ROWS // 2, TILE_ROWS)]
    hs = [_rms(x_ref[rsl, :], g2_ref[...]).astype(BF16) for rsl in halves]
    accs = [x_ref[rsl, :] for rsl in halves]
    for c in range(n_chunks):
        cols = slice(c * FF_CHUNK, (c + 1) * FF_CHUNK)
        for i in range(len(halves)):
            gate = _dot(hs[i], wg_ref[:, cols])
            up = _dot(hs[i], wu_ref[:, cols])
            accs[i] = accs[i] + _dot((jax.nn.silu(gate) * up).astype(BF16), wd_ref[cols, :])
    for i, rsl in enumerate(halves):
        y = _rms(accs[i], gf_ref[...])
        for g in range(rsl.start // nb, rsl.stop // nb):
            r0 = g * nb - rsl.start
            for k in range(D_MODEL // LANES):
                yt_ref[slot, k, pl.ds(g, nb, stride=SEQ_STRIDE), :] = y[r0:r0 + nb, k * LANES:(k + 1) * LANES]

    @pl.when(step < n_prompt)
    def _():
        for cp in prompt_writeback(step, slot):
            cp.start()

    @pl.when(step >= n_prompt)
    def _():
        for cp in sample_writeback(slot):
            cp.start()

    @pl.when(step == n_steps - 1)
    def _():
        wait_writeback(step - 1, 1 - slot)
        wait_writeback(step, slot)


def _ffn(x1, g2, gf, ffn_w, *, nb, seq):
    n = x1.shape[0]
    n_prompt = nb * seq // TILE_ROWS
    assert n == (n_prompt + 1) * TILE_ROWS and TILE_ROWS % nb == 0 and n_prompt >= 1
    tt = TILE_ROWS // nb
    w_gate, w_up, w_down = ffn_w
    return pl.pallas_call(
        functools.partial(_ffn_kernel, nb, n_prompt),
        grid=(n_prompt + 1,),
        in_specs=[pl.BlockSpec((TILE_ROWS, D_MODEL), lambda i: (i, 0)),
                  _const_spec(g2.shape), _const_spec(gf.shape)] + [_ANY_SPEC] * 3,
        out_specs=(_ANY_SPEC, _ANY_SPEC),
        out_shape=(jax.ShapeDtypeStruct((nb, seq, D_MODEL), F32),
                   jax.ShapeDtypeStruct((tt, nb, D_MODEL), F32)),
        scratch_shapes=[pltpu.VMEM((2, D_MODEL // LANES, nb * SEQ_STRIDE, LANES), F32),
                        pltpu.SemaphoreType.DMA((2,)),
                        pltpu.VMEM(w_gate.shape, BF16),
                        pltpu.VMEM(w_up.shape, BF16),
                        pltpu.VMEM(w_down.shape, BF16),
                        pltpu.VMEM((FF_STAGE_SLOTS, 2, D_MODEL, FF_CHUNK), F32),
                        pltpu.VMEM((FF_STAGE_SLOTS, FF_CHUNK, D_MODEL), F32),
                        pltpu.SemaphoreType.DMA((FF_STAGE_SLOTS, 3))],
        compiler_params=pltpu.CompilerParams(
            dimension_semantics=("arbitrary",), vmem_limit_bytes=VMEM_LIMIT_BYTES),
        name="ffn",
    )(x1, g2, gf, w_gate, w_up, w_down)


def kernel(x_prompt, x_sample, state_s5_re, state_s5_im, norm1, w_in, lam_re, lam_im, log_dt, b_re, b_im, c_re, c_im, d_skip, w_glu, b_glu, cm_ln_g, cm_ln_b, w_s, b_s, g_s5, g_cm, w_out, norm2, w_gate, w_up, w_down, norm_f):
    depth = norm1.shape[0]
    assert depth == 1
    l = 0
    batch, seq, _ = x_prompt.shape
    dec_batch, dec_seq, _ = x_sample.shape

    ar, ai, bdr, bdi, cdr, cdi = pl.pallas_call(
        _prep_kernel,
        out_shape=(
            jax.ShapeDtypeStruct((S5_GROUPS, S5_STATE), F32),
            jax.ShapeDtypeStruct((S5_GROUPS, S5_STATE), F32),
            jax.ShapeDtypeStruct((S5_HALVES, HALF_IN, HALF_STATES), BF16),
            jax.ShapeDtypeStruct((S5_HALVES, HALF_IN, HALF_STATES), BF16),
            jax.ShapeDtypeStruct((S5_HALVES, HALF_STATES, HALF_IN), BF16),
            jax.ShapeDtypeStruct((S5_HALVES, HALF_STATES, HALF_IN), BF16),
        ),
        name="s5_prepare",
    )(lam_re[l], lam_im[l], log_dt[l][:, None], b_re[l], b_im[l],
      c_re[l].reshape(S5_WIDTH, S5_STATE), c_im[l].reshape(S5_WIDTH, S5_STATE))

    row = lambda a: a.reshape(1, -1)
    wmix_p = w_s[l].reshape(CM_HEADS // 2, 2 * CHUNK, CHUNK)
    bmix_p = jnp.repeat(b_s[l].T, CM_HEAD_DIM, axis=1)
    lane_blocks = lambda a: jnp.moveaxis(a.reshape(a.shape[:-1] + (CM_WIDTH // LANES, LANES)), -2, 0)
    wmix_s = lane_blocks(jnp.repeat(jnp.transpose(w_s[l][:, :dec_seq, :dec_seq], (1, 2, 0)),
                                    CM_HEAD_DIM, axis=2))
    bmix_s = lane_blocks(bmix_p[:dec_seq])
    small_w = [row(norm1[l]), bdr, bdi, cdr, cdi, row(ar), row(ai), row(d_skip[l]), row(b_glu[l]),
               row(cm_ln_g[l]), row(cm_ln_b[l]), row(g_s5[l]), row(g_cm[l]),
               wmix_p, bmix_p, wmix_s, bmix_s]
    h0 = (state_s5_re[l].reshape(dec_batch, N_STATES), state_s5_im[l].reshape(dec_batch, N_STATES))

    x1, pr, pi_, pv, sr, si, sv = _mixer(x_prompt, x_sample, h0, (w_in[l], w_out[l], w_glu[l]), small_w)
    y_prompt, y_sample = _ffn(x1, row(norm2[l]), row(norm_f), (w_gate[l], w_up[l], w_down[l]),
                              nb=batch, seq=seq)

    st = lambda a, n: a.reshape(1, n, S5_GROUPS, S5_STATE)
    return (y_prompt, y_sample.reshape(dec_batch, dec_seq, D_MODEL),
            st(pr, batch), st(pi_, batch), pv[None],
            st(sr, dec_batch), st(si, dec_batch), sv[None])
```

```python
import functools

import jax
import jax.numpy as jnp
from jax import lax
from jax.experimental import pallas as pl
from jax.experimental.pallas import tpu as pltpu

F32 = jnp.float32
BF16 = jnp.bfloat16

D_MODEL = 1024
S5_WIDTH = 512
S5_GROUP = 16
S5_GROUPS = 32
S5_STATE = 64
N_STATES = S5_GROUPS * S5_STATE
CM_WIDTH = 512
CM_HEADS = 8
CM_HEAD_DIM = 64
CHUNK = 128
IN_WIDTH = S5_WIDTH + 2 * CM_WIDTH
D_FF = 2816
EPS = 1e-6

S5_HALVES = 2
HALF_GROUPS = S5_GROUPS // S5_HALVES
HALF_IN = S5_WIDTH // S5_HALVES
HALF_STATES = N_STATES // S5_HALVES
SUBLANES = 8
LANES = 128
SCAN_LANES = 512
SCAN_BLOCKS = SCAN_LANES // LANES
MXU_COLS = 256
TILE_ROWS = SUBLANES * CHUNK
ROW_BLOCK = 256
FF_CHUNK = 256
FF_STAGE_SLOTS = 4
VMEM_LIMIT_BYTES = 58 * 1024 * 1024


def _rms(x, g):
    r = lax.rsqrt(jnp.mean(x * x, axis=-1, keepdims=True) + EPS)
    return x * r * g


def _dot(a, b):
    return jnp.dot(a, b, preferred_element_type=F32)


def _load_lane_blocks(ref, n_blocks):
    return jnp.concatenate([ref[k] for k in range(n_blocks)], axis=1)


def _cast_weights(weights, slots, sems):
    plan = []
    for w_hbm, w_bf_ref in weights:
        n_rows, n_cols = w_hbm.shape
        for k in range(n_cols // LANES):
            i = len(plan)
            stage = slots[i].at[pl.ds(0, n_rows), :]
            copy = pltpu.make_async_copy(w_hbm.at[:, pl.ds(k * LANES, LANES)], stage, sems.at[i])
            plan.append((copy, stage, w_bf_ref, k))
    for copy, _, _, _ in plan:
        copy.start()
    for copy, stage, w_bf_ref, k in plan:
        copy.wait()
        w_bf_ref[:, k * LANES:(k + 1) * LANES] = stage[...].astype(BF16)


def _prep_kernel(lr_ref, li_ref, ldt_ref, br_ref, bi_ref, cr_ref, ci_ref,
                 ar_ref, ai_ref, bdr_ref, bdi_ref, cdr_ref, cdi_ref):
    dt = jnp.exp(ldt_ref[...])
    lr = lr_ref[...]
    li = li_ref[...]
    mag = jnp.exp(lr * dt)
    ar = mag * jnp.cos(li * dt)
    ai = mag * jnp.sin(li * dt)
    den = lr * lr + li * li
    qr = ((ar - 1.0) * lr + ai * li) / den
    qi = (ai * lr - (ar - 1.0) * li) / den
    for g in range(S5_GROUPS):
        ar_ref[:, g * S5_STATE:(g + 1) * S5_STATE] = ar[g:g + 1, :]
        ai_ref[:, g * S5_STATE:(g + 1) * S5_STATE] = ai[g:g + 1, :]
    qr_t = qr.T
    qi_t = qi.T

    def iota(shape, dim):
        return lax.broadcasted_iota(jnp.int32, shape, dim)

    rep_b = (iota((S5_GROUP, HALF_IN), 1) % S5_GROUP == iota((S5_GROUP, HALF_IN), 0)).astype(BF16)
    rep_c = (iota((S5_STATE, HALF_STATES), 1) % S5_STATE == iota((S5_STATE, HALF_STATES), 0)).astype(BF16)

    def repeat_lanes(x, rep):
        hi = x.astype(BF16)
        rest = x - hi.astype(F32)
        mid = rest.astype(BF16)
        lo = (rest - mid.astype(F32)).astype(BF16)
        return _dot(hi, rep) + _dot(mid, rep) + _dot(lo, rep)

    diag_b = iota((HALF_STATES, HALF_IN), 0) // S5_STATE == iota((HALF_STATES, HALF_IN), 1) // S5_GROUP
    diag_c = iota((HALF_IN, HALF_STATES), 0) // S5_GROUP == iota((HALF_IN, HALF_STATES), 1) // S5_STATE
    for hv in range(S5_HALVES):
        groups = range(hv * HALF_GROUPS, (hv + 1) * HALF_GROUPS)
        bb_r = jnp.concatenate([qr_t[:, g:g + 1] * br_ref[0, g] - qi_t[:, g:g + 1] * bi_ref[0, g]
                                for g in groups], axis=0)
        bb_i = jnp.concatenate([qr_t[:, g:g + 1] * bi_ref[0, g] + qi_t[:, g:g + 1] * br_ref[0, g]
                                for g in groups], axis=0)
        for bb, o_ref in ((bb_r, bdr_ref), (bb_i, bdi_ref)):
            wide = repeat_lanes(bb, rep_b)
            o_ref[hv] = jnp.where(diag_b, wide, 0.0).T.astype(BF16)
        for c_ref, o_ref in ((cr_ref, cdr_ref), (ci_ref, cdi_ref)):
            c_half = c_ref[hv * HALF_IN:(hv + 1) * HALF_IN, :]
            wide = repeat_lanes(c_half, rep_c)
            o_ref[hv] = jnp.where(diag_c, wide, 0.0).T.astype(BF16)


def _cm_uv(p_cm, lng_ref, lnb_ref):
    uv = jax.nn.gelu(p_cm)
    u = uv[:, :CM_WIDTH]
    vv = uv[:, CM_WIDTH:]
    mu = jnp.mean(vv, axis=-1, keepdims=True)
    vc = vv - mu
    v = vc * lax.rsqrt(jnp.mean(vc * vc, axis=-1, keepdims=True) + EPS) * lng_ref[...] + lnb_ref[...]
    return u, v


def _scan_lanes(bur_ref, bui_ref, ar_ref, ai_ref, hf, lc, h_init, rows_at, n_steps, unrolled):
    blocks = range(lc * SCAN_BLOCKS, (lc + 1) * SCAN_BLOCKS)
    glanes = slice(hf * HALF_STATES + lc * SCAN_LANES, hf * HALF_STATES + (lc + 1) * SCAN_LANES)
    a_r = jnp.broadcast_to(ar_ref[:, glanes], (SUBLANES, SCAN_LANES))
    a_i = jnp.broadcast_to(ai_ref[:, glanes], (SUBLANES, SCAN_LANES))

    def scan_step(t, carry):
        hr, hi = carry
        trows = rows_at(t)
        b_r = jnp.concatenate([bur_ref[k, trows, :] for k in blocks], axis=1)
        b_i = jnp.concatenate([bui_ref[k, trows, :] for k in blocks], axis=1)
        nr = a_r * hr - a_i * hi + b_r
        ni = a_r * hi + a_i * hr + b_i
        for n, k in enumerate(blocks):
            bur_ref[k, trows, :] = nr[:, n * LANES:(n + 1) * LANES]
            bui_ref[k, trows, :] = ni[:, n * LANES:(n + 1) * LANES]
        return nr, ni

    if unrolled:
        carry = h_init
        for t in range(n_steps):
            carry = scan_step(t, carry)
        return carry
    return lax.fori_loop(0, n_steps, scan_step, h_init, unroll=8)


def _mixer_kernel(n_prompt, dec_seq,
                  xp_hbm, xs_hbm, h0r_hbm, h0i_hbm, win_hbm, wout_hbm, wglu_hbm,
                  g1_ref, bdr_ref, bdi_ref, cdr_ref, cdi_ref, ar_ref, ai_ref, dskip_ref, bglu_ref,
                  lng_ref, lnb_ref, gs5_ref, gcm_ref, wmixp_ref, bmixp_ref, wmixs_ref, bmixs_ref,
                  x1_ref, pr_ref, pi_ref, pv_ref, sr_ref, si_ref, sv_ref,
                  xt_ref, in_sem, bur_ref, bui_ref, hsr_ref, hsi_ref, cma_ref, cmb_ref, cmc_ref,
                  xs_ref, y0_ref, win_ref, wout_ref, wglu_ref, w_sem, h0_sem):
    nb = SUBLANES
    rows = TILE_ROWS
    step = pl.program_id(0)
    slot = step % 2

    def fetch_prompt(s, sl):
        return [pltpu.make_async_copy(xp_hbm.at[b, pl.ds(s * CHUNK, CHUNK), :], xt_ref.at[sl, :, b, :],
                                      in_sem.at[sl, b]) for b in range(nb)]

    def fetch_sample(sl):
        return pltpu.make_async_copy(xs_hbm, xt_ref.at[sl], in_sem.at[sl, 0])

    @pl.when(step == 0)
    def _():
        for cp in fetch_prompt(0, 0):
            cp.start()
        staging = [ref.at[k] for ref in (bur_ref, bui_ref, cma_ref, cmb_ref, cmc_ref)
                   for k in range(ref.shape[0])]
        _cast_weights([(win_hbm, win_ref), (wout_hbm, wout_ref), (wglu_hbm, wglu_ref)], staging, w_sem)
        h0_copies = [pltpu.make_async_copy(h0r_hbm, sr_ref, h0_sem.at[0]),
                     pltpu.make_async_copy(h0i_hbm, si_ref, h0_sem.at[1])]
        for cp in h0_copies:
            cp.start()
        for cp in h0_copies:
            cp.wait()
        hsr_ref[...] = jnp.zeros_like(hsr_ref)
        hsi_ref[...] = jnp.zeros_like(hsi_ref)

    @pl.when(step + 1 < n_prompt)
    def _():
        for cp in fetch_prompt(step + 1, 1 - slot):
            cp.start()

    @pl.when(step + 1 == n_prompt)
    def _():
        fetch_sample(1 - slot).start()

    @pl.when(step < n_prompt)
    def _():
        for cp in fetch_prompt(step, slot):
            cp.wait()

    @pl.when(step == n_prompt)
    def _():
        fetch_sample(slot).wait()

    is_prompt = step < n_prompt
    is_sample = step == n_prompt
    cm_blocks = CM_WIDTH // LANES
    mxu_blocks = MXU_COLS // LANES
    us_ref, vs_ref, mx_ref = cma_ref, cmb_ref, cmc_ref
    row_blocks = [slice(r * ROW_BLOCK, (r + 1) * ROW_BLOCK) for r in range(rows // ROW_BLOCK)]

    def x_rows(rsl):
        groups = slice(rsl.start // nb, rsl.stop // nb)
        return xt_ref[slot, groups].reshape(ROW_BLOCK, D_MODEL)

    def b_proj(hf, xs_h, rsl):
        for n in range(HALF_STATES // MXU_COLS):
            cols = slice(n * MXU_COLS, (n + 1) * MXU_COLS)
            b_r = _dot(xs_h, bdr_ref[hf, :, cols])
            b_i = _dot(xs_h, bdi_ref[hf, :, cols])
            for q in range(mxu_blocks):
                bur_ref[n * mxu_blocks + q, rsl, :] = b_r[:, q * LANES:(q + 1) * LANES]
                bui_ref[n * mxu_blocks + q, rsl, :] = b_i[:, q * LANES:(q + 1) * LANES]

    def c_proj(hf, rsl):
        y_h = None
        for n in range(HALF_STATES // MXU_COLS):
            krows = slice(n * MXU_COLS, (n + 1) * MXU_COLS)
            blocks = range(n * mxu_blocks, (n + 1) * mxu_blocks)
            h_r = jnp.concatenate([bur_ref[k, rsl, :] for k in blocks], axis=1).astype(BF16)
            h_i = jnp.concatenate([bui_ref[k, rsl, :] for k in blocks], axis=1).astype(BF16)
            part = _dot(h_r, cdr_ref[hf, krows, :]) - _dot(h_i, cdi_ref[hf, krows, :])
            y_h = part if y_h is None else y_h + part
        return y_h

    def scan_half(hf):
        @pl.when(is_prompt)
        def _():
            for lc in range(HALF_STATES // SCAN_LANES):
                glanes = slice(hf * HALF_STATES + lc * SCAN_LANES, hf * HALF_STATES + (lc + 1) * SCAN_LANES)
                hr, hi = _scan_lanes(bur_ref, bui_ref, ar_ref, ai_ref, hf, lc,
                                     (hsr_ref[:, glanes], hsi_ref[:, glanes]),
                                     lambda t: pl.ds(pl.multiple_of(t * nb, nb), nb), CHUNK, False)
                hsr_ref[:, glanes] = hr
                hsi_ref[:, glanes] = hi
                pr_ref[:, glanes] = hr
                pi_ref[:, glanes] = hi

        @pl.when(is_sample)
        def _():
            tt = dec_seq

            def scan_group(sg, carry):
                srows = pl.ds(pl.multiple_of(sg * SUBLANES, SUBLANES), SUBLANES)
                for lc in range(HALF_STATES // SCAN_LANES):
                    glanes = slice(hf * HALF_STATES + lc * SCAN_LANES,
                                   hf * HALF_STATES + (lc + 1) * SCAN_LANES)
                    hr, hi = _scan_lanes(bur_ref, bui_ref, ar_ref, ai_ref, hf, lc,
                                         (sr_ref[srows, glanes], si_ref[srows, glanes]),
                                         lambda t: pl.ds(sg * (SUBLANES * tt) + t, SUBLANES, stride=tt),
                                         tt, True)
                    sr_ref[srows, glanes] = hr
                    si_ref[srows, glanes] = hi
                return carry

            lax.fori_loop(0, rows // tt // SUBLANES, scan_group, 0)

    for rsl in row_blocks:
        h = _rms(x_rows(rsl), g1_ref[...]).astype(BF16)
        u, v = _cm_uv(_dot(h, win_ref[:, S5_WIDTH:]), lng_ref, lnb_ref)
        for k in range(cm_blocks):
            us_ref[k, rsl, :] = u[:, k * LANES:(k + 1) * LANES]
            vs_ref[k, rsl, :] = v[:, k * LANES:(k + 1) * LANES]
        xs = _dot(h, win_ref[:, :S5_WIDTH])
        xs_ref[rsl, :] = xs
        b_proj(0, xs[:, :HALF_IN].astype(BF16), rsl)

    r_idx = lax.broadcasted_iota(jnp.int32, (2 * CHUNK, CHUNK), 0)
    c_idx = lax.broadcasted_iota(jnp.int32, (2 * CHUNK, CHUNK), 1)
    causal = (r_idx % CHUNK) >= c_idx
    first_head = lax.broadcasted_iota(jnp.int32, (CHUNK, LANES), 1) < CM_HEAD_DIM
    for j in range(CM_HEADS // 2):
        w_pair = jnp.where(causal, wmixp_ref[j], 0.0).astype(BF16)
        bias = bmixp_ref[:, j * LANES:(j + 1) * LANES]
        for b in range(nb):
            seq_rows = pl.ds(b, CHUNK, stride=nb)
            r = _dot(w_pair, vs_ref[j, seq_rows, :].astype(BF16))
            mixed = jnp.where(first_head, r[:CHUNK], r[CHUNK:]) + bias
            mx_ref[j, seq_rows, :] = us_ref[j, seq_rows, :] * mixed

    @pl.when(step == n_prompt - 1)
    def _():
        for j in range(cm_blocks):
            for b in range(nb):
                pv_ref[b, :, j * LANES:(j + 1) * LANES] = vs_ref[j, pl.ds(b, CHUNK, stride=nb), :]

    @pl.when(is_sample)
    def _():
        tt = dec_seq
        n_seq = rows // tt
        sv_ref[...] = _load_lane_blocks(vs_ref, cm_blocks).reshape(n_seq, tt, CM_WIDTH)
        def mix_block(k, carry):
            v_at = [vs_ref[k, pl.ds(s, n_seq, stride=tt), :] for s in range(tt)]
            for t in range(tt):
                acc = bmixs_ref[k, t:t + 1, :] + wmixs_ref[k, t, 0:1, :] * v_at[0]
                for s in range(1, t + 1):
                    acc = acc + wmixs_ref[k, t, s:s + 1, :] * v_at[s]
                trows = pl.ds(t, n_seq, stride=tt)
                mx_ref[k, trows, :] = us_ref[k, trows, :] * acc
            return carry

        lax.fori_loop(0, cm_blocks, mix_block, 0)

    scan_half(0)
    for rsl in row_blocks:
        y0_ref[rsl, :] = c_proj(0, rsl)
        b_proj(1, xs_ref[rsl, HALF_IN:].astype(BF16), rsl)
        o_cm = jnp.concatenate([mx_ref[k, rsl, :] for k in range(cm_blocks)], axis=1)
        n_cm = _rms(o_cm, gcm_ref[...]).astype(BF16)
        x1_ref[rsl, :] = x_rows(rsl) + _dot(n_cm, wout_ref[S5_WIDTH:, :])
    scan_half(1)
    for rsl in row_blocks:
        y = jnp.concatenate([y0_ref[rsl, :], c_proj(1, rsl)], axis=1) + dskip_ref[...] * xs_ref[rsl, :]
        g = jax.nn.gelu(y)
        o_s5 = g * jax.nn.sigmoid(_dot(g.astype(BF16), wglu_ref[...]) + bglu_ref[...])
        n_s5 = _rms(o_s5, gs5_ref[...]).astype(BF16)
        x1_ref[rsl, :] = x1_ref[rsl, :] + _dot(n_s5, wout_ref[:S5_WIDTH, :])


def _const_spec(shape):
    nd = len(shape)
    return pl.BlockSpec(shape, lambda i, nd=nd: (0,) * nd, pipeline_mode=pl.Buffered(1))


_ANY_SPEC = pl.BlockSpec(memory_space=pl.ANY)


def _mixer(x_prompt, x_sample, h0, big_w, small_w):
    nb, seq, _ = x_prompt.shape
    dec_batch, dec_seq, _ = x_sample.shape
    assert nb == SUBLANES and seq % CHUNK == 0
    assert dec_batch * dec_seq == TILE_ROWS and dec_seq <= SUBLANES and dec_batch == CHUNK
    n_prompt = seq // CHUNK
    w_in, w_out, w_glu = big_w
    in_specs = [_ANY_SPEC] * 7 + [_const_spec(a.shape) for a in small_w]
    out_shape = (
        jax.ShapeDtypeStruct(((n_prompt + 1) * TILE_ROWS, D_MODEL), F32),
        jax.ShapeDtypeStruct((nb, N_STATES), F32),
        jax.ShapeDtypeStruct((nb, N_STATES), F32),
        jax.ShapeDtypeStruct((nb, CHUNK, CM_WIDTH), F32),
        jax.ShapeDtypeStruct((dec_batch, N_STATES), F32),
        jax.ShapeDtypeStruct((dec_batch, N_STATES), F32),
        jax.ShapeDtypeStruct((dec_batch, dec_seq, CM_WIDTH), F32),
    )
    out_specs = (pl.BlockSpec((TILE_ROWS, D_MODEL), lambda i: (i, 0)),) + tuple(
        _const_spec(o.shape) for o in out_shape[1:])
    scratch = [
        pltpu.VMEM((2, CHUNK, nb, D_MODEL), F32),
        pltpu.SemaphoreType.DMA((2, nb)),
        pltpu.VMEM((HALF_STATES // LANES, TILE_ROWS, LANES), F32),
        pltpu.VMEM((HALF_STATES // LANES, TILE_ROWS, LANES), F32),
        pltpu.VMEM((nb, N_STATES), F32),
        pltpu.VMEM((nb, N_STATES), F32),
        pltpu.VMEM((CM_WIDTH // LANES, TILE_ROWS, LANES), F32),
        pltpu.VMEM((CM_WIDTH // LANES, TILE_ROWS, LANES), F32),
        pltpu.VMEM((CM_WIDTH // LANES, TILE_ROWS, LANES), F32),
        pltpu.VMEM((TILE_ROWS, S5_WIDTH), F32),
        pltpu.VMEM((TILE_ROWS, HALF_IN), F32),
        pltpu.VMEM(w_in.shape, BF16),
        pltpu.VMEM(w_out.shape, BF16),
        pltpu.VMEM(w_glu.shape, BF16),
        pltpu.SemaphoreType.DMA(((w_in.shape[1] + w_out.shape[1] + w_glu.shape[1]) // LANES,)),
        pltpu.SemaphoreType.DMA((2,)),
    ]
    return pl.pallas_call(
        functools.partial(_mixer_kernel, n_prompt, dec_seq),
        grid=(n_prompt + 1,),
        in_specs=in_specs,
        out_specs=out_specs,
        out_shape=out_shape,
        scratch_shapes=scratch,
        compiler_params=pltpu.CompilerParams(
            dimension_semantics=("arbitrary",), vmem_limit_bytes=VMEM_LIMIT_BYTES),
        name="mixer",
    )(x_prompt, x_sample, *h0, w_in, w_out, w_glu, *small_w)


def _ffn_kernel(nb, n_prompt, x_ref, g2_ref, gf_ref, wg_hbm, wu_hbm, wd_hbm, yp_hbm, ys_hbm,
                yt_ref, out_sem, wg_ref, wu_ref, wd_ref, stage_gu_ref, stage_d_ref, w_sem):
    tt = TILE_ROWS // nb
    step = pl.program_id(0)
    n_steps = pl.num_programs(0)
    slot = step % 2
    n_chunks = D_FF // FF_CHUNK

    def prompt_writeback(s, sl):
        return [pltpu.make_async_copy(yt_ref.at[sl, :, b, :], yp_hbm.at[b, pl.ds(s * tt, tt), :],
                                      out_sem.at[sl, b]) for b in range(nb)]

    def sample_writeback(sl):
        return [pltpu.make_async_copy(yt_ref.at[sl], ys_hbm, out_sem.at[sl, 0])]

    def wait_writeback(s, sl):
        @pl.when(s < n_prompt)
        def _():
            for cp in prompt_writeback(s, sl):
                cp.wait()

        @pl.when(s >= n_prompt)
        def _():
            for cp in sample_writeback(sl):
                cp.wait()

    def weight_chunk_copies(c, sl):
        cols = pl.ds(c * FF_CHUNK, FF_CHUNK)
        return [pltpu.make_async_copy(wg_hbm.at[:, cols], stage_gu_ref.at[sl, 0], w_sem.at[sl, 0]),
                pltpu.make_async_copy(wu_hbm.at[:, cols], stage_gu_ref.at[sl, 1], w_sem.at[sl, 1]),
                pltpu.make_async_copy(wd_hbm.at[cols, :], stage_d_ref.at[sl], w_sem.at[sl, 2])]

    def stage_in_chunk(c):
        sl = c % FF_STAGE_SLOTS
        cols = slice(c * FF_CHUNK, (c + 1) * FF_CHUNK)
        for cp in weight_chunk_copies(c, sl):
            cp.wait()
        wg_ref[:, cols] = stage_gu_ref[sl, 0].astype(BF16)
        wu_ref[:, cols] = stage_gu_ref[sl, 1].astype(BF16)
        wd_ref[cols, :] = stage_d_ref[sl].astype(BF16)
        if c + FF_STAGE_SLOTS < n_chunks:
            for cp in weight_chunk_copies(c + FF_STAGE_SLOTS, sl):
                cp.start()

    @pl.when(step == 0)
    def _():
        for c in range(FF_STAGE_SLOTS):
            for cp in weight_chunk_copies(c, c):
                cp.start()
        for c in range(n_chunks):
            stage_in_chunk(c)

    @pl.when(step >= 2)
    def _():
        wait_writeback(step - 2, slot)

    halves = [slice(0, TILE_ROWS // 2), slice(TILE_ROWS // 2, TILE_ROWS)]
    hs = [_rms(x_ref[rsl, :], g2_ref[...]).astype(BF16) for rsl in halves]
    accs = [x_ref[rsl, :] for rsl in halves]
    for c in range(n_chunks):
        cols = slice(c * FF_CHUNK, (c + 1) * FF_CHUNK)
        for i in range(len(halves)):
            gate = _dot(hs[i], wg_ref[:, cols])
            up = _dot(hs[i], wu_ref[:, cols])
            accs[i] = accs[i] + _dot((jax.nn.silu(gate) * up).astype(BF16), wd_ref[cols, :])
    for i, rsl in enumerate(halves):
        groups = slice(rsl.start // nb, rsl.stop // nb)
        yt_ref[slot, groups] = _rms(accs[i], gf_ref[...]).reshape(groups.stop - groups.start, nb, D_MODEL)

    @pl.when(step < n_prompt)
    def _():
        for cp in prompt_writeback(step, slot):
            cp.start()

    @pl.when(step >= n_prompt)
    def _():
        for cp in sample_writeback(slot):
            cp.start()

    @pl.when(step == n_steps - 1)
    def _():
        wait_writeback(step - 1, 1 - slot)
        wait_writeback(step, slot)


def _ffn(x1, g2, gf, ffn_w, *, nb, seq):
    n = x1.shape[0]
    n_prompt = nb * seq // TILE_ROWS
    assert n == (n_prompt + 1) * TILE_ROWS and TILE_ROWS % nb == 0 and n_prompt >= 1
    tt = TILE_ROWS // nb
    w_gate, w_up, w_down = ffn_w
    return pl.pallas_call(
        functools.partial(_ffn_kernel, nb, n_prompt),
        grid=(n_prompt + 1,),
        in_specs=[pl.BlockSpec((TILE_ROWS, D_MODEL), lambda i: (i, 0)),
                  _const_spec(g2.shape), _const_spec(gf.shape)] + [_ANY_SPEC] * 3,
        out_specs=(_ANY_SPEC, _ANY_SPEC),
        out_shape=(jax.ShapeDtypeStruct((nb, seq, D_MODEL), F32),
                   jax.ShapeDtypeStruct((tt, nb, D_MODEL), F32)),
        scratch_shapes=[pltpu.VMEM((2, tt, nb, D_MODEL), F32),
                        pltpu.SemaphoreType.DMA((2, nb)),
                        pltpu.VMEM(w_gate.shape, BF16),
                        pltpu.VMEM(w_up.shape, BF16),
                        pltpu.VMEM(w_down.shape, BF16),
                        pltpu.VMEM((FF_STAGE_SLOTS, 2, D_MODEL, FF_CHUNK), F32),
                        pltpu.VMEM((FF_STAGE_SLOTS, FF_CHUNK, D_MODEL), F32),
                        pltpu.SemaphoreType.DMA((FF_STAGE_SLOTS, 3))],
        compiler_params=pltpu.CompilerParams(
            dimension_semantics=("arbitrary",), vmem_limit_bytes=VMEM_LIMIT_BYTES),
        name="ffn",
    )(x1, g2, gf, w_gate, w_up, w_down)


def kernel(x_prompt, x_sample, state_s5_re, state_s5_im, norm1, w_in, lam_re, lam_im, log_dt, b_re, b_im, c_re, c_im, d_skip, w_glu, b_glu, cm_ln_g, cm_ln_b, w_s, b_s, g_s5, g_cm, w_out, norm2, w_gate, w_up, w_down, norm_f):
    depth = norm1.shape[0]
    assert depth == 1
    l = 0
    batch, seq, _ = x_prompt.shape
    dec_batch, dec_seq, _ = x_sample.shape

    ar, ai, bdr, bdi, cdr, cdi = pl.pallas_call(
        _prep_kernel,
        out_shape=(
            jax.ShapeDtypeStruct((1, N_STATES), F32),
            jax.ShapeDtypeStruct((1, N_STATES), F32),
            jax.ShapeDtypeStruct((S5_HALVES, HALF_IN, HALF_STATES), BF16),
            jax.ShapeDtypeStruct((S5_HALVES, HALF_IN, HALF_STATES), BF16),
            jax.ShapeDtypeStruct((S5_HALVES, HALF_STATES, HALF_IN), BF16),
            jax.ShapeDtypeStruct((S5_HALVES, HALF_STATES, HALF_IN), BF16),
        ),
        name="s5_prepare",
    )(lam_re[l], lam_im[l], log_dt[l][:, None], b_re, b_im,
      c_re[l].reshape(S5_WIDTH, S5_STATE), c_im[l].reshape(S5_WIDTH, S5_STATE))

    row = lambda a: a.reshape(1, -1)
    wmix_p = w_s[l].reshape(CM_HEADS // 2, 2 * CHUNK, CHUNK)
    bmix_p = jnp.repeat(b_s[l].T, CM_HEAD_DIM, axis=1)
    lane_blocks = lambda a: jnp.moveaxis(a.reshape(a.shape[:-1] + (CM_WIDTH // LANES, LANES)), -2, 0)
    wmix_s = lane_blocks(jnp.repeat(jnp.transpose(w_s[l][:, :dec_seq, :dec_seq], (1, 2, 0)),
                                    CM_HEAD_DIM, axis=2))
    bmix_s = lane_blocks(bmix_p[:dec_seq])
    small_w = [row(norm1[l]), bdr, bdi, cdr, cdi, ar, ai, row(d_skip[l]), row(b_glu[l]),
               row(cm_ln_g[l]), row(cm_ln_b[l]), row(g_s5[l]), row(g_cm[l]),
               wmix_p, bmix_p, wmix_s, bmix_s]
    h0 = (state_s5_re[l].reshape(dec_batch, N_STATES), state_s5_im[l].reshape(dec_batch, N_STATES))

    x1, pr, pi_, pv, sr, si, sv = _mixer(x_prompt, x_sample, h0, (w_in[l], w_out[l], w_glu[l]), small_w)
    y_prompt, y_sample = _ffn(x1, row(norm2[l]), row(norm_f), (w_gate[l], w_up[l], w_down[l]),
                              nb=batch, seq=seq)

    st = lambda a, n: a.reshape(1, n, S5_GROUPS, S5_STATE)
    return (y_prompt, y_sample.reshape(dec_batch, dec_seq, D_MODEL),
            st(pr, batch), st(pi_, batch), pv[None],
            st(sr, dec_batch), st(si, dec_batch), sv[None])
```

```python
import functools

import jax
import jax.numpy as jnp
from jax import lax
from jax.experimental import pallas as pl
from jax.experimental.pallas import tpu as pltpu

F32 = jnp.float32
BF16 = jnp.bfloat16

D_MODEL = 1024
S5_WIDTH = 512
S5_GROUP = 16
S5_GROUPS = 32
S5_STATE = 64
N_STATES = S5_GROUPS * S5_STATE
CM_WIDTH = 512
CM_HEADS = 8
CM_HEAD_DIM = 64
CHUNK = 128
IN_WIDTH = S5_WIDTH + 2 * CM_WIDTH
D_FF = 2816
EPS = 1e-6

S5_HALVES = 2
HALF_GROUPS = S5_GROUPS // S5_HALVES
HALF_IN = S5_WIDTH // S5_HALVES
HALF_STATES = N_STATES // S5_HALVES
SUBLANES = 8
LANES = 128
SCAN_LANES = 512
SCAN_BLOCKS = SCAN_LANES // LANES
MXU_COLS = 256
TILE_ROWS = SUBLANES * CHUNK
BLOCK_ROWS = TILE_ROWS + SUBLANES
IN_ROWS = 256
MID_ROWS = 1024
GATE_ROWS = 1024
FF_CHUNK = 256
FFN_ROWS = 512
FF_STAGE_SLOTS = 4
VMEM_LIMIT_BYTES = 58 * 1024 * 1024


def _rms(x, g):
    r = lax.rsqrt(jnp.mean(x * x, axis=-1, keepdims=True) + EPS)
    return x * r * g


def _dot(a, b):
    return jnp.dot(a, b, preferred_element_type=F32)


def _load_lane_blocks(ref, n_blocks):
    return jnp.concatenate([ref[k, :TILE_ROWS, :] for k in range(n_blocks)], axis=1)


def _cast_weights(weights, slots, sems):
    plan = []
    for w_hbm, w_bf_ref in weights:
        n_rows, n_cols = w_hbm.shape
        for k in range(n_cols // LANES):
            i = len(plan)
            stage = slots[i].at[pl.ds(0, n_rows), :]
            copy = pltpu.make_async_copy(w_hbm.at[:, pl.ds(k * LANES, LANES)], stage, sems.at[i])
            plan.append((copy, stage, w_bf_ref, k))
    for copy, _, _, _ in plan:
        copy.start()
    for copy, stage, w_bf_ref, k in plan:
        copy.wait()
        w_bf_ref[:, k * LANES:(k + 1) * LANES] = stage[...].astype(BF16)


def _prep_kernel(lr_ref, li_ref, ldt_ref, br_ref, bi_ref, cr_ref, ci_ref,
                 ar_ref, ai_ref, bdr_ref, bdi_ref, cdr_ref, cdi_ref):
    dt = jnp.exp(ldt_ref[...])
    lr = lr_ref[...]
    li = li_ref[...]
    mag = jnp.exp(lr * dt)
    ar = mag * jnp.cos(li * dt)
    ai = mag * jnp.sin(li * dt)
    den = lr * lr + li * li
    qr = ((ar - 1.0) * lr + ai * li) / den
    qi = (ai * lr - (ar - 1.0) * li) / den
    for g in range(S5_GROUPS):
        ar_ref[:, g * S5_STATE:(g + 1) * S5_STATE] = ar[g:g + 1, :]
        ai_ref[:, g * S5_STATE:(g + 1) * S5_STATE] = ai[g:g + 1, :]
    qr_t = qr.T
    qi_t = qi.T

    def iota(shape, dim):
        return lax.broadcasted_iota(jnp.int32, shape, dim)

    rep_b = (iota((S5_GROUP, HALF_IN), 1) % S5_GROUP == iota((S5_GROUP, HALF_IN), 0)).astype(BF16)
    rep_c = (iota((S5_STATE, HALF_STATES), 1) % S5_STATE == iota((S5_STATE, HALF_STATES), 0)).astype(BF16)

    def repeat_lanes(x, rep):
        hi = x.astype(BF16)
        rest = x - hi.astype(F32)
        mid = rest.astype(BF16)
        lo = (rest - mid.astype(F32)).astype(BF16)
        return _dot(hi, rep) + _dot(mid, rep) + _dot(lo, rep)

    diag_b = iota((HALF_STATES, HALF_IN), 0) // S5_STATE == iota((HALF_STATES, HALF_IN), 1) // S5_GROUP
    diag_c = iota((HALF_IN, HALF_STATES), 0) // S5_GROUP == iota((HALF_IN, HALF_STATES), 1) // S5_STATE
    for hv in range(S5_HALVES):
        groups = range(hv * HALF_GROUPS, (hv + 1) * HALF_GROUPS)
        bb_r = jnp.concatenate([qr_t[:, g:g + 1] * br_ref[0, g] - qi_t[:, g:g + 1] * bi_ref[0, g]
                                for g in groups], axis=0)
        bb_i = jnp.concatenate([qr_t[:, g:g + 1] * bi_ref[0, g] + qi_t[:, g:g + 1] * br_ref[0, g]
                                for g in groups], axis=0)
        for bb, o_ref in ((bb_r, bdr_ref), (bb_i, bdi_ref)):
            wide = repeat_lanes(bb, rep_b)
            o_ref[hv] = jnp.where(diag_b, wide, 0.0).T.astype(BF16)
        for c_ref, o_ref in ((cr_ref, cdr_ref), (ci_ref, cdi_ref)):
            c_half = c_ref[hv * HALF_IN:(hv + 1) * HALF_IN, :]
            wide = repeat_lanes(c_half, rep_c)
            o_ref[hv] = jnp.where(diag_c, wide, 0.0).T.astype(BF16)


def _cm_uv(p_cm, lng_ref, lnb_ref):
    uv = jax.nn.gelu(p_cm)
    u = uv[:, :CM_WIDTH]
    vv = uv[:, CM_WIDTH:]
    mu = jnp.mean(vv, axis=-1, keepdims=True)
    vc = vv - mu
    v = vc * lax.rsqrt(jnp.mean(vc * vc, axis=-1, keepdims=True) + EPS) * lng_ref[...] + lnb_ref[...]
    return u, v


def _scan_lanes(bur_ref, bui_ref, ar_ref, ai_ref, hf, lc, h_init, rows_at, n_steps, unrolled):
    blocks = range(lc * SCAN_BLOCKS, (lc + 1) * SCAN_BLOCKS)
    glanes = slice(hf * HALF_STATES + lc * SCAN_LANES, hf * HALF_STATES + (lc + 1) * SCAN_LANES)
    a_r = jnp.broadcast_to(ar_ref[:, glanes], (SUBLANES, SCAN_LANES))
    a_i = jnp.broadcast_to(ai_ref[:, glanes], (SUBLANES, SCAN_LANES))

    def scan_step(t, carry):
        hr, hi = carry
        trows = rows_at(t)
        b_r = jnp.concatenate([bur_ref[k, trows, :] for k in blocks], axis=1)
        b_i = jnp.concatenate([bui_ref[k, trows, :] for k in blocks], axis=1)
        nr = a_r * hr - a_i * hi + b_r
        ni = a_r * hi + a_i * hr + b_i
        for n, k in enumerate(blocks):
            bur_ref[k, trows, :] = nr[:, n * LANES:(n + 1) * LANES]
            bui_ref[k, trows, :] = ni[:, n * LANES:(n + 1) * LANES]
        return nr, ni

    if unrolled:
        carry = h_init
        for t in range(n_steps):
            carry = scan_step(t, carry)
        return carry
    return lax.fori_loop(0, n_steps, scan_step, h_init, unroll=8)


def _mixer_kernel(n_prompt, dec_seq,
                  xp_hbm, xs_hbm, h0r_hbm, h0i_hbm, win_hbm, wout_hbm, wglu_hbm,
                  g1_ref, bdr_ref, bdi_ref, cdr_ref, cdi_ref, ar_ref, ai_ref, dskip_ref, bglu_ref,
                  lng_ref, lnb_ref, gs5_ref, gcm_ref, wmixp_ref, bmixp_ref, wmixs_ref, bmixs_ref,
                  x1_ref, pr_ref, pi_ref, pv_ref, sr_ref, si_ref, sv_ref,
                  xt_ref, in_sem, bur_ref, bui_ref, hsr_ref, hsi_ref, cma_ref, cmb_ref, cmc_ref,
                  xs_ref, y0_ref, win_ref, wout_ref, wglu_ref, w_sem, h0_sem):
    nb = SUBLANES
    rows = TILE_ROWS
    step = pl.program_id(0)
    slot = step % 2

    def fetch_prompt(s, sl):
        return [pltpu.make_async_copy(xp_hbm.at[b, pl.ds(s * CHUNK, CHUNK), :], xt_ref.at[sl, :, b, :],
                                      in_sem.at[sl, b]) for b in range(nb)]

    def fetch_sample(sl):
        return pltpu.make_async_copy(xs_hbm, xt_ref.at[sl], in_sem.at[sl, 0])

    @pl.when(step == 0)
    def _():
        for cp in fetch_prompt(0, 0):
            cp.start()
        staging = [ref.at[k] for ref in (bur_ref, bui_ref, cma_ref, cmb_ref, cmc_ref)
                   for k in range(ref.shape[0])]
        _cast_weights([(win_hbm, win_ref), (wout_hbm, wout_ref), (wglu_hbm, wglu_ref)], staging, w_sem)
        h0_copies = [pltpu.make_async_copy(h0r_hbm, sr_ref, h0_sem.at[0]),
                     pltpu.make_async_copy(h0i_hbm, si_ref, h0_sem.at[1])]
        for cp in h0_copies:
            cp.start()
        for cp in h0_copies:
            cp.wait()
        hsr_ref[...] = jnp.zeros_like(hsr_ref)
        hsi_ref[...] = jnp.zeros_like(hsi_ref)

    @pl.when(step + 1 < n_prompt)
    def _():
        for cp in fetch_prompt(step + 1, 1 - slot):
            cp.start()

    @pl.when(step + 1 == n_prompt)
    def _():
        fetch_sample(1 - slot).start()

    @pl.when(step < n_prompt)
    def _():
        for cp in fetch_prompt(step, slot):
            cp.wait()

    @pl.when(step == n_prompt)
    def _():
        fetch_sample(slot).wait()

    is_prompt = step < n_prompt
    is_sample = step == n_prompt
    cm_blocks = CM_WIDTH // LANES
    mxu_blocks = MXU_COLS // LANES
    us_ref, vs_ref, mx_ref = cma_ref, cmb_ref, cmc_ref
    def row_blocks(n):
        return [slice(r * n, (r + 1) * n) for r in range(rows // n)]

    def x_rows(rsl):
        groups = slice(rsl.start // nb, rsl.stop // nb)
        return xt_ref[slot, groups].reshape(rsl.stop - rsl.start, D_MODEL)

    def b_proj(hf, xs_h, rsl):
        for n in range(HALF_STATES // MXU_COLS):
            cols = slice(n * MXU_COLS, (n + 1) * MXU_COLS)
            b_r = _dot(xs_h, bdr_ref[hf, :, cols])
            b_i = _dot(xs_h, bdi_ref[hf, :, cols])
            for q in range(mxu_blocks):
                bur_ref[n * mxu_blocks + q, rsl, :] = b_r[:, q * LANES:(q + 1) * LANES]
                bui_ref[n * mxu_blocks + q, rsl, :] = b_i[:, q * LANES:(q + 1) * LANES]

    def c_proj(hf, rsl):
        y_h = None
        for n in range(HALF_STATES // MXU_COLS):
            krows = slice(n * MXU_COLS, (n + 1) * MXU_COLS)
            blocks = range(n * mxu_blocks, (n + 1) * mxu_blocks)
            h_r = jnp.concatenate([bur_ref[k, rsl, :] for k in blocks], axis=1).astype(BF16)
            h_i = jnp.concatenate([bui_ref[k, rsl, :] for k in blocks], axis=1).astype(BF16)
            part = _dot(h_r, cdr_ref[hf, krows, :]) - _dot(h_i, cdi_ref[hf, krows, :])
            y_h = part if y_h is None else y_h + part
        return y_h

    def scan_half(hf):
        @pl.when(is_prompt)
        def _():
            for lc in range(HALF_STATES // SCAN_LANES):
                glanes = slice(hf * HALF_STATES + lc * SCAN_LANES, hf * HALF_STATES + (lc + 1) * SCAN_LANES)
                hr, hi = _scan_lanes(bur_ref, bui_ref, ar_ref, ai_ref, hf, lc,
                                     (hsr_ref[:, glanes], hsi_ref[:, glanes]),
                                     lambda t: pl.ds(pl.multiple_of(t * nb, nb), nb), CHUNK, False)
                hsr_ref[:, glanes] = hr
                hsi_ref[:, glanes] = hi
                pr_ref[:, glanes] = hr
                pi_ref[:, glanes] = hi

        @pl.when(is_sample)
        def _():
            tt = dec_seq

            def scan_group(sg, carry):
                srows = pl.ds(pl.multiple_of(sg * SUBLANES, SUBLANES), SUBLANES)
                for lc in range(HALF_STATES // SCAN_LANES):
                    glanes = slice(hf * HALF_STATES + lc * SCAN_LANES,
                                   hf * HALF_STATES + (lc + 1) * SCAN_LANES)
                    hr, hi = _scan_lanes(bur_ref, bui_ref, ar_ref, ai_ref, hf, lc,
                                         (sr_ref[srows, glanes], si_ref[srows, glanes]),
                                         lambda t: pl.ds(sg * (SUBLANES * tt) + t, SUBLANES, stride=tt),
                                         tt, True)
                    sr_ref[srows, glanes] = hr
                    si_ref[srows, glanes] = hi
                return carry

            lax.fori_loop(0, rows // tt // SUBLANES, scan_group, 0)

    for rsl in row_blocks(IN_ROWS):
        h = _rms(x_rows(rsl), g1_ref[...]).astype(BF16)
        u, v = _cm_uv(_dot(h, win_ref[:, S5_WIDTH:]), lng_ref, lnb_ref)
        for k in range(cm_blocks):
            us_ref[k, rsl, :] = u[:, k * LANES:(k + 1) * LANES]
            vs_ref[k, rsl, :] = v[:, k * LANES:(k + 1) * LANES]
        xs = _dot(h, win_ref[:, :S5_WIDTH])
        xs_ref[rsl, :] = xs
        b_proj(0, xs[:, :HALF_IN].astype(BF16), rsl)

    r_idx = lax.broadcasted_iota(jnp.int32, (2 * CHUNK, CHUNK), 0)
    c_idx = lax.broadcasted_iota(jnp.int32, (2 * CHUNK, CHUNK), 1)
    causal = (r_idx % CHUNK) >= c_idx
    first_head = lax.broadcasted_iota(jnp.int32, (CHUNK, LANES), 1) < CM_HEAD_DIM
    for j in range(CM_HEADS // 2):
        w_pair = jnp.where(causal, wmixp_ref[j], 0.0).astype(BF16)
        bias = bmixp_ref[:, j * LANES:(j + 1) * LANES]
        for b in range(nb):
            seq_rows = pl.ds(b, CHUNK, stride=nb)
            r = _dot(w_pair, vs_ref[j, seq_rows, :].astype(BF16))
            mixed = jnp.where(first_head, r[:CHUNK], r[CHUNK:]) + bias
            mx_ref[j, seq_rows, :] = us_ref[j, seq_rows, :] * mixed

    @pl.when(step == n_prompt - 1)
    def _():
        for j in range(cm_blocks):
            for b in range(nb):
                pv_ref[b, :, j * LANES:(j + 1) * LANES] = vs_ref[j, pl.ds(b, CHUNK, stride=nb), :]

    @pl.when(is_sample)
    def _():
        tt = dec_seq
        n_seq = rows // tt
        sv_ref[...] = _load_lane_blocks(vs_ref, cm_blocks).reshape(n_seq, tt, CM_WIDTH)
        def mix_block(k, carry):
            v_at = [vs_ref[k, pl.ds(s, n_seq, stride=tt), :] for s in range(tt)]
            for t in range(tt):
                acc = bmixs_ref[k, t:t + 1, :] + wmixs_ref[k, t, 0:1, :] * v_at[0]
                for s in range(1, t + 1):
                    acc = acc + wmixs_ref[k, t, s:s + 1, :] * v_at[s]
                trows = pl.ds(t, n_seq, stride=tt)
                mx_ref[k, trows, :] = us_ref[k, trows, :] * acc
            return carry

        lax.fori_loop(0, cm_blocks, mix_block, 0)

    scan_half(0)
    for rsl in row_blocks(MID_ROWS):
        y0_ref[rsl, :] = c_proj(0, rsl)
        b_proj(1, xs_ref[rsl, HALF_IN:].astype(BF16), rsl)
        o_cm = jnp.concatenate([mx_ref[k, rsl, :] for k in range(cm_blocks)], axis=1)
        n_cm = _rms(o_cm, gcm_ref[...]).astype(BF16)
        x1_ref[rsl, :] = x_rows(rsl) + _dot(n_cm, wout_ref[S5_WIDTH:, :])
    scan_half(1)
    for rsl in row_blocks(GATE_ROWS):
        y = jnp.concatenate([y0_ref[rsl, :], c_proj(1, rsl)], axis=1) + dskip_ref[...] * xs_ref[rsl, :]
        g = jax.nn.gelu(y)
        o_s5 = g * jax.nn.sigmoid(_dot(g.astype(BF16), wglu_ref[...]) + bglu_ref[...])
        n_s5 = _rms(o_s5, gs5_ref[...]).astype(BF16)
        x1_ref[rsl, :] = x1_ref[rsl, :] + _dot(n_s5, wout_ref[:S5_WIDTH, :])


def _const_spec(shape):
    nd = len(shape)
    return pl.BlockSpec(shape, lambda i, nd=nd: (0,) * nd, pipeline_mode=pl.Buffered(1))


_ANY_SPEC = pl.BlockSpec(memory_space=pl.ANY)


def _mixer(x_prompt, x_sample, h0, big_w, small_w):
    nb, seq, _ = x_prompt.shape
    dec_batch, dec_seq, _ = x_sample.shape
    assert nb == SUBLANES and seq % CHUNK == 0
    assert dec_batch * dec_seq == TILE_ROWS and dec_seq <= SUBLANES and dec_batch == CHUNK
    n_prompt = seq // CHUNK
    w_in, w_out, w_glu = big_w
    in_specs = [_ANY_SPEC] * 7 + [_const_spec(a.shape) for a in small_w]
    out_shape = (
        jax.ShapeDtypeStruct(((n_prompt + 1) * TILE_ROWS, D_MODEL), F32),
        jax.ShapeDtypeStruct((nb, N_STATES), F32),
        jax.ShapeDtypeStruct((nb, N_STATES), F32),
        jax.ShapeDtypeStruct((nb, CHUNK, CM_WIDTH), F32),
        jax.ShapeDtypeStruct((dec_batch, N_STATES), F32),
        jax.ShapeDtypeStruct((dec_batch, N_STATES), F32),
        jax.ShapeDtypeStruct((dec_batch, dec_seq, CM_WIDTH), F32),
    )
    out_specs = (pl.BlockSpec((TILE_ROWS, D_MODEL), lambda i: (i, 0)),) + tuple(
        _const_spec(o.shape) for o in out_shape[1:])
    scratch = [
        pltpu.VMEM((2, CHUNK, nb, D_MODEL), F32),
        pltpu.SemaphoreType.DMA((2, nb)),
        pltpu.VMEM((HALF_STATES // LANES, BLOCK_ROWS, LANES), F32),
        pltpu.VMEM((HALF_STATES // LANES, BLOCK_ROWS, LANES), F32),
        pltpu.VMEM((nb, N_STATES), F32),
        pltpu.VMEM((nb, N_STATES), F32),
        pltpu.VMEM((CM_WIDTH // LANES, BLOCK_ROWS, LANES), F32),
        pltpu.VMEM((CM_WIDTH // LANES, BLOCK_ROWS, LANES), F32),
        pltpu.VMEM((CM_WIDTH // LANES, BLOCK_ROWS, LANES), F32),
        pltpu.VMEM((TILE_ROWS, S5_WIDTH), F32),
        pltpu.VMEM((TILE_ROWS, HALF_IN), F32),
        pltpu.VMEM(w_in.shape, BF16),
        pltpu.VMEM(w_out.shape, BF16),
        pltpu.VMEM(w_glu.shape, BF16),
        pltpu.SemaphoreType.DMA(((w_in.shape[1] + w_out.shape[1] + w_glu.shape[1]) // LANES,)),
        pltpu.SemaphoreType.DMA((2,)),
    ]
    return pl.pallas_call(
        functools.partial(_mixer_kernel, n_prompt, dec_seq),
        grid=(n_prompt + 1,),
        in_specs=in_specs,
        out_specs=out_specs,
        out_shape=out_shape,
        scratch_shapes=scratch,
        compiler_params=pltpu.CompilerParams(
            dimension_semantics=("arbitrary",), vmem_limit_bytes=VMEM_LIMIT_BYTES),
        name="mixer",
    )(x_prompt, x_sample, *h0, w_in, w_out, w_glu, *small_w)


def _ffn_kernel(nb, n_prompt, x_ref, g2_ref, gf_ref, wg_hbm, wu_hbm, wd_hbm, yp_hbm, ys_hbm,
                yt_ref, out_sem, wg_ref, wu_ref, wd_ref, stage_gu_ref, stage_d_ref, w_sem):
    tt = TILE_ROWS // nb
    step = pl.program_id(0)
    n_steps = pl.num_programs(0)
    slot = step % 2
    n_chunks = D_FF // FF_CHUNK

    def prompt_writeback(s, sl):
        return [pltpu.make_async_copy(yt_ref.at[sl, :, b, :], yp_hbm.at[b, pl.ds(s * tt, tt), :],
                                      out_sem.at[sl, b]) for b in range(nb)]

    def sample_writeback(sl):
        return [pltpu.make_async_copy(yt_ref.at[sl], ys_hbm, out_sem.at[sl, 0])]

    def wait_writeback(s, sl):
        @pl.when(s < n_prompt)
        def _():
            for cp in prompt_writeback(s, sl):
                cp.wait()

        @pl.when(s >= n_prompt)
        def _():
            for cp in sample_writeback(sl):
                cp.wait()

    def weight_chunk_copies(c, sl):
        cols = pl.ds(c * FF_CHUNK, FF_CHUNK)
        return [pltpu.make_async_copy(wg_hbm.at[:, cols], stage_gu_ref.at[sl, 0], w_sem.at[sl, 0]),
                pltpu.make_async_copy(wu_hbm.at[:, cols], stage_gu_ref.at[sl, 1], w_sem.at[sl, 1]),
                pltpu.make_async_copy(wd_hbm.at[cols, :], stage_d_ref.at[sl], w_sem.at[sl, 2])]

    def stage_in_chunk(c):
        sl = c % FF_STAGE_SLOTS
        cols = slice(c * FF_CHUNK, (c + 1) * FF_CHUNK)
        for cp in weight_chunk_copies(c, sl):
            cp.wait()
        wg_ref[:, cols] = stage_gu_ref[sl, 0].astype(BF16)
        wu_ref[:, cols] = stage_gu_ref[sl, 1].astype(BF16)
        wd_ref[cols, :] = stage_d_ref[sl].astype(BF16)
        if c + FF_STAGE_SLOTS < n_chunks:
            for cp in weight_chunk_copies(c + FF_STAGE_SLOTS, sl):
                cp.start()

    @pl.when(step == 0)
    def _():
        for c in range(FF_STAGE_SLOTS):
            for cp in weight_chunk_copies(c, c):
                cp.start()
        for c in range(n_chunks):
            stage_in_chunk(c)

    @pl.when(step >= 2)
    def _():
        wait_writeback(step - 2, slot)

    halves = [slice(r, r + FFN_ROWS) for r in range(0, TILE_ROWS, FFN_ROWS)]
    hs = [_rms(x_ref[rsl, :], g2_ref[...]).astype(BF16) for rsl in halves]
    accs = [x_ref[rsl, :] for rsl in halves]
    for c in range(n_chunks):
        cols = slice(c * FF_CHUNK, (c + 1) * FF_CHUNK)
        for i in range(len(halves)):
            gate = _dot(hs[i], wg_ref[:, cols])
            up = _dot(hs[i], wu_ref[:, cols])
            accs[i] = accs[i] + _dot((jax.nn.silu(gate) * up).astype(BF16), wd_ref[cols, :])
    for i, rsl in enumerate(halves):
        groups = slice(rsl.start // nb, rsl.stop // nb)
        yt_ref[slot, groups] = _rms(accs[i], gf_ref[...]).reshape(groups.stop - groups.start, nb, D_MODEL)

    @pl.when(step < n_prompt)
    def _():
        for cp in prompt_writeback(step, slot):
            cp.start()

    @pl.when(step >= n_prompt)
    def _():
        for cp in sample_writeback(slot):
            cp.start()

    @pl.when(step == n_steps - 1)
    def _():
        wait_writeback(step - 1, 1 - slot)
        wait_writeback(step, slot)


def _ffn(x1, g2, gf, ffn_w, *, nb, seq):
    n = x1.shape[0]
    n_prompt = nb * seq // TILE_ROWS
    assert n == (n_prompt + 1) * TILE_ROWS and TILE_ROWS % nb == 0 and n_prompt >= 1
    tt = TILE_ROWS // nb
    w_gate, w_up, w_down = ffn_w
    return pl.pallas_call(
        functools.partial(_ffn_kernel, nb, n_prompt),
        grid=(n_prompt + 1,),
        in_specs=[pl.BlockSpec((TILE_ROWS, D_MODEL), lambda i: (i, 0)),
                  _const_spec(g2.shape), _const_spec(gf.shape)] + [_ANY_SPEC] * 3,
        out_specs=(_ANY_SPEC, _ANY_SPEC),
        out_shape=(jax.ShapeDtypeStruct((nb, seq, D_MODEL), F32),
                   jax.ShapeDtypeStruct((tt, nb, D_MODEL), F32)),
        scratch_shapes=[pltpu.VMEM((2, tt, nb, D_MODEL), F32),
                        pltpu.SemaphoreType.DMA((2, nb)),
                        pltpu.VMEM(w_gate.shape, BF16),
                        pltpu.VMEM(w_up.shape, BF16),
                        pltpu.VMEM(w_down.shape, BF16),
                        pltpu.VMEM((FF_STAGE_SLOTS, 2, D_MODEL, FF_CHUNK), F32),
                        pltpu.VMEM((FF_STAGE_SLOTS, FF_CHUNK, D_MODEL), F32),
                        pltpu.SemaphoreType.DMA((FF_STAGE_SLOTS, 3))],
        compiler_params=pltpu.CompilerParams(
            dimension_semantics=("arbitrary",), vmem_limit_bytes=VMEM_LIMIT_BYTES),
        name="ffn",
    )(x1, g2, gf, w_gate, w_up, w_down)


def kernel(x_prompt, x_sample, state_s5_re, state_s5_im, norm1, w_in, lam_re, lam_im, log_dt, b_re, b_im, c_re, c_im, d_skip, w_glu, b_glu, cm_ln_g, cm_ln_b, w_s, b_s, g_s5, g_cm, w_out, norm2, w_gate, w_up, w_down, norm_f):
    depth = norm1.shape[0]
    assert depth == 1
    l = 0
    batch, seq, _ = x_prompt.shape
    dec_batch, dec_seq, _ = x_sample.shape

    ar, ai, bdr, bdi, cdr, cdi = pl.pallas_call(
        _prep_kernel,
        out_shape=(
            jax.ShapeDtypeStruct((1, N_STATES), F32),
            jax.ShapeDtypeStruct((1, N_STATES), F32),
            jax.ShapeDtypeStruct((S5_HALVES, HALF_IN, HALF_STATES), BF16),
            jax.ShapeDtypeStruct((S5_HALVES, HALF_IN, HALF_STATES), BF16),
            jax.ShapeDtypeStruct((S5_HALVES, HALF_STATES, HALF_IN), BF16),
            jax.ShapeDtypeStruct((S5_HALVES, HALF_STATES, HALF_IN), BF16),
        ),
        name="s5_prepare",
    )(lam_re[l], lam_im[l], log_dt[l][:, None], b_re, b_im,
      c_re[l].reshape(S5_WIDTH, S5_STATE), c_im[l].reshape(S5_WIDTH, S5_STATE))

    row = lambda a: a.reshape(1, -1)
    wmix_p = w_s[l].reshape(CM_HEADS // 2, 2 * CHUNK, CHUNK)
    bmix_p = jnp.repeat(b_s[l].T, CM_HEAD_DIM, axis=1)
    lane_blocks = lambda a: jnp.moveaxis(a.reshape(a.shape[:-1] + (CM_WIDTH // LANES, LANES)), -2, 0)
    wmix_s = lane_blocks(jnp.repeat(jnp.transpose(w_s[l][:, :dec_seq, :dec_seq], (1, 2, 0)),
                                    CM_HEAD_DIM, axis=2))
    bmix_s = lane_blocks(bmix_p[:dec_seq])
    small_w = [row(norm1[l]), bdr, bdi, cdr, cdi, ar, ai, row(d_skip[l]), row(b_glu[l]),
               row(cm_ln_g[l]), row(cm_ln_b[l]), row(g_s5[l]), row(g_cm[l]),
               wmix_p, bmix_p, wmix_s, bmix_s]
    h0 = (state_s5_re[l].reshape(dec_batch, N_STATES), state_s5_im[l].reshape(dec_batch, N_STATES))

    x1, pr, pi_, pv, sr, si, sv = _mixer(x_prompt, x_sample, h0, (w_in[l], w_out[l], w_glu[l]), small_w)
    y_prompt, y_sample = _ffn(x1, row(norm2[l]), row(norm_f), (w_gate[l], w_up[l], w_down[l]),
                              nb=batch, seq=seq)

    st = lambda a, n: a.reshape(1, n, S5_GROUPS, S5_STATE)
    return (y_prompt, y_sample.reshape(dec_batch, dec_seq, D_MODEL),
            st(pr, batch), st(pi_, batch), pv[None],
            st(sr, dec_batch), st(si, dec_batch), sv[None])
```

```python
import functools

import jax
import jax.numpy as jnp
from jax import lax
from jax.experimental import pallas as pl
from jax.experimental.pallas import tpu as pltpu

F32 = jnp.float32
BF16 = jnp.bfloat16

D_MODEL = 1024
S5_WIDTH = 512
S5_GROUP = 16
S5_GROUPS = 32
S5_STATE = 64
N_STATES = S5_GROUPS * S5_STATE
CM_WIDTH = 512
CM_HEADS = 8
CM_HEAD_DIM = 64
CHUNK = 128
IN_WIDTH = S5_WIDTH + 2 * CM_WIDTH
D_FF = 2816
EPS = 1e-6

S5_HALVES = 2
HALF_GROUPS = S5_GROUPS // S5_HALVES
HALF_IN = S5_WIDTH // S5_HALVES
HALF_STATES = N_STATES // S5_HALVES
SUBLANES = 8
LANES = 128
SCAN_LANES = 512
SCAN_BLOCKS = SCAN_LANES // LANES
MXU_COLS = 256
TILE_ROWS = SUBLANES * CHUNK
BLOCK_ROWS = TILE_ROWS + SUBLANES
IN_ROWS = 1024
MID_ROWS = 1024
GATE_ROWS = 1024
FF_CHUNK = 256
FFN_ROWS = 1024
FF_STAGE_SLOTS = 4
VMEM_LIMIT_BYTES = 58 * 1024 * 1024


def _rms(x, g):
    r = lax.rsqrt(jnp.mean(x * x, axis=-1, keepdims=True) + EPS)
    return x * r * g


def _dot(a, b):
    return jnp.dot(a, b, preferred_element_type=F32)


def _load_lane_blocks(ref, n_blocks):
    return jnp.concatenate([ref[k, :TILE_ROWS, :] for k in range(n_blocks)], axis=1)


def _cast_weights(weights, slots, sems):
    plan = []
    for w_hbm, w_bf_ref in weights:
        n_rows, n_cols = w_hbm.shape
        for k in range(n_cols // LANES):
            i = len(plan)
            stage = slots[i].at[pl.ds(0, n_rows), :]
            copy = pltpu.make_async_copy(w_hbm.at[:, pl.ds(k * LANES, LANES)], stage, sems.at[i])
            plan.append((copy, stage, w_bf_ref, k))
    for copy, _, _, _ in plan:
        copy.start()
    for copy, stage, w_bf_ref, k in plan:
        copy.wait()
        w_bf_ref[:, k * LANES:(k + 1) * LANES] = stage[...].astype(BF16)


def _prep_kernel(lr_ref, li_ref, ldt_ref, br_ref, bi_ref, cr_ref, ci_ref,
                 ar_ref, ai_ref, bdr_ref, bdi_ref, cdr_ref, cdi_ref):
    dt = jnp.exp(ldt_ref[...])
    lr = lr_ref[...]
    li = li_ref[...]
    mag = jnp.exp(lr * dt)
    ar = mag * jnp.cos(li * dt)
    ai = mag * jnp.sin(li * dt)
    den = lr * lr + li * li
    qr = ((ar - 1.0) * lr + ai * li) / den
    qi = (ai * lr - (ar - 1.0) * li) / den
    for g in range(S5_GROUPS):
        ar_ref[:, g * S5_STATE:(g + 1) * S5_STATE] = ar[g:g + 1, :]
        ai_ref[:, g * S5_STATE:(g + 1) * S5_STATE] = ai[g:g + 1, :]
    qr_t = qr.T
    qi_t = qi.T

    def iota(shape, dim):
        return lax.broadcasted_iota(jnp.int32, shape, dim)

    rep_b = (iota((S5_GROUP, HALF_IN), 1) % S5_GROUP == iota((S5_GROUP, HALF_IN), 0)).astype(BF16)
    rep_c = (iota((S5_STATE, HALF_STATES), 1) % S5_STATE == iota((S5_STATE, HALF_STATES), 0)).astype(BF16)

    def repeat_lanes(x, rep):
        hi = x.astype(BF16)
        rest = x - hi.astype(F32)
        mid = rest.astype(BF16)
        lo = (rest - mid.astype(F32)).astype(BF16)
        return _dot(hi, rep) + _dot(mid, rep) + _dot(lo, rep)

    diag_b = iota((HALF_STATES, HALF_IN), 0) // S5_STATE == iota((HALF_STATES, HALF_IN), 1) // S5_GROUP
    diag_c = iota((HALF_IN, HALF_STATES), 0) // S5_GROUP == iota((HALF_IN, HALF_STATES), 1) // S5_STATE
    for hv in range(S5_HALVES):
        groups = range(hv * HALF_GROUPS, (hv + 1) * HALF_GROUPS)
        bb_r = jnp.concatenate([qr_t[:, g:g + 1] * br_ref[0, g] - qi_t[:, g:g + 1] * bi_ref[0, g]
                                for g in groups], axis=0)
        bb_i = jnp.concatenate([qr_t[:, g:g + 1] * bi_ref[0, g] + qi_t[:, g:g + 1] * br_ref[0, g]
                                for g in groups], axis=0)
        for bb, o_ref in ((bb_r, bdr_ref), (bb_i, bdi_ref)):
            wide = repeat_lanes(bb, rep_b)
            o_ref[hv] = jnp.where(diag_b, wide, 0.0).T.astype(BF16)
        for c_ref, o_ref in ((cr_ref, cdr_ref), (ci_ref, cdi_ref)):
            c_half = c_ref[hv * HALF_IN:(hv + 1) * HALF_IN, :]
            wide = repeat_lanes(c_half, rep_c)
            o_ref[hv] = jnp.where(diag_c, wide, 0.0).T.astype(BF16)


def _cm_uv(p_cm, lng_ref, lnb_ref):
    uv = jax.nn.gelu(p_cm)
    u = uv[:, :CM_WIDTH]
    vv = uv[:, CM_WIDTH:]
    mu = jnp.mean(vv, axis=-1, keepdims=True)
    vc = vv - mu
    v = vc * lax.rsqrt(jnp.mean(vc * vc, axis=-1, keepdims=True) + EPS) * lng_ref[...] + lnb_ref[...]
    return u, v


def _scan_lanes(bur_ref, bui_ref, ar_ref, ai_ref, hf, lc, h_init, rows_at, n_steps, unrolled):
    blocks = range(lc * SCAN_BLOCKS, (lc + 1) * SCAN_BLOCKS)
    glanes = slice(hf * HALF_STATES + lc * SCAN_LANES, hf * HALF_STATES + (lc + 1) * SCAN_LANES)
    a_r = jnp.broadcast_to(ar_ref[:, glanes], (SUBLANES, SCAN_LANES))
    a_i = jnp.broadcast_to(ai_ref[:, glanes], (SUBLANES, SCAN_LANES))

    def scan_step(t, carry):
        hr, hi = carry
        trows = rows_at(t)
        b_r = jnp.concatenate([bur_ref[k, trows, :] for k in blocks], axis=1)
        b_i = jnp.concatenate([bui_ref[k, trows, :] for k in blocks], axis=1)
        nr = a_r * hr - a_i * hi + b_r
        ni = a_r * hi + a_i * hr + b_i
        for n, k in enumerate(blocks):
            bur_ref[k, trows, :] = nr[:, n * LANES:(n + 1) * LANES]
            bui_ref[k, trows, :] = ni[:, n * LANES:(n + 1) * LANES]
        return nr, ni

    if unrolled:
        carry = h_init
        for t in range(n_steps):
            carry = scan_step(t, carry)
        return carry
    return lax.fori_loop(0, n_steps, scan_step, h_init, unroll=8)


def _mixer_kernel(n_prompt, dec_seq,
                  xp_hbm, xs_hbm, h0r_hbm, h0i_hbm, win_hbm, wout_hbm, wglu_hbm,
                  g1_ref, bdr_ref, bdi_ref, cdr_ref, cdi_ref, ar_ref, ai_ref, dskip_ref, bglu_ref,
                  lng_ref, lnb_ref, gs5_ref, gcm_ref, wmixp_ref, bmixp_ref, wmixs_ref, bmixs_ref,
                  x1_ref, pr_ref, pi_ref, pv_ref, sr_ref, si_ref, sv_ref,
                  xt_ref, in_sem, bur_ref, bui_ref, hsr_ref, hsi_ref, cma_ref, cmb_ref, cmc_ref,
                  xs_ref, y0_ref, win_ref, wout_ref, wglu_ref, w_sem, h0_sem):
    nb = SUBLANES
    rows = TILE_ROWS
    step = pl.program_id(0)
    slot = step % 2

    def fetch_prompt(s, sl):
        return [pltpu.make_async_copy(xp_hbm.at[b, pl.ds(s * CHUNK, CHUNK), :], xt_ref.at[sl, :, b, :],
                                      in_sem.at[sl, b]) for b in range(nb)]

    def fetch_sample(sl):
        return pltpu.make_async_copy(xs_hbm, xt_ref.at[sl], in_sem.at[sl, 0])

    @pl.when(step == 0)
    def _():
        for cp in fetch_prompt(0, 0):
            cp.start()
        staging = [ref.at[k] for ref in (bur_ref, bui_ref, cma_ref, cmb_ref, cmc_ref)
                   for k in range(ref.shape[0])]
        _cast_weights([(win_hbm, win_ref), (wout_hbm, wout_ref), (wglu_hbm, wglu_ref)], staging, w_sem)
        h0_copies = [pltpu.make_async_copy(h0r_hbm, sr_ref, h0_sem.at[0]),
                     pltpu.make_async_copy(h0i_hbm, si_ref, h0_sem.at[1])]
        for cp in h0_copies:
            cp.start()
        for cp in h0_copies:
            cp.wait()
        hsr_ref[...] = jnp.zeros_like(hsr_ref)
        hsi_ref[...] = jnp.zeros_like(hsi_ref)

    @pl.when(step + 1 < n_prompt)
    def _():
        for cp in fetch_prompt(step + 1, 1 - slot):
            cp.start()

    @pl.when(step + 1 == n_prompt)
    def _():
        fetch_sample(1 - slot).start()

    @pl.when(step < n_prompt)
    def _():
        for cp in fetch_prompt(step, slot):
            cp.wait()

    @pl.when(step == n_prompt)
    def _():
        fetch_sample(slot).wait()

    is_prompt = step < n_prompt
    is_sample = step == n_prompt
    cm_blocks = CM_WIDTH // LANES
    mxu_blocks = MXU_COLS // LANES
    us_ref, vs_ref, mx_ref = cma_ref, cmb_ref, cmc_ref
    def row_blocks(n):
        return [slice(r * n, (r + 1) * n) for r in range(rows // n)]

    def x_rows(rsl):
        groups = slice(rsl.start // nb, rsl.stop // nb)
        return xt_ref[slot, groups].reshape(rsl.stop - rsl.start, D_MODEL)

    def b_proj(hf, xs_h, rsl):
        for n in range(HALF_STATES // MXU_COLS):
            cols = slice(n * MXU_COLS, (n + 1) * MXU_COLS)
            b_r = _dot(xs_h, bdr_ref[hf, :, cols])
            b_i = _dot(xs_h, bdi_ref[hf, :, cols])
            for q in range(mxu_blocks):
                bur_ref[n * mxu_blocks + q, rsl, :] = b_r[:, q * LANES:(q + 1) * LANES]
                bui_ref[n * mxu_blocks + q, rsl, :] = b_i[:, q * LANES:(q + 1) * LANES]

    def c_proj(hf, rsl):
        y_h = None
        for n in range(HALF_STATES // MXU_COLS):
            krows = slice(n * MXU_COLS, (n + 1) * MXU_COLS)
            blocks = range(n * mxu_blocks, (n + 1) * mxu_blocks)
            h_r = jnp.concatenate([bur_ref[k, rsl, :] for k in blocks], axis=1).astype(BF16)
            h_i = jnp.concatenate([bui_ref[k, rsl, :] for k in blocks], axis=1).astype(BF16)
            part = _dot(h_r, cdr_ref[hf, krows, :]) - _dot(h_i, cdi_ref[hf, krows, :])
            y_h = part if y_h is None else y_h + part
        return y_h

    def scan_half(hf):
        @pl.when(is_prompt)
        def _():
            for lc in range(HALF_STATES // SCAN_LANES):
                glanes = slice(hf * HALF_STATES + lc * SCAN_LANES, hf * HALF_STATES + (lc + 1) * SCAN_LANES)
                hr, hi = _scan_lanes(bur_ref, bui_ref, ar_ref, ai_ref, hf, lc,
                                     (hsr_ref[:, glanes], hsi_ref[:, glanes]),
                                     lambda t: pl.ds(pl.multiple_of(t * nb, nb), nb), CHUNK, False)
                hsr_ref[:, glanes] = hr
                hsi_ref[:, glanes] = hi
                pr_ref[:, glanes] = hr
                pi_ref[:, glanes] = hi

        @pl.when(is_sample)
        def _():
            tt = dec_seq

            def scan_group(sg, carry):
                srows = pl.ds(pl.multiple_of(sg * SUBLANES, SUBLANES), SUBLANES)
                for lc in range(HALF_STATES // SCAN_LANES):
                    glanes = slice(hf * HALF_STATES + lc * SCAN_LANES,
                                   hf * HALF_STATES + (lc + 1) * SCAN_LANES)
                    hr, hi = _scan_lanes(bur_ref, bui_ref, ar_ref, ai_ref, hf, lc,
                                         (sr_ref[srows, glanes], si_ref[srows, glanes]),
                                         lambda t: pl.ds(sg * (SUBLANES * tt) + t, SUBLANES, stride=tt),
                                         tt, True)
                    sr_ref[srows, glanes] = hr
                    si_ref[srows, glanes] = hi
                return carry

            lax.fori_loop(0, rows // tt // SUBLANES, scan_group, 0)

    for rsl in row_blocks(IN_ROWS):
        h = _rms(x_rows(rsl), g1_ref[...]).astype(BF16)
        u, v = _cm_uv(_dot(h, win_ref[:, S5_WIDTH:]), lng_ref, lnb_ref)
        for k in range(cm_blocks):
            us_ref[k, rsl, :] = u[:, k * LANES:(k + 1) * LANES]
            vs_ref[k, rsl, :] = v[:, k * LANES:(k + 1) * LANES]
        xs = _dot(h, win_ref[:, :S5_WIDTH])
        xs_ref[rsl, :] = xs
        b_proj(0, xs[:, :HALF_IN].astype(BF16), rsl)

    r_idx = lax.broadcasted_iota(jnp.int32, (2 * CHUNK, CHUNK), 0)
    c_idx = lax.broadcasted_iota(jnp.int32, (2 * CHUNK, CHUNK), 1)
    causal = (r_idx % CHUNK) >= c_idx
    first_head = lax.broadcasted_iota(jnp.int32, (CHUNK, LANES), 1) < CM_HEAD_DIM
    for j in range(CM_HEADS // 2):
        w_pair = jnp.where(causal, wmixp_ref[j], 0.0).astype(BF16)
        bias = bmixp_ref[:, j * LANES:(j + 1) * LANES]
        for b in range(nb):
            seq_rows = pl.ds(b, CHUNK, stride=nb)
            r = _dot(w_pair, vs_ref[j, seq_rows, :].astype(BF16))
            mixed = jnp.where(first_head, r[:CHUNK], r[CHUNK:]) + bias
            mx_ref[j, seq_rows, :] = us_ref[j, seq_rows, :] * mixed

    @pl.when(step == n_prompt - 1)
    def _():
        for j in range(cm_blocks):
            for b in range(nb):
                pv_ref[b, :, j * LANES:(j + 1) * LANES] = vs_ref[j, pl.ds(b, CHUNK, stride=nb), :]

    @pl.when(is_sample)
    def _():
        tt = dec_seq
        n_seq = rows // tt
        sv_ref[...] = _load_lane_blocks(vs_ref, cm_blocks).reshape(n_seq, tt, CM_WIDTH)
        def mix_block(k, carry):
            v_at = [vs_ref[k, pl.ds(s, n_seq, stride=tt), :] for s in range(tt)]
            for t in range(tt):
                acc = bmixs_ref[k, t:t + 1, :] + wmixs_ref[k, t, 0:1, :] * v_at[0]
                for s in range(1, t + 1):
                    acc = acc + wmixs_ref[k, t, s:s + 1, :] * v_at[s]
                trows = pl.ds(t, n_seq, stride=tt)
                mx_ref[k, trows, :] = us_ref[k, trows, :] * acc
            return carry

        lax.fori_loop(0, cm_blocks, mix_block, 0)

    scan_half(0)
    for rsl in row_blocks(MID_ROWS):
        y0_ref[rsl, :] = c_proj(0, rsl)
        b_proj(1, xs_ref[rsl, HALF_IN:].astype(BF16), rsl)
        o_cm = jnp.concatenate([mx_ref[k, rsl, :] for k in range(cm_blocks)], axis=1)
        n_cm = _rms(o_cm, gcm_ref[...]).astype(BF16)
        x1_ref[rsl, :] = x_rows(rsl) + _dot(n_cm, wout_ref[S5_WIDTH:, :])
    scan_half(1)
    for rsl in row_blocks(GATE_ROWS):
        y = jnp.concatenate([y0_ref[rsl, :], c_proj(1, rsl)], axis=1) + dskip_ref[...] * xs_ref[rsl, :]
        g = jax.nn.gelu(y)
        o_s5 = g * jax.nn.sigmoid(_dot(g.astype(BF16), wglu_ref[...]) + bglu_ref[...])
        n_s5 = _rms(o_s5, gs5_ref[...]).astype(BF16)
        x1_ref[rsl, :] = x1_ref[rsl, :] + _dot(n_s5, wout_ref[:S5_WIDTH, :])


def _const_spec(shape):
    nd = len(shape)
    return pl.BlockSpec(shape, lambda i, nd=nd: (0,) * nd, pipeline_mode=pl.Buffered(1))


_ANY_SPEC = pl.BlockSpec(memory_space=pl.ANY)


def _mixer(x_prompt, x_sample, h0, big_w, small_w):
    nb, seq, _ = x_prompt.shape
    dec_batch, dec_seq, _ = x_sample.shape
    assert nb == SUBLANES and seq % CHUNK == 0
    assert dec_batch * dec_seq == TILE_ROWS and dec_seq <= SUBLANES and dec_batch == CHUNK
    n_prompt = seq // CHUNK
    w_in, w_out, w_glu = big_w
    in_specs = [_ANY_SPEC] * 7 + [_const_spec(a.shape) for a in small_w]
    out_shape = (
        jax.ShapeDtypeStruct(((n_prompt + 1) * TILE_ROWS, D_MODEL), F32),
        jax.ShapeDtypeStruct((nb, N_STATES), F32),
        jax.ShapeDtypeStruct((nb, N_STATES), F32),
        jax.ShapeDtypeStruct((nb, CHUNK, CM_WIDTH), F32),
        jax.ShapeDtypeStruct((dec_batch, N_STATES), F32),
        jax.ShapeDtypeStruct((dec_batch, N_STATES), F32),
        jax.ShapeDtypeStruct((dec_batch, dec_seq, CM_WIDTH), F32),
    )
    out_specs = (pl.BlockSpec((TILE_ROWS, D_MODEL), lambda i: (i, 0)),) + tuple(
        _const_spec(o.shape) for o in out_shape[1:])
    scratch = [
        pltpu.VMEM((2, CHUNK, nb, D_MODEL), F32),
        pltpu.SemaphoreType.DMA((2, nb)),
        pltpu.VMEM((HALF_STATES // LANES, BLOCK_ROWS, LANES), F32),
        pltpu.VMEM((HALF_STATES // LANES, BLOCK_ROWS, LANES), F32),
        pltpu.VMEM((nb, N_STATES), F32),
        pltpu.VMEM((nb, N_STATES), F32),
        pltpu.VMEM((CM_WIDTH // LANES, BLOCK_ROWS, LANES), F32),
        pltpu.VMEM((CM_WIDTH // LANES, BLOCK_ROWS, LANES), F32),
        pltpu.VMEM((CM_WIDTH // LANES, BLOCK_ROWS, LANES), F32),
        pltpu.VMEM((TILE_ROWS, S5_WIDTH), F32),
        pltpu.VMEM((TILE_ROWS, HALF_IN), F32),
        pltpu.VMEM(w_in.shape, BF16),
        pltpu.VMEM(w_out.shape, BF16),
        pltpu.VMEM(w_glu.shape, BF16),
        pltpu.SemaphoreType.DMA(((w_in.shape[1] + w_out.shape[1] + w_glu.shape[1]) // LANES,)),
        pltpu.SemaphoreType.DMA((2,)),
    ]
    return pl.pallas_call(
        functools.partial(_mixer_kernel, n_prompt, dec_seq),
        grid=(n_prompt + 1,),
        in_specs=in_specs,
        out_specs=out_specs,
        out_shape=out_shape,
        scratch_shapes=scratch,
        compiler_params=pltpu.CompilerParams(
            dimension_semantics=("arbitrary",), vmem_limit_bytes=VMEM_LIMIT_BYTES),
        name="mixer",
    )(x_prompt, x_sample, *h0, w_in, w_out, w_glu, *small_w)


def _ffn_kernel(nb, n_prompt, x_ref, g2_ref, gf_ref, wg_hbm, wu_hbm, wd_hbm, yp_hbm, ys_hbm,
                yt_ref, out_sem, wg_ref, wu_ref, wd_ref, stage_gu_ref, stage_d_ref, w_sem):
    tt = TILE_ROWS // nb
    step = pl.program_id(0)
    n_steps = pl.num_programs(0)
    slot = step % 2
    n_chunks = D_FF // FF_CHUNK

    def prompt_writeback(s, sl):
        return [pltpu.make_async_copy(yt_ref.at[sl, :, b, :], yp_hbm.at[b, pl.ds(s * tt, tt), :],
                                      out_sem.at[sl, b]) for b in range(nb)]

    def sample_writeback(sl):
        return [pltpu.make_async_copy(yt_ref.at[sl], ys_hbm, out_sem.at[sl, 0])]

    def wait_writeback(s, sl):
        @pl.when(s < n_prompt)
        def _():
            for cp in prompt_writeback(s, sl):
                cp.wait()

        @pl.when(s >= n_prompt)
        def _():
            for cp in sample_writeback(sl):
                cp.wait()

    def weight_chunk_copies(c, sl):
        cols = pl.ds(c * FF_CHUNK, FF_CHUNK)
        return [pltpu.make_async_copy(wg_hbm.at[:, cols], stage_gu_ref.at[sl, 0], w_sem.at[sl, 0]),
                pltpu.make_async_copy(wu_hbm.at[:, cols], stage_gu_ref.at[sl, 1], w_sem.at[sl, 1]),
                pltpu.make_async_copy(wd_hbm.at[cols, :], stage_d_ref.at[sl], w_sem.at[sl, 2])]

    def stage_in_chunk(c):
        sl = c % FF_STAGE_SLOTS
        cols = slice(c * FF_CHUNK, (c + 1) * FF_CHUNK)
        for cp in weight_chunk_copies(c, sl):
            cp.wait()
        wg_ref[:, cols] = stage_gu_ref[sl, 0].astype(BF16)
        wu_ref[:, cols] = stage_gu_ref[sl, 1].astype(BF16)
        wd_ref[cols, :] = stage_d_ref[sl].astype(BF16)
        if c + FF_STAGE_SLOTS < n_chunks:
            for cp in weight_chunk_copies(c + FF_STAGE_SLOTS, sl):
                cp.start()

    @pl.when(step == 0)
    def _():
        for c in range(FF_STAGE_SLOTS):
            for cp in weight_chunk_copies(c, c):
                cp.start()
        for c in range(n_chunks):
            stage_in_chunk(c)

    @pl.when(step >= 2)
    def _():
        wait_writeback(step - 2, slot)

    halves = [slice(r, r + FFN_ROWS) for r in range(0, TILE_ROWS, FFN_ROWS)]
    hs = [_rms(x_ref[rsl, :], g2_ref[...]).astype(BF16) for rsl in halves]
    accs = [x_ref[rsl, :] for rsl in halves]
    for c in range(n_chunks):
        cols = slice(c * FF_CHUNK, (c + 1) * FF_CHUNK)
        for i in range(len(halves)):
            gate = _dot(hs[i], wg_ref[:, cols])
            up = _dot(hs[i], wu_ref[:, cols])
            accs[i] = accs[i] + _dot((jax.nn.silu(gate) * up).astype(BF16), wd_ref[cols, :])
    for i, rsl in enumerate(halves):
        groups = slice(rsl.start // nb, rsl.stop // nb)
        yt_ref[slot, groups] = _rms(accs[i], gf_ref[...]).reshape(groups.stop - groups.start, nb, D_MODEL)

    @pl.when(step < n_prompt)
    def _():
        for cp in prompt_writeback(step, slot):
            cp.start()

    @pl.when(step >= n_prompt)
    def _():
        for cp in sample_writeback(slot):
            cp.start()

    @pl.when(step == n_steps - 1)
    def _():
        wait_writeback(step - 1, 1 - slot)
        wait_writeback(step, slot)


def _ffn(x1, g2, gf, ffn_w, *, nb, seq):
    n = x1.shape[0]
    n_prompt = nb * seq // TILE_ROWS
    assert n == (n_prompt + 1) * TILE_ROWS and TILE_ROWS % nb == 0 and n_prompt >= 1
    tt = TILE_ROWS // nb
    w_gate, w_up, w_down = ffn_w
    return pl.pallas_call(
        functools.partial(_ffn_kernel, nb, n_prompt),
        grid=(n_prompt + 1,),
        in_specs=[pl.BlockSpec((TILE_ROWS, D_MODEL), lambda i: (i, 0)),
                  _const_spec(g2.shape), _const_spec(gf.shape)] + [_ANY_SPEC] * 3,
        out_specs=(_ANY_SPEC, _ANY_SPEC),
        out_shape=(jax.ShapeDtypeStruct((nb, seq, D_MODEL), F32),
                   jax.ShapeDtypeStruct((tt, nb, D_MODEL), F32)),
        scratch_shapes=[pltpu.VMEM((2, tt, nb, D_MODEL), F32),
                        pltpu.SemaphoreType.DMA((2, nb)),
                        pltpu.VMEM(w_gate.shape, BF16),
                        pltpu.VMEM(w_up.shape, BF16),
                        pltpu.VMEM(w_down.shape, BF16),
                        pltpu.VMEM((FF_STAGE_SLOTS, 2, D_MODEL, FF_CHUNK), F32),
                        pltpu.VMEM((FF_STAGE_SLOTS, FF_CHUNK, D_MODEL), F32),
                        pltpu.SemaphoreType.DMA((FF_STAGE_SLOTS, 3))],
        compiler_params=pltpu.CompilerParams(
            dimension_semantics=("arbitrary",), vmem_limit_bytes=VMEM_LIMIT_BYTES),
        name="ffn",
    )(x1, g2, gf, w_gate, w_up, w_down)


def kernel(x_prompt, x_sample, state_s5_re, state_s5_im, norm1, w_in, lam_re, lam_im, log_dt, b_re, b_im, c_re, c_im, d_skip, w_glu, b_glu, cm_ln_g, cm_ln_b, w_s, b_s, g_s5, g_cm, w_out, norm2, w_gate, w_up, w_down, norm_f):
    depth = norm1.shape[0]
    assert depth == 1
    l = 0
    batch, seq, _ = x_prompt.shape
    dec_batch, dec_seq, _ = x_sample.shape

    ar, ai, bdr, bdi, cdr, cdi = pl.pallas_call(
        _prep_kernel,
        out_shape=(
            jax.ShapeDtypeStruct((1, N_STATES), F32),
            jax.ShapeDtypeStruct((1, N_STATES), F32),
            jax.ShapeDtypeStruct((S5_HALVES, HALF_IN, HALF_STATES), BF16),
            jax.ShapeDtypeStruct((S5_HALVES, HALF_IN, HALF_STATES), BF16),
            jax.ShapeDtypeStruct((S5_HALVES, HALF_STATES, HALF_IN), BF16),
            jax.ShapeDtypeStruct((S5_HALVES, HALF_STATES, HALF_IN), BF16),
        ),
        name="s5_prepare",
    )(lam_re[l], lam_im[l], log_dt[l][:, None], b_re, b_im,
      c_re[l].reshape(S5_WIDTH, S5_STATE), c_im[l].reshape(S5_WIDTH, S5_STATE))

    row = lambda a: a.reshape(1, -1)
    wmix_p = w_s[l].reshape(CM_HEADS // 2, 2 * CHUNK, CHUNK)
    bmix_p = jnp.repeat(b_s[l].T, CM_HEAD_DIM, axis=1)
    lane_blocks = lambda a: jnp.moveaxis(a.reshape(a.shape[:-1] + (CM_WIDTH // LANES, LANES)), -2, 0)
    wmix_s = lane_blocks(jnp.repeat(jnp.transpose(w_s[l][:, :dec_seq, :dec_seq], (1, 2, 0)),
                                    CM_HEAD_DIM, axis=2))
    bmix_s = lane_blocks(bmix_p[:dec_seq])
    small_w = [row(norm1[l]), bdr, bdi, cdr, cdi, ar, ai, row(d_skip[l]), row(b_glu[l]),
               row(cm_ln_g[l]), row(cm_ln_b[l]), row(g_s5[l]), row(g_cm[l]),
               wmix_p, bmix_p, wmix_s, bmix_s]
    h0 = (state_s5_re[l].reshape(dec_batch, N_STATES), state_s5_im[l].reshape(dec_batch, N_STATES))

    x1, pr, pi_, pv, sr, si, sv = _mixer(x_prompt, x_sample, h0, (w_in[l], w_out[l], w_glu[l]), small_w)
    y_prompt, y_sample = _ffn(x1, row(norm2[l]), row(norm_f), (w_gate[l], w_up[l], w_down[l]),
                              nb=batch, seq=seq)

    st = lambda a, n: a.reshape(1, n, S5_GROUPS, S5_STATE)
    return (y_prompt, y_sample.reshape(dec_batch, dec_seq, D_MODEL),
            st(pr, batch), st(pi_, batch), pv[None],
            st(sr, dec_batch), st(si, dec_batch), sv[None])
```

```python
import functools

import jax
import jax.numpy as jnp
from jax import lax
from jax.experimental import pallas as pl
from jax.experimental.pallas import tpu as pltpu

F32 = jnp.float32
BF16 = jnp.bfloat16

D_MODEL = 1024
S5_WIDTH = 512
S5_GROUP = 16
S5_GROUPS = 32
S5_STATE = 64
N_STATES = S5_GROUPS * S5_STATE
CM_WIDTH = 512
CM_HEADS = 8
CM_HEAD_DIM = 64
CHUNK = 128
IN_WIDTH = S5_WIDTH + 2 * CM_WIDTH
D_FF = 2816
EPS = 1e-6

S5_HALVES = 2
HALF_GROUPS = S5_GROUPS // S5_HALVES
HALF_IN = S5_WIDTH // S5_HALVES
HALF_STATES = N_STATES // S5_HALVES
SUBLANES = 8
LANES = 128
SCAN_LANES = 1024
SCAN_BLOCKS = SCAN_LANES // LANES
MXU_COLS = 256
TILE_ROWS = SUBLANES * CHUNK
BLOCK_ROWS = TILE_ROWS + SUBLANES
IN_ROWS = 256
MID_ROWS = 1024
GATE_ROWS = 1024
FF_CHUNK = 256
FFN_ROWS = 512
FF_STAGE_SLOTS = 4
VMEM_LIMIT_BYTES = 58 * 1024 * 1024


def _rms(x, g):
    r = lax.rsqrt(jnp.mean(x * x, axis=-1, keepdims=True) + EPS)
    return x * r * g


def _dot(a, b):
    return jnp.dot(a, b, preferred_element_type=F32)


def _load_lane_blocks(ref, n_blocks):
    return jnp.concatenate([ref[k, :TILE_ROWS, :] for k in range(n_blocks)], axis=1)


def _cast_weights(weights, slots, sems):
    plan = []
    for w_hbm, w_bf_ref in weights:
        n_rows, n_cols = w_hbm.shape
        for k in range(n_cols // LANES):
            i = len(plan)
            stage = slots[i].at[pl.ds(0, n_rows), :]
            copy = pltpu.make_async_copy(w_hbm.at[:, pl.ds(k * LANES, LANES)], stage, sems.at[i])
            plan.append((copy, stage, w_bf_ref, k))
    for copy, _, _, _ in plan:
        copy.start()
    for copy, stage, w_bf_ref, k in plan:
        copy.wait()
        w_bf_ref[:, k * LANES:(k + 1) * LANES] = stage[...].astype(BF16)


def _prep_kernel(lr_ref, li_ref, ldt_ref, br_ref, bi_ref, cr_ref, ci_ref,
                 ar_ref, ai_ref, bdr_ref, bdi_ref, cdr_ref, cdi_ref):
    dt = jnp.exp(ldt_ref[...])
    lr = lr_ref[...]
    li = li_ref[...]
    mag = jnp.exp(lr * dt)
    ar = mag * jnp.cos(li * dt)
    ai = mag * jnp.sin(li * dt)
    den = lr * lr + li * li
    qr = ((ar - 1.0) * lr + ai * li) / den
    qi = (ai * lr - (ar - 1.0) * li) / den
    for g in range(S5_GROUPS):
        ar_ref[:, g * S5_STATE:(g + 1) * S5_STATE] = ar[g:g + 1, :]
        ai_ref[:, g * S5_STATE:(g + 1) * S5_STATE] = ai[g:g + 1, :]

    def iota(shape, dim):
        return lax.broadcasted_iota(jnp.int32, shape, dim)

    rep = (iota((S5_STATE, HALF_STATES), 1) % S5_STATE == iota((S5_STATE, HALF_STATES), 0)).astype(BF16)

    def repeat_lanes(x, rep):
        hi = x.astype(BF16)
        rest = x - hi.astype(F32)
        mid = rest.astype(BF16)
        lo = (rest - mid.astype(F32)).astype(BF16)
        return _dot(hi, rep) + _dot(mid, rep) + _dot(lo, rep)

    diag = iota((HALF_IN, HALF_STATES), 0) // S5_GROUP == iota((HALF_IN, HALF_STATES), 1) // S5_STATE
    for hv in range(S5_HALVES):
        groups = range(hv * HALF_GROUPS, (hv + 1) * HALF_GROUPS)
        bb_r = jnp.concatenate([qr[g:g + 1, :] * br_ref[0, g] - qi[g:g + 1, :] * bi_ref[0, g]
                                for g in groups], axis=0)
        bb_i = jnp.concatenate([qr[g:g + 1, :] * bi_ref[0, g] + qi[g:g + 1, :] * br_ref[0, g]
                                for g in groups], axis=0)
        for bb, o_ref in ((bb_r, bdr_ref), (bb_i, bdi_ref)):
            o_ref[hv] = jnp.where(diag, repeat_lanes(bb, rep), 0.0).astype(BF16)
        for c_ref, o_ref in ((cr_ref, cdr_ref), (ci_ref, cdi_ref)):
            c_half = c_ref[hv * HALF_IN:(hv + 1) * HALF_IN, :]
            o_ref[hv] = jnp.where(diag, repeat_lanes(c_half, rep), 0.0).T.astype(BF16)


def _cm_uv(p_cm, lng_ref, lnb_ref):
    uv = jax.nn.gelu(p_cm)
    u = uv[:, :CM_WIDTH]
    vv = uv[:, CM_WIDTH:]
    mu = jnp.mean(vv, axis=-1, keepdims=True)
    vc = vv - mu
    v = vc * lax.rsqrt(jnp.mean(vc * vc, axis=-1, keepdims=True) + EPS) * lng_ref[...] + lnb_ref[...]
    return u, v


def _scan_lanes(bur_ref, bui_ref, ar_ref, ai_ref, hf, lc, h_init, rows_at, n_steps, unrolled):
    blocks = range(lc * SCAN_BLOCKS, (lc + 1) * SCAN_BLOCKS)
    glanes = slice(hf * HALF_STATES + lc * SCAN_LANES, hf * HALF_STATES + (lc + 1) * SCAN_LANES)
    a_r = jnp.broadcast_to(ar_ref[:, glanes], (SUBLANES, SCAN_LANES))
    a_i = jnp.broadcast_to(ai_ref[:, glanes], (SUBLANES, SCAN_LANES))

    def scan_step(t, carry):
        hr, hi = carry
        trows = rows_at(t)
        b_r = jnp.concatenate([bur_ref[k, trows, :] for k in blocks], axis=1)
        b_i = jnp.concatenate([bui_ref[k, trows, :] for k in blocks], axis=1)
        nr = a_r * hr - a_i * hi + b_r
        ni = a_r * hi + a_i * hr + b_i
        for n, k in enumerate(blocks):
            bur_ref[k, trows, :] = nr[:, n * LANES:(n + 1) * LANES]
            bui_ref[k, trows, :] = ni[:, n * LANES:(n + 1) * LANES]
        return nr, ni

    if unrolled:
        carry = h_init
        for t in range(n_steps):
            carry = scan_step(t, carry)
        return carry
    return lax.fori_loop(0, n_steps, scan_step, h_init, unroll=8)


def _mixer_kernel(n_prompt, dec_seq,
                  xp_hbm, xs_hbm, h0r_hbm, h0i_hbm, win_hbm, wout_hbm, wglu_hbm,
                  g1_ref, bdr_ref, bdi_ref, cdr_ref, cdi_ref, ar_ref, ai_ref, dskip_ref, bglu_ref,
                  lng_ref, lnb_ref, gs5_ref, gcm_ref, wmixp_ref, bmixp_ref, wmixs_ref, bmixs_ref,
                  x1_ref, pr_ref, pi_ref, pv_ref, srt_ref, sit_ref, sv_ref,
                  xt_ref, in_sem, bur_ref, bui_ref, hsr_ref, hsi_ref, sr_ref, si_ref, cma_ref, cmb_ref, cmc_ref,
                  xs_ref, y0_ref, win_ref, wout_ref, wglu_ref, w_sem, h0_sem):
    nb = SUBLANES
    rows = TILE_ROWS
    step = pl.program_id(0)
    slot = step % 2

    def fetch_prompt(s, sl):
        return [pltpu.make_async_copy(xp_hbm.at[b, pl.ds(s * CHUNK, CHUNK), :], xt_ref.at[sl, :, b, :],
                                      in_sem.at[sl, b]) for b in range(nb)]

    def fetch_sample(sl):
        return pltpu.make_async_copy(xs_hbm, xt_ref.at[sl], in_sem.at[sl, 0])

    @pl.when(step == 0)
    def _():
        for cp in fetch_prompt(0, 0):
            cp.start()
        staging = [ref.at[k] for ref in (bur_ref, bui_ref, cma_ref, cmb_ref, cmc_ref)
                   for k in range(ref.shape[0])]
        _cast_weights([(win_hbm, win_ref), (wout_hbm, wout_ref), (wglu_hbm, wglu_ref)], staging, w_sem)
        h0_copies = [pltpu.make_async_copy(h0r_hbm, srt_ref, h0_sem.at[0]),
                     pltpu.make_async_copy(h0i_hbm, sit_ref, h0_sem.at[1])]
        for cp in h0_copies:
            cp.start()
        for cp in h0_copies:
            cp.wait()
        sr_ref[...] = srt_ref[...].T
        si_ref[...] = sit_ref[...].T
        hsr_ref[...] = jnp.zeros_like(hsr_ref)
        hsi_ref[...] = jnp.zeros_like(hsi_ref)

    @pl.when(step + 1 < n_prompt)
    def _():
        for cp in fetch_prompt(step + 1, 1 - slot):
            cp.start()

    @pl.when(step + 1 == n_prompt)
    def _():
        fetch_sample(1 - slot).start()

    @pl.when(step < n_prompt)
    def _():
        for cp in fetch_prompt(step, slot):
            cp.wait()

    @pl.when(step == n_prompt)
    def _():
        fetch_sample(slot).wait()

    is_prompt = step < n_prompt
    is_sample = step == n_prompt
    cm_blocks = CM_WIDTH // LANES
    mxu_blocks = MXU_COLS // LANES
    us_ref, vs_ref, mx_ref = cma_ref, cmb_ref, cmc_ref
    def row_blocks(n):
        return [slice(r * n, (r + 1) * n) for r in range(rows // n)]

    def x_rows(rsl):
        groups = slice(rsl.start // nb, rsl.stop // nb)
        return xt_ref[slot, groups].reshape(rsl.stop - rsl.start, D_MODEL)

    def b_proj(hf, xs_h, rsl):
        for n in range(HALF_STATES // MXU_COLS):
            cols = slice(n * MXU_COLS, (n + 1) * MXU_COLS)
            b_r = _dot(xs_h, bdr_ref[hf, :, cols])
            b_i = _dot(xs_h, bdi_ref[hf, :, cols])
            for q in range(mxu_blocks):
                bur_ref[n * mxu_blocks + q, rsl, :] = b_r[:, q * LANES:(q + 1) * LANES]
                bui_ref[n * mxu_blocks + q, rsl, :] = b_i[:, q * LANES:(q + 1) * LANES]

    def c_proj(hf, rsl):
        y_h = None
        for n in range(HALF_STATES // MXU_COLS):
            krows = slice(n * MXU_COLS, (n + 1) * MXU_COLS)
            blocks = range(n * mxu_blocks, (n + 1) * mxu_blocks)
            h_r = jnp.concatenate([bur_ref[k, rsl, :] for k in blocks], axis=1).astype(BF16)
            h_i = jnp.concatenate([bui_ref[k, rsl, :] for k in blocks], axis=1).astype(BF16)
            part = _dot(h_r, cdr_ref[hf, krows, :]) - _dot(h_i, cdi_ref[hf, krows, :])
            y_h = part if y_h is None else y_h + part
        return y_h

    def scan_half(hf):
        @pl.when(is_prompt)
        def _():
            for lc in range(HALF_STATES // SCAN_LANES):
                glanes = slice(hf * HALF_STATES + lc * SCAN_LANES, hf * HALF_STATES + (lc + 1) * SCAN_LANES)
                hr, hi = _scan_lanes(bur_ref, bui_ref, ar_ref, ai_ref, hf, lc,
                                     (hsr_ref[:, glanes], hsi_ref[:, glanes]),
                                     lambda t: pl.ds(pl.multiple_of(t * nb, nb), nb), CHUNK, False)
                hsr_ref[:, glanes] = hr
                hsi_ref[:, glanes] = hi
                pr_ref[:, glanes] = hr
                pi_ref[:, glanes] = hi

        @pl.when(is_sample)
        def _():
            tt = dec_seq

            def scan_group(sg, carry):
                srows = pl.ds(pl.multiple_of(sg * SUBLANES, SUBLANES), SUBLANES)
                for lc in range(HALF_STATES // SCAN_LANES):
                    glanes = slice(hf * HALF_STATES + lc * SCAN_LANES,
                                   hf * HALF_STATES + (lc + 1) * SCAN_LANES)
                    hr, hi = _scan_lanes(bur_ref, bui_ref, ar_ref, ai_ref, hf, lc,
                                         (sr_ref[srows, glanes], si_ref[srows, glanes]),
                                         lambda t: pl.ds(sg * (SUBLANES * tt) + t, SUBLANES, stride=tt),
                                         tt, True)
                    sr_ref[srows, glanes] = hr
                    si_ref[srows, glanes] = hi
                return carry

            lax.fori_loop(0, rows // tt // SUBLANES, scan_group, 0)
            if hf == S5_HALVES - 1:
                srt_ref[...] = sr_ref[...].T
                sit_ref[...] = si_ref[...].T

    for rsl in row_blocks(IN_ROWS):
        h = _rms(x_rows(rsl), g1_ref[...]).astype(BF16)
        u, v = _cm_uv(_dot(h, win_ref[:, S5_WIDTH:]), lng_ref, lnb_ref)
        for k in range(cm_blocks):
            us_ref[k, rsl, :] = u[:, k * LANES:(k + 1) * LANES]
            vs_ref[k, rsl, :] = v[:, k * LANES:(k + 1) * LANES]
        xs = _dot(h, win_ref[:, :S5_WIDTH])
        xs_ref[rsl, :] = xs
        b_proj(0, xs[:, :HALF_IN].astype(BF16), rsl)

    r_idx = lax.broadcasted_iota(jnp.int32, (2 * CHUNK, CHUNK), 0)
    c_idx = lax.broadcasted_iota(jnp.int32, (2 * CHUNK, CHUNK), 1)
    causal = (r_idx % CHUNK) >= c_idx
    first_head = lax.broadcasted_iota(jnp.int32, (CHUNK, LANES), 1) < CM_HEAD_DIM
    for j in range(CM_HEADS // 2):
        w_pair = jnp.where(causal, wmixp_ref[j], 0.0).astype(BF16)
        bias = bmixp_ref[:, j * LANES:(j + 1) * LANES]
        for b in range(nb):
            seq_rows = pl.ds(b, CHUNK, stride=nb)
            r = _dot(w_pair, vs_ref[j, seq_rows, :].astype(BF16))
            mixed = jnp.where(first_head, r[:CHUNK], r[CHUNK:]) + bias
            mx_ref[j, seq_rows, :] = us_ref[j, seq_rows, :] * mixed

    @pl.when(step == n_prompt - 1)
    def _():
        for j in range(cm_blocks):
            for b in range(nb):
                pv_ref[b, :, j * LANES:(j + 1) * LANES] = vs_ref[j, pl.ds(b, CHUNK, stride=nb), :]

    @pl.when(is_sample)
    def _():
        tt = dec_seq
        n_seq = rows // tt
        sv_ref[...] = _load_lane_blocks(vs_ref, cm_blocks).reshape(n_seq, tt, CM_WIDTH)
        def mix_block(k, carry):
            v_at = [vs_ref[k, pl.ds(s, n_seq, stride=tt), :] for s in range(tt)]
            for t in range(tt):
                acc = bmixs_ref[k, t:t + 1, :] + wmixs_ref[k, t, 0:1, :] * v_at[0]
                for s in range(1, t + 1):
                    acc = acc + wmixs_ref[k, t, s:s + 1, :] * v_at[s]
                trows = pl.ds(t, n_seq, stride=tt)
                mx_ref[k, trows, :] = us_ref[k, trows, :] * acc
            return carry

        lax.fori_loop(0, cm_blocks, mix_block, 0)

    scan_half(0)
    for rsl in row_blocks(MID_ROWS):
        y0_ref[rsl, :] = c_proj(0, rsl)
        b_proj(1, xs_ref[rsl, HALF_IN:].astype(BF16), rsl)
        o_cm = jnp.concatenate([mx_ref[k, rsl, :] for k in range(cm_blocks)], axis=1)
        n_cm = _rms(o_cm, gcm_ref[...]).astype(BF16)
        x1_ref[rsl, :] = x_rows(rsl) + _dot(n_cm, wout_ref[S5_WIDTH:, :])
    scan_half(1)
    for rsl in row_blocks(GATE_ROWS):
        y = jnp.concatenate([y0_ref[rsl, :], c_proj(1, rsl)], axis=1) + dskip_ref[...] * xs_ref[rsl, :]
        g = jax.nn.gelu(y)
        o_s5 = g * jax.nn.sigmoid(_dot(g.astype(BF16), wglu_ref[...]) + bglu_ref[...])
        n_s5 = _rms(o_s5, gs5_ref[...]).astype(BF16)
        x1_ref[rsl, :] = x1_ref[rsl, :] + _dot(n_s5, wout_ref[:S5_WIDTH, :])


def _const_spec(shape):
    nd = len(shape)
    return pl.BlockSpec(shape, lambda i, nd=nd: (0,) * nd, pipeline_mode=pl.Buffered(1))


_ANY_SPEC = pl.BlockSpec(memory_space=pl.ANY)


def _mixer(x_prompt, x_sample, h0, big_w, small_w):
    nb, seq, _ = x_prompt.shape
    dec_batch, dec_seq, _ = x_sample.shape
    assert nb == SUBLANES and seq % CHUNK == 0
    assert dec_batch * dec_seq == TILE_ROWS and dec_seq <= SUBLANES and dec_batch == CHUNK
    n_prompt = seq // CHUNK
    w_in, w_out, w_glu = big_w
    in_specs = [_ANY_SPEC] * 7 + [_const_spec(a.shape) for a in small_w]
    out_shape = (
        jax.ShapeDtypeStruct(((n_prompt + 1) * TILE_ROWS, D_MODEL), F32),
        jax.ShapeDtypeStruct((nb, N_STATES), F32),
        jax.ShapeDtypeStruct((nb, N_STATES), F32),
        jax.ShapeDtypeStruct((nb, CHUNK, CM_WIDTH), F32),
        jax.ShapeDtypeStruct((N_STATES, dec_batch), F32),
        jax.ShapeDtypeStruct((N_STATES, dec_batch), F32),
        jax.ShapeDtypeStruct((dec_batch, dec_seq, CM_WIDTH), F32),
    )
    out_specs = (pl.BlockSpec((TILE_ROWS, D_MODEL), lambda i: (i, 0)),) + tuple(
        _const_spec(o.shape) for o in out_shape[1:])
    scratch = [
        pltpu.VMEM((2, CHUNK, nb, D_MODEL), F32),
        pltpu.SemaphoreType.DMA((2, nb)),
        pltpu.VMEM((HALF_STATES // LANES, BLOCK_ROWS, LANES), F32),
        pltpu.VMEM((HALF_STATES // LANES, BLOCK_ROWS, LANES), F32),
        pltpu.VMEM((nb, N_STATES), F32),
        pltpu.VMEM((nb, N_STATES), F32),
        pltpu.VMEM((dec_batch, N_STATES), F32),
        pltpu.VMEM((dec_batch, N_STATES), F32),
        pltpu.VMEM((CM_WIDTH // LANES, BLOCK_ROWS, LANES), F32),
        pltpu.VMEM((CM_WIDTH // LANES, BLOCK_ROWS, LANES), F32),
        pltpu.VMEM((CM_WIDTH // LANES, BLOCK_ROWS, LANES), F32),
        pltpu.VMEM((TILE_ROWS, S5_WIDTH), F32),
        pltpu.VMEM((TILE_ROWS, HALF_IN), F32),
        pltpu.VMEM(w_in.shape, BF16),
        pltpu.VMEM(w_out.shape, BF16),
        pltpu.VMEM(w_glu.shape, BF16),
        pltpu.SemaphoreType.DMA(((w_in.shape[1] + w_out.shape[1] + w_glu.shape[1]) // LANES,)),
        pltpu.SemaphoreType.DMA((2,)),
    ]
    return pl.pallas_call(
        functools.partial(_mixer_kernel, n_prompt, dec_seq),
        grid=(n_prompt + 1,),
        in_specs=in_specs,
        out_specs=out_specs,
        out_shape=out_shape,
        scratch_shapes=scratch,
        compiler_params=pltpu.CompilerParams(
            dimension_semantics=("arbitrary",), vmem_limit_bytes=VMEM_LIMIT_BYTES),
        name="mixer",
    )(x_prompt, x_sample, *h0, w_in, w_out, w_glu, *small_w)


def _ffn_kernel(nb, n_prompt, x_ref, g2_ref, gf_ref, wg_hbm, wu_hbm, wd_hbm, yp_hbm, ys_hbm,
                yt_ref, out_sem, wg_ref, wu_ref, wd_ref, stage_gu_ref, stage_d_ref, w_sem):
    tt = TILE_ROWS // nb
    step = pl.program_id(0)
    n_steps = pl.num_programs(0)
    slot = step % 2
    n_chunks = D_FF // FF_CHUNK

    def prompt_writeback(s, sl):
        return [pltpu.make_async_copy(yt_ref.at[sl, :, b, :], yp_hbm.at[b, pl.ds(s * tt, tt), :],
                                      out_sem.at[sl, b]) for b in range(nb)]

    def sample_writeback(sl):
        return [pltpu.make_async_copy(yt_ref.at[sl], ys_hbm, out_sem.at[sl, 0])]

    def wait_writeback(s, sl):
        @pl.when(s < n_prompt)
        def _():
            for cp in prompt_writeback(s, sl):
                cp.wait()

        @pl.when(s >= n_prompt)
        def _():
            for cp in sample_writeback(sl):
                cp.wait()

    def weight_chunk_copies(c, sl):
        cols = pl.ds(c * FF_CHUNK, FF_CHUNK)
        return [pltpu.make_async_copy(wg_hbm.at[:, cols], stage_gu_ref.at[sl, 0], w_sem.at[sl, 0]),
                pltpu.make_async_copy(wu_hbm.at[:, cols], stage_gu_ref.at[sl, 1], w_sem.at[sl, 1]),
                pltpu.make_async_copy(wd_hbm.at[cols, :], stage_d_ref.at[sl], w_sem.at[sl, 2])]

    def stage_in_chunk(c):
        sl = c % FF_STAGE_SLOTS
        cols = slice(c * FF_CHUNK, (c + 1) * FF_CHUNK)
        for cp in weight_chunk_copies(c, sl):
            cp.wait()
        wg_ref[:, cols] = stage_gu_ref[sl, 0].astype(BF16)
        wu_ref[:, cols] = stage_gu_ref[sl, 1].astype(BF16)
        wd_ref[cols, :] = stage_d_ref[sl].astype(BF16)
        if c + FF_STAGE_SLOTS < n_chunks:
            for cp in weight_chunk_copies(c + FF_STAGE_SLOTS, sl):
                cp.start()

    @pl.when(step == 0)
    def _():
        for c in range(FF_STAGE_SLOTS):
            for cp in weight_chunk_copies(c, c):
                cp.start()
        for c in range(n_chunks):
            stage_in_chunk(c)

    @pl.when(step >= 2)
    def _():
        wait_writeback(step - 2, slot)

    halves = [slice(r, r + FFN_ROWS) for r in range(0, TILE_ROWS, FFN_ROWS)]
    hs = [_rms(x_ref[rsl, :], g2_ref[...]).astype(BF16) for rsl in halves]
    accs = [x_ref[rsl, :] for rsl in halves]
    for c in range(n_chunks):
        cols = slice(c * FF_CHUNK, (c + 1) * FF_CHUNK)
        for i in range(len(halves)):
            gate = _dot(hs[i], wg_ref[:, cols])
            up = _dot(hs[i], wu_ref[:, cols])
            accs[i] = accs[i] + _dot((jax.nn.silu(gate) * up).astype(BF16), wd_ref[cols, :])
    for i, rsl in enumerate(halves):
        groups = slice(rsl.start // nb, rsl.stop // nb)
        yt_ref[slot, groups] = _rms(accs[i], gf_ref[...]).reshape(groups.stop - groups.start, nb, D_MODEL)

    @pl.when(step < n_prompt)
    def _():
        for cp in prompt_writeback(step, slot):
            cp.start()

    @pl.when(step >= n_prompt)
    def _():
        for cp in sample_writeback(slot):
            cp.start()

    @pl.when(step == n_steps - 1)
    def _():
        wait_writeback(step - 1, 1 - slot)
        wait_writeback(step, slot)


def _ffn(x1, g2, gf, ffn_w, *, nb, seq):
    n = x1.shape[0]
    n_prompt = nb * seq // TILE_ROWS
    assert n == (n_prompt + 1) * TILE_ROWS and TILE_ROWS % nb == 0 and n_prompt >= 1
    tt = TILE_ROWS // nb
    w_gate, w_up, w_down = ffn_w
    return pl.pallas_call(
        functools.partial(_ffn_kernel, nb, n_prompt),
        grid=(n_prompt + 1,),
        in_specs=[pl.BlockSpec((TILE_ROWS, D_MODEL), lambda i: (i, 0)),
                  _const_spec(g2.shape), _const_spec(gf.shape)] + [_ANY_SPEC] * 3,
        out_specs=(_ANY_SPEC, _ANY_SPEC),
        out_shape=(jax.ShapeDtypeStruct((nb, seq, D_MODEL), F32),
                   jax.ShapeDtypeStruct((tt, nb, D_MODEL), F32)),
        scratch_shapes=[pltpu.VMEM((2, tt, nb, D_MODEL), F32),
                        pltpu.SemaphoreType.DMA((2, nb)),
                        pltpu.VMEM(w_gate.shape, BF16),
                        pltpu.VMEM(w_up.shape, BF16),
                        pltpu.VMEM(w_down.shape, BF16),
                        pltpu.VMEM((FF_STAGE_SLOTS, 2, D_MODEL, FF_CHUNK), F32),
                        pltpu.VMEM((FF_STAGE_SLOTS, FF_CHUNK, D_MODEL), F32),
                        pltpu.SemaphoreType.DMA((FF_STAGE_SLOTS, 3))],
        compiler_params=pltpu.CompilerParams(
            dimension_semantics=("arbitrary",), vmem_limit_bytes=VMEM_LIMIT_BYTES),
        name="ffn",
    )(x1, g2, gf, w_gate, w_up, w_down)


def kernel(x_prompt, x_sample, state_s5_re, state_s5_im, norm1, w_in, lam_re, lam_im, log_dt, b_re, b_im, c_re, c_im, d_skip, w_glu, b_glu, cm_ln_g, cm_ln_b, w_s, b_s, g_s5, g_cm, w_out, norm2, w_gate, w_up, w_down, norm_f):
    depth = norm1.shape[0]
    assert depth == 1
    l = 0
    batch, seq, _ = x_prompt.shape
    dec_batch, dec_seq, _ = x_sample.shape

    ar, ai, bdr, bdi, cdr, cdi = pl.pallas_call(
        _prep_kernel,
        out_shape=(
            jax.ShapeDtypeStruct((1, N_STATES), F32),
            jax.ShapeDtypeStruct((1, N_STATES), F32),
            jax.ShapeDtypeStruct((S5_HALVES, HALF_IN, HALF_STATES), BF16),
            jax.ShapeDtypeStruct((S5_HALVES, HALF_IN, HALF_STATES), BF16),
            jax.ShapeDtypeStruct((S5_HALVES, HALF_STATES, HALF_IN), BF16),
            jax.ShapeDtypeStruct((S5_HALVES, HALF_STATES, HALF_IN), BF16),
        ),
        name="s5_prepare",
    )(lam_re[l], lam_im[l], log_dt[l][:, None], jnp.swapaxes(b_re, 2, 3), jnp.swapaxes(b_im, 2, 3),
      c_re[l].reshape(S5_WIDTH, S5_STATE), c_im[l].reshape(S5_WIDTH, S5_STATE))

    row = lambda a: a.reshape(1, -1)
    wmix_p = w_s[l].reshape(CM_HEADS // 2, 2 * CHUNK, CHUNK)
    bmix_p = jnp.repeat(b_s[l].T, CM_HEAD_DIM, axis=1)
    lane_blocks = lambda a: jnp.moveaxis(a.reshape(a.shape[:-1] + (CM_WIDTH // LANES, LANES)), -2, 0)
    wmix_s = lane_blocks(jnp.repeat(jnp.transpose(w_s[l][:, :dec_seq, :dec_seq], (1, 2, 0)),
                                    CM_HEAD_DIM, axis=2))
    bmix_s = lane_blocks(bmix_p[:dec_seq])
    small_w = [row(norm1[l]), bdr, bdi, cdr, cdi, ar, ai, row(d_skip[l]), row(b_glu[l]),
               row(cm_ln_g[l]), row(cm_ln_b[l]), row(g_s5[l]), row(g_cm[l]),
               wmix_p, bmix_p, wmix_s, bmix_s]
    to_state_major = lambda a: jnp.transpose(a, (1, 2, 0)).reshape(N_STATES, dec_batch)
    from_state_major = lambda a: jnp.transpose(a.reshape(S5_GROUPS, S5_STATE, dec_batch), (2, 0, 1))[None]
    h0 = (to_state_major(state_s5_re[l]), to_state_major(state_s5_im[l]))

    x1, pr, pi_, pv, sr, si, sv = _mixer(x_prompt, x_sample, h0, (w_in[l], w_out[l], w_glu[l]), small_w)
    y_prompt, y_sample = _ffn(x1, row(norm2[l]), row(norm_f), (w_gate[l], w_up[l], w_down[l]),
                              nb=batch, seq=seq)

    st = lambda a, n: a.reshape(1, n, S5_GROUPS, S5_STATE)
    return (y_prompt, y_sample.reshape(dec_batch, dec_seq, D_MODEL),
            st(pr, batch), st(pi_, batch), pv[None],
            from_state_major(sr), from_state_major(si), sv[None])
```

```python
import functools

import jax
import jax.numpy as jnp
from jax import lax
from jax.experimental import pallas as pl
from jax.experimental.pallas import tpu as pltpu

F32 = jnp.float32
BF16 = jnp.bfloat16

D_MODEL = 1024
S5_WIDTH = 512
S5_GROUP = 16
S5_GROUPS = 32
S5_STATE = 64
N_STATES = S5_GROUPS * S5_STATE
CM_WIDTH = 512
CM_HEADS = 8
CM_HEAD_DIM = 64
CHUNK = 128
D_FF = 2816
EPS = 1e-6

S5_HALVES = 2
HALF_GROUPS = S5_GROUPS // S5_HALVES
HALF_IN = S5_WIDTH // S5_HALVES
HALF_STATES = N_STATES // S5_HALVES
SUBLANES = 8
LANES = 128
SCAN_LANES = 1024
SCAN_BLOCKS = SCAN_LANES // LANES
MXU_COLS = 256
TILE_ROWS = SUBLANES * CHUNK
BLOCK_ROWS = TILE_ROWS + SUBLANES
IN_ROWS = 256
MID_ROWS = 1024
GATE_ROWS = 1024
FF_CHUNK = 256
FFN_ROWS = 512
FF_STAGE_SLOTS = 4
VMEM_LIMIT_BYTES = 58 * 1024 * 1024


def _rms(x, g):
    r = lax.rsqrt(jnp.mean(x * x, axis=-1, keepdims=True) + EPS)
    return x * r * g


def _dot(a, b):
    return jnp.dot(a, b, preferred_element_type=F32)


def _load_lane_blocks(ref, n_blocks):
    return jnp.concatenate([ref[k, :TILE_ROWS, :] for k in range(n_blocks)], axis=1)


def _cast_weights(weights, slots, sems):
    plan = []
    for w_hbm, w_bf_ref in weights:
        n_rows, n_cols = w_hbm.shape
        for k in range(n_cols // LANES):
            i = len(plan)
            stage = slots[i].at[pl.ds(0, n_rows), :]
            copy = pltpu.make_async_copy(w_hbm.at[:, pl.ds(k * LANES, LANES)], stage, sems.at[i])
            plan.append((copy, stage, w_bf_ref, k))
    for copy, _, _, _ in plan:
        copy.start()
    for copy, stage, w_bf_ref, k in plan:
        copy.wait()
        w_bf_ref[:, k * LANES:(k + 1) * LANES] = stage[...].astype(BF16)


def _prep_kernel(lr_ref, li_ref, ldt_ref, br_ref, bi_ref, cr_ref, ci_ref,
                 ar_ref, ai_ref, bdr_ref, bdi_ref, cdr_ref, cdi_ref):
    dt = jnp.exp(ldt_ref[...])
    lr = lr_ref[...]
    li = li_ref[...]
    mag = jnp.exp(lr * dt)
    ar = mag * jnp.cos(li * dt)
    ai = mag * jnp.sin(li * dt)
    den = lr * lr + li * li
    qr = ((ar - 1.0) * lr + ai * li) / den
    qi = (ai * lr - (ar - 1.0) * li) / den
    for g in range(S5_GROUPS):
        ar_ref[:, g * S5_STATE:(g + 1) * S5_STATE] = ar[g:g + 1, :]
        ai_ref[:, g * S5_STATE:(g + 1) * S5_STATE] = ai[g:g + 1, :]

    def iota(shape, dim):
        return lax.broadcasted_iota(jnp.int32, shape, dim)

    rep = (iota((S5_STATE, HALF_STATES), 1) % S5_STATE == iota((S5_STATE, HALF_STATES), 0)).astype(BF16)

    def repeat_lanes(x, rep):
        hi = x.astype(BF16)
        rest = x - hi.astype(F32)
        mid = rest.astype(BF16)
        lo = (rest - mid.astype(F32)).astype(BF16)
        return _dot(hi, rep) + _dot(mid, rep) + _dot(lo, rep)

    diag = iota((HALF_IN, HALF_STATES), 0) // S5_GROUP == iota((HALF_IN, HALF_STATES), 1) // S5_STATE
    for hv in range(S5_HALVES):
        groups = range(hv * HALF_GROUPS, (hv + 1) * HALF_GROUPS)
        bb_r = jnp.concatenate([qr[g:g + 1, :] * br_ref[0, g] - qi[g:g + 1, :] * bi_ref[0, g]
                                for g in groups], axis=0)
        bb_i = jnp.concatenate([qr[g:g + 1, :] * bi_ref[0, g] + qi[g:g + 1, :] * br_ref[0, g]
                                for g in groups], axis=0)
        for bb, o_ref in ((bb_r, bdr_ref), (bb_i, bdi_ref)):
            o_ref[hv] = jnp.where(diag, repeat_lanes(bb, rep), 0.0).astype(BF16)
        for c_ref, o_ref in ((cr_ref, cdr_ref), (ci_ref, cdi_ref)):
            c_half = c_ref[hv * HALF_IN:(hv + 1) * HALF_IN, :]
            o_ref[hv] = jnp.where(diag, repeat_lanes(c_half, rep), 0.0).T.astype(BF16)


def _cm_uv(p_cm, lng_ref, lnb_ref):
    uv = jax.nn.gelu(p_cm)
    u = uv[:, :CM_WIDTH]
    vv = uv[:, CM_WIDTH:]
    mu = jnp.mean(vv, axis=-1, keepdims=True)
    vc = vv - mu
    v = vc * lax.rsqrt(jnp.mean(vc * vc, axis=-1, keepdims=True) + EPS) * lng_ref[...] + lnb_ref[...]
    return u, v


def _scan_lanes(bur_ref, bui_ref, ar_ref, ai_ref, hf, lc, h_init, rows_at, n_steps, unrolled):
    blocks = range(lc * SCAN_BLOCKS, (lc + 1) * SCAN_BLOCKS)
    glanes = slice(hf * HALF_STATES + lc * SCAN_LANES, hf * HALF_STATES + (lc + 1) * SCAN_LANES)
    a_r = jnp.broadcast_to(ar_ref[:, glanes], (SUBLANES, SCAN_LANES))
    a_i = jnp.broadcast_to(ai_ref[:, glanes], (SUBLANES, SCAN_LANES))

    def scan_step(t, carry):
        hr, hi = carry
        trows = rows_at(t)
        b_r = jnp.concatenate([bur_ref[k, trows, :] for k in blocks], axis=1)
        b_i = jnp.concatenate([bui_ref[k, trows, :] for k in blocks], axis=1)
        nr = a_r * hr - a_i * hi + b_r
        ni = a_r * hi + a_i * hr + b_i
        for n, k in enumerate(blocks):
            bur_ref[k, trows, :] = nr[:, n * LANES:(n + 1) * LANES]
            bui_ref[k, trows, :] = ni[:, n * LANES:(n + 1) * LANES]
        return nr, ni

    if unrolled:
        carry = h_init
        for t in range(n_steps):
            carry = scan_step(t, carry)
        return carry
    return lax.fori_loop(0, n_steps, scan_step, h_init, unroll=8)


def _mixer_kernel(n_prompt, dec_seq,
                  xp_hbm, xs_hbm, h0r_hbm, h0i_hbm, win_hbm, wout_hbm, wglu_hbm,
                  g1_ref, bdr_ref, bdi_ref, cdr_ref, cdi_ref, ar_ref, ai_ref, dskip_ref, bglu_ref,
                  lng_ref, lnb_ref, gs5_ref, gcm_ref, wmixp_ref, bmixp_ref, wmixs_ref, bmixs_ref,
                  x1_ref, pr_ref, pi_ref, pv_ref, srt_ref, sit_ref, sv_ref,
                  xt_ref, in_sem, bur_ref, bui_ref, hsr_ref, hsi_ref, sr_ref, si_ref, cma_ref, cmb_ref, cmc_ref,
                  xs_ref, y0_ref, win_ref, wout_ref, wglu_ref, wtril_ref, w_sem, h0_sem):
    nb = SUBLANES
    rows = TILE_ROWS
    step = pl.program_id(0)
    slot = step % 2

    def fetch_prompt(s, sl):
        return [pltpu.make_async_copy(xp_hbm.at[b, pl.ds(s * CHUNK, CHUNK), :], xt_ref.at[sl, :, b, :],
                                      in_sem.at[sl, b]) for b in range(nb)]

    def fetch_sample(sl):
        return pltpu.make_async_copy(xs_hbm, xt_ref.at[sl], in_sem.at[sl, 0])

    @pl.when(step == 0)
    def _():
        for cp in fetch_prompt(0, 0):
            cp.start()
        staging = [ref.at[k] for ref in (bur_ref, bui_ref, cma_ref, cmb_ref, cmc_ref)
                   for k in range(ref.shape[0])]
        _cast_weights([(win_hbm, win_ref), (wout_hbm, wout_ref), (wglu_hbm, wglu_ref)], staging, w_sem)
        h0_copies = [pltpu.make_async_copy(h0r_hbm, srt_ref, h0_sem.at[0]),
                     pltpu.make_async_copy(h0i_hbm, sit_ref, h0_sem.at[1])]
        for cp in h0_copies:
            cp.start()
        for cp in h0_copies:
            cp.wait()
        sr_ref[...] = srt_ref[...].T
        si_ref[...] = sit_ref[...].T
        r_idx = lax.broadcasted_iota(jnp.int32, (2 * CHUNK, CHUNK), 0)
        c_idx = lax.broadcasted_iota(jnp.int32, (2 * CHUNK, CHUNK), 1)
        causal = (r_idx % CHUNK) >= c_idx
        for j in range(CM_HEADS // 2):
            wtril_ref[j] = jnp.where(causal, wmixp_ref[j], 0.0).astype(BF16)
        hsr_ref[...] = jnp.zeros_like(hsr_ref)
        hsi_ref[...] = jnp.zeros_like(hsi_ref)

    @pl.when(step + 1 < n_prompt)
    def _():
        for cp in fetch_prompt(step + 1, 1 - slot):
            cp.start()

    @pl.when(step + 1 == n_prompt)
    def _():
        fetch_sample(1 - slot).start()

    @pl.when(step < n_prompt)
    def _():
        for cp in fetch_prompt(step, slot):
            cp.wait()

    @pl.when(step == n_prompt)
    def _():
        fetch_sample(slot).wait()

    is_prompt = step < n_prompt
    is_sample = step == n_prompt
    cm_blocks = CM_WIDTH // LANES
    mxu_blocks = MXU_COLS // LANES
    us_ref, vs_ref, mx_ref = cma_ref, cmb_ref, cmc_ref
    def row_blocks(n):
        return [slice(r * n, (r + 1) * n) for r in range(rows // n)]

    def x_rows(rsl):
        groups = slice(rsl.start // nb, rsl.stop // nb)
        return xt_ref[slot, groups].reshape(rsl.stop - rsl.start, D_MODEL)

    def b_proj(hf, xs_h, rsl):
        for n in range(HALF_STATES // MXU_COLS):
            cols = slice(n * MXU_COLS, (n + 1) * MXU_COLS)
            b_r = _dot(xs_h, bdr_ref[hf, :, cols])
            b_i = _dot(xs_h, bdi_ref[hf, :, cols])
            for q in range(mxu_blocks):
                bur_ref[n * mxu_blocks + q, rsl, :] = b_r[:, q * LANES:(q + 1) * LANES]
                bui_ref[n * mxu_blocks + q, rsl, :] = b_i[:, q * LANES:(q + 1) * LANES]

    def c_proj(hf, rsl):
        y_h = None
        for n in range(HALF_STATES // MXU_COLS):
            krows = slice(n * MXU_COLS, (n + 1) * MXU_COLS)
            blocks = range(n * mxu_blocks, (n + 1) * mxu_blocks)
            h_r = jnp.concatenate([bur_ref[k, rsl, :] for k in blocks], axis=1).astype(BF16)
            h_i = jnp.concatenate([bui_ref[k, rsl, :] for k in blocks], axis=1).astype(BF16)
            part = _dot(h_r, cdr_ref[hf, krows, :]) - _dot(h_i, cdi_ref[hf, krows, :])
            y_h = part if y_h is None else y_h + part
        return y_h

    def scan_half(hf):
        @pl.when(is_prompt)
        def _():
            for lc in range(HALF_STATES // SCAN_LANES):
                glanes = slice(hf * HALF_STATES + lc * SCAN_LANES, hf * HALF_STATES + (lc + 1) * SCAN_LANES)
                hr, hi = _scan_lanes(bur_ref, bui_ref, ar_ref, ai_ref, hf, lc,
                                     (hsr_ref[:, glanes], hsi_ref[:, glanes]),
                                     lambda t: pl.ds(pl.multiple_of(t * nb, nb), nb), CHUNK, False)
                hsr_ref[:, glanes] = hr
                hsi_ref[:, glanes] = hi
                pr_ref[:, glanes] = hr
                pi_ref[:, glanes] = hi

        @pl.when(is_sample)
        def _():
            tt = dec_seq

            def scan_group(sg, carry):
                srows = pl.ds(pl.multiple_of(sg * SUBLANES, SUBLANES), SUBLANES)
                for lc in range(HALF_STATES // SCAN_LANES):
                    glanes = slice(hf * HALF_STATES + lc * SCAN_LANES,
                                   hf * HALF_STATES + (lc + 1) * SCAN_LANES)
                    hr, hi = _scan_lanes(bur_ref, bui_ref, ar_ref, ai_ref, hf, lc,
                                         (sr_ref[srows, glanes], si_ref[srows, glanes]),
                                         lambda t: pl.ds(sg * (SUBLANES * tt) + t, SUBLANES, stride=tt),
                                         tt, True)
                    sr_ref[srows, glanes] = hr
                    si_ref[srows, glanes] = hi
                return carry

            lax.fori_loop(0, rows // tt // SUBLANES, scan_group, 0)
            if hf == S5_HALVES - 1:
                srt_ref[...] = sr_ref[...].T
                sit_ref[...] = si_ref[...].T

    for rsl in row_blocks(IN_ROWS):
        h = _rms(x_rows(rsl), g1_ref[...]).astype(BF16)
        u, v = _cm_uv(_dot(h, win_ref[:, S5_WIDTH:]), lng_ref, lnb_ref)
        for k in range(cm_blocks):
            us_ref[k, rsl, :] = u[:, k * LANES:(k + 1) * LANES]
            vs_ref[k, rsl, :] = v[:, k * LANES:(k + 1) * LANES]
        xs = _dot(h, win_ref[:, :S5_WIDTH])
        xs_ref[rsl, :] = xs
        b_proj(0, xs[:, :HALF_IN].astype(BF16), rsl)

    first_head = lax.broadcasted_iota(jnp.int32, (CHUNK, LANES), 1) < CM_HEAD_DIM
    for j in range(CM_HEADS // 2):
        w_pair = wtril_ref[j]
        bias = bmixp_ref[:, j * LANES:(j + 1) * LANES]
        for b in range(nb):
            seq_rows = pl.ds(b, CHUNK, stride=nb)
            r = _dot(w_pair, vs_ref[j, seq_rows, :].astype(BF16))
            mixed = jnp.where(first_head, r[:CHUNK], r[CHUNK:]) + bias
            mx_ref[j, seq_rows, :] = us_ref[j, seq_rows, :] * mixed

    @pl.when(step == n_prompt - 1)
    def _():
        for j in range(cm_blocks):
            for b in range(nb):
                pv_ref[b, :, j * LANES:(j + 1) * LANES] = vs_ref[j, pl.ds(b, CHUNK, stride=nb), :]

    @pl.when(is_sample)
    def _():
        tt = dec_seq
        n_seq = rows // tt
        sv_ref[...] = _load_lane_blocks(vs_ref, cm_blocks).reshape(n_seq, tt, CM_WIDTH)
        def mix_block(k, carry):
            v_at = [vs_ref[k, pl.ds(s, n_seq, stride=tt), :] for s in range(tt)]
            for t in range(tt):
                acc = bmixs_ref[k, t:t + 1, :] + wmixs_ref[k, t, 0:1, :] * v_at[0]
                for s in range(1, t + 1):
                    acc = acc + wmixs_ref[k, t, s:s + 1, :] * v_at[s]
                trows = pl.ds(t, n_seq, stride=tt)
                mx_ref[k, trows, :] = us_ref[k, trows, :] * acc
            return carry

        lax.fori_loop(0, cm_blocks, mix_block, 0)

    scan_half(0)
    for rsl in row_blocks(MID_ROWS):
        y0_ref[rsl, :] = c_proj(0, rsl)
        b_proj(1, xs_ref[rsl, HALF_IN:].astype(BF16), rsl)
        o_cm = jnp.concatenate([mx_ref[k, rsl, :] for k in range(cm_blocks)], axis=1)
        n_cm = _rms(o_cm, gcm_ref[...]).astype(BF16)
        x1_ref[rsl, :] = x_rows(rsl) + _dot(n_cm, wout_ref[S5_WIDTH:, :])
    scan_half(1)
    for rsl in row_blocks(GATE_ROWS):
        y = jnp.concatenate([y0_ref[rsl, :], c_proj(1, rsl)], axis=1) + dskip_ref[...] * xs_ref[rsl, :]
        g = jax.nn.gelu(y)
        o_s5 = g * jax.nn.sigmoid(_dot(g.astype(BF16), wglu_ref[...]) + bglu_ref[...])
        n_s5 = _rms(o_s5, gs5_ref[...]).astype(BF16)
        x1_ref[rsl, :] = x1_ref[rsl, :] + _dot(n_s5, wout_ref[:S5_WIDTH, :])


def _const_spec(shape):
    nd = len(shape)
    return pl.BlockSpec(shape, lambda i, nd=nd: (0,) * nd, pipeline_mode=pl.Buffered(1))


_ANY_SPEC = pl.BlockSpec(memory_space=pl.ANY)


def _mixer(x_prompt, x_sample, h0, big_w, small_w):
    nb, seq, _ = x_prompt.shape
    dec_batch, dec_seq, _ = x_sample.shape
    assert nb == SUBLANES and seq % CHUNK == 0
    assert dec_batch * dec_seq == TILE_ROWS and dec_seq <= SUBLANES and dec_batch == CHUNK
    n_prompt = seq // CHUNK
    w_in, w_out, w_glu = big_w
    in_specs = [_ANY_SPEC] * 7 + [_const_spec(a.shape) for a in small_w]
    out_shape = (
        jax.ShapeDtypeStruct(((n_prompt + 1) * TILE_ROWS, D_MODEL), F32),
        jax.ShapeDtypeStruct((nb, N_STATES), F32),
        jax.ShapeDtypeStruct((nb, N_STATES), F32),
        jax.ShapeDtypeStruct((nb, CHUNK, CM_WIDTH), F32),
        jax.ShapeDtypeStruct((N_STATES, dec_batch), F32),
        jax.ShapeDtypeStruct((N_STATES, dec_batch), F32),
        jax.ShapeDtypeStruct((dec_batch, dec_seq, CM_WIDTH), F32),
    )
    out_specs = (pl.BlockSpec((TILE_ROWS, D_MODEL), lambda i: (i, 0)),) + tuple(
        _const_spec(o.shape) for o in out_shape[1:])
    scratch = [
        pltpu.VMEM((2, CHUNK, nb, D_MODEL), F32),
        pltpu.SemaphoreType.DMA((2, nb)),
        pltpu.VMEM((HALF_STATES // LANES, BLOCK_ROWS, LANES), F32),
        pltpu.VMEM((HALF_STATES // LANES, BLOCK_ROWS, LANES), F32),
        pltpu.VMEM((nb, N_STATES), F32),
        pltpu.VMEM((nb, N_STATES), F32),
        pltpu.VMEM((dec_batch, N_STATES), F32),
        pltpu.VMEM((dec_batch, N_STATES), F32),
        pltpu.VMEM((CM_WIDTH // LANES, BLOCK_ROWS, LANES), F32),
        pltpu.VMEM((CM_WIDTH // LANES, BLOCK_ROWS, LANES), F32),
        pltpu.VMEM((CM_WIDTH // LANES, BLOCK_ROWS, LANES), F32),
        pltpu.VMEM((TILE_ROWS, S5_WIDTH), F32),
        pltpu.VMEM((TILE_ROWS, HALF_IN), F32),
        pltpu.VMEM(w_in.shape, BF16),
        pltpu.VMEM(w_out.shape, BF16),
        pltpu.VMEM(w_glu.shape, BF16),
        pltpu.VMEM((CM_HEADS // 2, 2 * CHUNK, CHUNK), BF16),
        pltpu.SemaphoreType.DMA(((w_in.shape[1] + w_out.shape[1] + w_glu.shape[1]) // LANES,)),
        pltpu.SemaphoreType.DMA((2,)),
    ]
    return pl.pallas_call(
        functools.partial(_mixer_kernel, n_prompt, dec_seq),
        grid=(n_prompt + 1,),
        in_specs=in_specs,
        out_specs=out_specs,
        out_shape=out_shape,
        scratch_shapes=scratch,
        compiler_params=pltpu.CompilerParams(
            dimension_semantics=("arbitrary",), vmem_limit_bytes=VMEM_LIMIT_BYTES),
        name="mixer",
    )(x_prompt, x_sample, *h0, w_in, w_out, w_glu, *small_w)


def _ffn_kernel(nb, n_prompt, x_ref, g2_ref, gf_ref, wg_hbm, wu_hbm, wd_hbm, yp_hbm, ys_hbm,
                yt_ref, out_sem, wg_ref, wu_ref, wd_ref, stage_gu_ref, stage_d_ref, w_sem):
    tt = TILE_ROWS // nb
    step = pl.program_id(0)
    n_steps = pl.num_programs(0)
    slot = step % 2
    n_chunks = D_FF // FF_CHUNK

    def prompt_writeback(s, sl):
        return [pltpu.make_async_copy(yt_ref.at[sl, :, b, :], yp_hbm.at[b, pl.ds(s * tt, tt), :],
                                      out_sem.at[sl, b]) for b in range(nb)]

    def sample_writeback(sl):
        return [pltpu.make_async_copy(yt_ref.at[sl], ys_hbm, out_sem.at[sl, 0])]

    def wait_writeback(s, sl):
        @pl.when(s < n_prompt)
        def _():
            for cp in prompt_writeback(s, sl):
                cp.wait()

        @pl.when(s >= n_prompt)
        def _():
            for cp in sample_writeback(sl):
                cp.wait()

    def weight_chunk_copies(c, sl):
        cols = pl.ds(c * FF_CHUNK, FF_CHUNK)
        return [pltpu.make_async_copy(wg_hbm.at[:, cols], stage_gu_ref.at[sl, 0], w_sem.at[sl, 0]),
                pltpu.make_async_copy(wu_hbm.at[:, cols], stage_gu_ref.at[sl, 1], w_sem.at[sl, 1]),
                pltpu.make_async_copy(wd_hbm.at[cols, :], stage_d_ref.at[sl], w_sem.at[sl, 2])]

    def stage_in_chunk(c):
        sl = c % FF_STAGE_SLOTS
        cols = slice(c * FF_CHUNK, (c + 1) * FF_CHUNK)
        for cp in weight_chunk_copies(c, sl):
            cp.wait()
        wg_ref[:, cols] = stage_gu_ref[sl, 0].astype(BF16)
        wu_ref[:, cols] = stage_gu_ref[sl, 1].astype(BF16)
        wd_ref[cols, :] = stage_d_ref[sl].astype(BF16)
        if c + FF_STAGE_SLOTS < n_chunks:
            for cp in weight_chunk_copies(c + FF_STAGE_SLOTS, sl):
                cp.start()

    @pl.when(step == 0)
    def _():
        for c in range(FF_STAGE_SLOTS):
            for cp in weight_chunk_copies(c, c):
                cp.start()
        for c in range(n_chunks):
            stage_in_chunk(c)

    @pl.when(step >= 2)
    def _():
        wait_writeback(step - 2, slot)

    halves = [slice(r, r + FFN_ROWS) for r in range(0, TILE_ROWS, FFN_ROWS)]
    hs = [_rms(x_ref[rsl, :], g2_ref[...]).astype(BF16) for rsl in halves]
    accs = [x_ref[rsl, :] for rsl in halves]
    for c in range(n_chunks):
        cols = slice(c * FF_CHUNK, (c + 1) * FF_CHUNK)
        for i in range(len(halves)):
            gate = _dot(hs[i], wg_ref[:, cols])
            up = _dot(hs[i], wu_ref[:, cols])
            accs[i] = accs[i] + _dot((jax.nn.silu(gate) * up).astype(BF16), wd_ref[cols, :])
    for i, rsl in enumerate(halves):
        groups = slice(rsl.start // nb, rsl.stop // nb)
        yt_ref[slot, groups] = _rms(accs[i], gf_ref[...]).reshape(groups.stop - groups.start, nb, D_MODEL)

    @pl.when(step < n_prompt)
    def _():
        for cp in prompt_writeback(step, slot):
            cp.start()

    @pl.when(step >= n_prompt)
    def _():
        for cp in sample_writeback(slot):
            cp.start()

    @pl.when(step == n_steps - 1)
    def _():
        wait_writeback(step - 1, 1 - slot)
        wait_writeback(step, slot)


def _ffn(x1, g2, gf, ffn_w, *, nb, seq):
    n = x1.shape[0]
    n_prompt = nb * seq // TILE_ROWS
    assert n == (n_prompt + 1) * TILE_ROWS and TILE_ROWS % nb == 0 and n_prompt >= 1
    tt = TILE_ROWS // nb
    w_gate, w_up, w_down = ffn_w
    return pl.pallas_call(
        functools.partial(_ffn_kernel, nb, n_prompt),
        grid=(n_prompt + 1,),
        in_specs=[pl.BlockSpec((TILE_ROWS, D_MODEL), lambda i: (i, 0)),
                  _const_spec(g2.shape), _const_spec(gf.shape)] + [_ANY_SPEC] * 3,
        out_specs=(_ANY_SPEC, _ANY_SPEC),
        out_shape=(jax.ShapeDtypeStruct((nb, seq, D_MODEL), F32),
                   jax.ShapeDtypeStruct((tt, nb, D_MODEL), F32)),
        scratch_shapes=[pltpu.VMEM((2, tt, nb, D_MODEL), F32),
                        pltpu.SemaphoreType.DMA((2, nb)),
                        pltpu.VMEM(w_gate.shape, BF16),
                        pltpu.VMEM(w_up.shape, BF16),
                        pltpu.VMEM(w_down.shape, BF16),
                        pltpu.VMEM((FF_STAGE_SLOTS, 2, D_MODEL, FF_CHUNK), F32),
                        pltpu.VMEM((FF_STAGE_SLOTS, FF_CHUNK, D_MODEL), F32),
                        pltpu.SemaphoreType.DMA((FF_STAGE_SLOTS, 3))],
        compiler_params=pltpu.CompilerParams(
            dimension_semantics=("arbitrary",), vmem_limit_bytes=VMEM_LIMIT_BYTES),
        name="ffn",
    )(x1, g2, gf, w_gate, w_up, w_down)


def kernel(x_prompt, x_sample, state_s5_re, state_s5_im, norm1, w_in, lam_re, lam_im, log_dt, b_re, b_im, c_re, c_im, d_skip, w_glu, b_glu, cm_ln_g, cm_ln_b, w_s, b_s, g_s5, g_cm, w_out, norm2, w_gate, w_up, w_down, norm_f):
    depth = norm1.shape[0]
    assert depth == 1
    l = 0
    batch, seq, _ = x_prompt.shape
    dec_batch, dec_seq, _ = x_sample.shape

    ar, ai, bdr, bdi, cdr, cdi = pl.pallas_call(
        _prep_kernel,
        out_shape=(
            jax.ShapeDtypeStruct((1, N_STATES), F32),
            jax.ShapeDtypeStruct((1, N_STATES), F32),
            jax.ShapeDtypeStruct((S5_HALVES, HALF_IN, HALF_STATES), BF16),
            jax.ShapeDtypeStruct((S5_HALVES, HALF_IN, HALF_STATES), BF16),
            jax.ShapeDtypeStruct((S5_HALVES, HALF_STATES, HALF_IN), BF16),
            jax.ShapeDtypeStruct((S5_HALVES, HALF_STATES, HALF_IN), BF16),
        ),
        name="s5_prepare",
    )(lam_re[l], lam_im[l], log_dt[l][:, None], jnp.swapaxes(b_re, 2, 3), jnp.swapaxes(b_im, 2, 3),
      c_re[l].reshape(S5_WIDTH, S5_STATE), c_im[l].reshape(S5_WIDTH, S5_STATE))

    row = lambda a: a.reshape(1, -1)
    wmix_p = w_s[l].reshape(CM_HEADS // 2, 2 * CHUNK, CHUNK)
    bmix_p = jnp.repeat(b_s[l].T, CM_HEAD_DIM, axis=1)
    lane_blocks = lambda a: jnp.moveaxis(a.reshape(a.shape[:-1] + (CM_WIDTH // LANES, LANES)), -2, 0)
    wmix_s = lane_blocks(jnp.repeat(jnp.transpose(w_s[l][:, :dec_seq, :dec_seq], (1, 2, 0)),
                                    CM_HEAD_DIM, axis=2))
    bmix_s = lane_blocks(bmix_p[:dec_seq])
    small_w = [row(norm1[l]), bdr, bdi, cdr, cdi, ar, ai, row(d_skip[l]), row(b_glu[l]),
               row(cm_ln_g[l]), row(cm_ln_b[l]), row(g_s5[l]), row(g_cm[l]),
               wmix_p, bmix_p, wmix_s, bmix_s]
    to_state_major = lambda a: jnp.transpose(a, (1, 2, 0)).reshape(N_STATES, dec_batch)
    from_state_major = lambda a: jnp.transpose(a.reshape(S5_GROUPS, S5_STATE, dec_batch), (2, 0, 1))[None]
    h0 = (to_state_major(state_s5_re[l]), to_state_major(state_s5_im[l]))

    x1, pr, pi_, pv, sr, si, sv = _mixer(x_prompt, x_sample, h0, (w_in[l], w_out[l], w_glu[l]), small_w)
    y_prompt, y_sample = _ffn(x1, row(norm2[l]), row(norm_f), (w_gate[l], w_up[l], w_down[l]),
                              nb=batch, seq=seq)

    st = lambda a, n: a.reshape(1, n, S5_GROUPS, S5_STATE)
    return (y_prompt, y_sample.reshape(dec_batch, dec_seq, D_MODEL),
            st(pr, batch), st(pi_, batch), pv[None],
            from_state_major(sr), from_state_major(si), sv[None])
```

```python
import functools

import jax
import jax.numpy as jnp
from jax import lax
from jax.experimental import pallas as pl
from jax.experimental.pallas import tpu as pltpu

F32 = jnp.float32
BF16 = jnp.bfloat16

D_MODEL = 1024
S5_WIDTH = 512
S5_GROUP = 16
S5_GROUPS = 32
S5_STATE = 64
N_STATES = S5_GROUPS * S5_STATE
CM_WIDTH = 512
CM_HEADS = 8
CM_HEAD_DIM = 64
CHUNK = 128
D_FF = 2816
EPS = 1e-6

S5_HALVES = 2
HALF_GROUPS = S5_GROUPS // S5_HALVES
HALF_IN = S5_WIDTH // S5_HALVES
HALF_STATES = N_STATES // S5_HALVES
SUBLANES = 8
LANES = 128
SCAN_LANES = 1024
SCAN_BLOCKS = SCAN_LANES // LANES
MXU_COLS = 256
TILE_ROWS = SUBLANES * CHUNK
BLOCK_ROWS = TILE_ROWS + SUBLANES
IN_ROWS = 256
MID_ROWS = 1024
GATE_ROWS = 1024
FF_CHUNK = 256
FFN_ROWS = 512
FF_STAGE_SLOTS = 4
VMEM_LIMIT_BYTES = 58 * 1024 * 1024


def _rms(x, g):
    r = lax.rsqrt(jnp.mean(x * x, axis=-1, keepdims=True) + EPS)
    return x * r * g


def _dot(a, b):
    return jnp.dot(a, b, preferred_element_type=F32)


def _load_lane_blocks(ref, n_blocks):
    return jnp.concatenate([ref[k, :TILE_ROWS, :] for k in range(n_blocks)], axis=1)


def _cast_weights(weights, slots, sems):
    plan = []
    for w_hbm, w_bf_ref in weights:
        n_rows, n_cols = w_hbm.shape
        for k in range(n_cols // LANES):
            i = len(plan)
            stage = slots[i].at[pl.ds(0, n_rows), :]
            copy = pltpu.make_async_copy(w_hbm.at[:, pl.ds(k * LANES, LANES)], stage, sems.at[i])
            plan.append((copy, stage, w_bf_ref, k))
    for copy, _, _, _ in plan:
        copy.start()
    for copy, stage, w_bf_ref, k in plan:
        copy.wait()
        w_bf_ref[:, k * LANES:(k + 1) * LANES] = stage[...].astype(BF16)


def _prep_kernel(lr_ref, li_ref, ldt_ref, br_ref, bi_ref, cr_ref, ci_ref,
                 ar_ref, ai_ref, bdr_ref, bdi_ref, cdr_ref, cdi_ref):
    dt = jnp.exp(ldt_ref[...])
    lr = lr_ref[...]
    li = li_ref[...]
    mag = jnp.exp(lr * dt)
    ar = mag * jnp.cos(li * dt)
    ai = mag * jnp.sin(li * dt)
    den = lr * lr + li * li
    qr = ((ar - 1.0) * lr + ai * li) / den
    qi = (ai * lr - (ar - 1.0) * li) / den
    for g in range(S5_GROUPS):
        ar_ref[:, g * S5_STATE:(g + 1) * S5_STATE] = ar[g:g + 1, :]
        ai_ref[:, g * S5_STATE:(g + 1) * S5_STATE] = ai[g:g + 1, :]

    def iota(shape, dim):
        return lax.broadcasted_iota(jnp.int32, shape, dim)

    rep = (iota((S5_STATE, HALF_STATES), 1) % S5_STATE == iota((S5_STATE, HALF_STATES), 0)).astype(BF16)

    def repeat_lanes(x, rep):
        hi = x.astype(BF16)
        rest = x - hi.astype(F32)
        mid = rest.astype(BF16)
        lo = (rest - mid.astype(F32)).astype(BF16)
        return _dot(hi, rep) + _dot(mid, rep) + _dot(lo, rep)

    diag = iota((HALF_IN, HALF_STATES), 0) // S5_GROUP == iota((HALF_IN, HALF_STATES), 1) // S5_STATE
    for hv in range(S5_HALVES):
        groups = range(hv * HALF_GROUPS, (hv + 1) * HALF_GROUPS)
        bb_r = jnp.concatenate([qr[g:g + 1, :] * br_ref[0, g] - qi[g:g + 1, :] * bi_ref[0, g]
                                for g in groups], axis=0)
        bb_i = jnp.concatenate([qr[g:g + 1, :] * bi_ref[0, g] + qi[g:g + 1, :] * br_ref[0, g]
                                for g in groups], axis=0)
        for bb, o_ref in ((bb_r, bdr_ref), (bb_i, bdi_ref)):
            o_ref[hv] = jnp.where(diag, repeat_lanes(bb, rep), 0.0).astype(BF16)
        for c_ref, o_ref in ((cr_ref, cdr_ref), (ci_ref, cdi_ref)):
            c_half = c_ref[hv * HALF_IN:(hv + 1) * HALF_IN, :]
            o_ref[hv] = jnp.where(diag, repeat_lanes(c_half, rep), 0.0).T.astype(BF16)


def _cm_uv(p_cm, lng_ref, lnb_ref):
    uv = jax.nn.gelu(p_cm)
    u = uv[:, :CM_WIDTH]
    vv = uv[:, CM_WIDTH:]
    mu = jnp.mean(vv, axis=-1, keepdims=True)
    vc = vv - mu
    v = vc * lax.rsqrt(jnp.mean(vc * vc, axis=-1, keepdims=True) + EPS) * lng_ref[...] + lnb_ref[...]
    return u, v


def _scan_lanes(bur_ref, bui_ref, ar_ref, ai_ref, hf, lc, h_init, rows_at, n_steps, unrolled):
    blocks = range(lc * SCAN_BLOCKS, (lc + 1) * SCAN_BLOCKS)
    glanes = slice(hf * HALF_STATES + lc * SCAN_LANES, hf * HALF_STATES + (lc + 1) * SCAN_LANES)
    a_r = jnp.broadcast_to(ar_ref[:, glanes], (SUBLANES, SCAN_LANES))
    a_i = jnp.broadcast_to(ai_ref[:, glanes], (SUBLANES, SCAN_LANES))

    def scan_step(t, carry):
        hr, hi = carry
        trows = rows_at(t)
        b_r = jnp.concatenate([bur_ref[k, trows, :] for k in blocks], axis=1)
        b_i = jnp.concatenate([bui_ref[k, trows, :] for k in blocks], axis=1)
        nr = a_r * hr - a_i * hi + b_r
        ni = a_r * hi + a_i * hr + b_i
        for n, k in enumerate(blocks):
            bur_ref[k, trows, :] = nr[:, n * LANES:(n + 1) * LANES]
            bui_ref[k, trows, :] = ni[:, n * LANES:(n + 1) * LANES]
        return nr, ni

    if unrolled:
        carry = h_init
        for t in range(n_steps):
            carry = scan_step(t, carry)
        return carry
    return lax.fori_loop(0, n_steps, scan_step, h_init, unroll=8)


def _mixer_kernel(n_prompt, dec_seq,
                  xp_hbm, xs_hbm, h0r_hbm, h0i_hbm, win_hbm, wout_hbm, wglu_hbm,
                  g1_ref, bdr_ref, bdi_ref, cdr_ref, cdi_ref, ar_ref, ai_ref, dskip_ref, bglu_ref,
                  lng_ref, lnb_ref, gs5_ref, gcm_ref, wmixp_ref, cmix_ref,
                  x1_ref, pr_ref, pi_ref, pv_ref, srt_ref, sit_ref, sv_ref,
                  xt_ref, in_sem, bur_ref, bui_ref, hsr_ref, hsi_ref, sr_ref, si_ref, cma_ref, cmb_ref, cmc_ref,
                  xs_ref, y0_ref, win_ref, wout_ref, wglu_ref, wtril_ref, w_sem, h0_sem):
    nb = SUBLANES
    rows = TILE_ROWS
    step = pl.program_id(0)
    slot = step % 2

    def fetch_prompt(s, sl):
        return [pltpu.make_async_copy(xp_hbm.at[b, pl.ds(s * CHUNK, CHUNK), :], xt_ref.at[sl, :, b, :],
                                      in_sem.at[sl, b]) for b in range(nb)]

    def fetch_sample(sl):
        return pltpu.make_async_copy(xs_hbm, xt_ref.at[sl], in_sem.at[sl, 0])

    @pl.when(step == 0)
    def _():
        for cp in fetch_prompt(0, 0):
            cp.start()
        staging = [ref.at[k] for ref in (bur_ref, bui_ref, cma_ref, cmb_ref, cmc_ref)
                   for k in range(ref.shape[0])]
        _cast_weights([(win_hbm, win_ref), (wout_hbm, wout_ref), (wglu_hbm, wglu_ref)], staging, w_sem)
        h0_copies = [pltpu.make_async_copy(h0r_hbm, srt_ref, h0_sem.at[0]),
                     pltpu.make_async_copy(h0i_hbm, sit_ref, h0_sem.at[1])]
        for cp in h0_copies:
            cp.start()
        for cp in h0_copies:
            cp.wait()
        sr_ref[...] = srt_ref[...].T
        si_ref[...] = sit_ref[...].T
        r_idx = lax.broadcasted_iota(jnp.int32, (2 * CHUNK, CHUNK), 0)
        c_idx = lax.broadcasted_iota(jnp.int32, (2 * CHUNK, CHUNK), 1)
        causal = (r_idx % CHUNK) >= c_idx
        for j in range(CM_HEADS // 2):
            wtril_ref[j] = jnp.where(causal, wmixp_ref[j], 0.0).astype(BF16)
        hsr_ref[...] = jnp.zeros_like(hsr_ref)
        hsi_ref[...] = jnp.zeros_like(hsi_ref)

    @pl.when(step + 1 < n_prompt)
    def _():
        for cp in fetch_prompt(step + 1, 1 - slot):
            cp.start()

    @pl.when(step + 1 == n_prompt)
    def _():
        fetch_sample(1 - slot).start()

    @pl.when(step < n_prompt)
    def _():
        for cp in fetch_prompt(step, slot):
            cp.wait()

    @pl.when(step == n_prompt)
    def _():
        fetch_sample(slot).wait()

    is_prompt = step < n_prompt
    is_sample = step == n_prompt
    cm_blocks = CM_WIDTH // LANES
    mxu_blocks = MXU_COLS // LANES
    us_ref, vs_ref, mx_ref = cma_ref, cmb_ref, cmc_ref
    def row_blocks(n):
        return [slice(r * n, (r + 1) * n) for r in range(rows // n)]

    def x_rows(rsl):
        groups = slice(rsl.start // nb, rsl.stop // nb)
        return xt_ref[slot, groups].reshape(rsl.stop - rsl.start, D_MODEL)

    def b_proj(hf, xs_h, rsl):
        for n in range(HALF_STATES // MXU_COLS):
            cols = slice(n * MXU_COLS, (n + 1) * MXU_COLS)
            b_r = _dot(xs_h, bdr_ref[hf, :, cols])
            b_i = _dot(xs_h, bdi_ref[hf, :, cols])
            for q in range(mxu_blocks):
                bur_ref[n * mxu_blocks + q, rsl, :] = b_r[:, q * LANES:(q + 1) * LANES]
                bui_ref[n * mxu_blocks + q, rsl, :] = b_i[:, q * LANES:(q + 1) * LANES]

    def c_proj(hf, rsl):
        y_h = None
        for n in range(HALF_STATES // MXU_COLS):
            krows = slice(n * MXU_COLS, (n + 1) * MXU_COLS)
            blocks = range(n * mxu_blocks, (n + 1) * mxu_blocks)
            h_r = jnp.concatenate([bur_ref[k, rsl, :] for k in blocks], axis=1).astype(BF16)
            h_i = jnp.concatenate([bui_ref[k, rsl, :] for k in blocks], axis=1).astype(BF16)
            part = _dot(h_r, cdr_ref[hf, krows, :]) - _dot(h_i, cdi_ref[hf, krows, :])
            y_h = part if y_h is None else y_h + part
        return y_h

    def scan_half(hf):
        @pl.when(is_prompt)
        def _():
            for lc in range(HALF_STATES // SCAN_LANES):
                glanes = slice(hf * HALF_STATES + lc * SCAN_LANES, hf * HALF_STATES + (lc + 1) * SCAN_LANES)
                hr, hi = _scan_lanes(bur_ref, bui_ref, ar_ref, ai_ref, hf, lc,
                                     (hsr_ref[:, glanes], hsi_ref[:, glanes]),
                                     lambda t: pl.ds(pl.multiple_of(t * nb, nb), nb), CHUNK, False)
                hsr_ref[:, glanes] = hr
                hsi_ref[:, glanes] = hi
                pr_ref[:, glanes] = hr
                pi_ref[:, glanes] = hi

        @pl.when(is_sample)
        def _():
            tt = dec_seq

            def scan_group(sg, carry):
                srows = pl.ds(pl.multiple_of(sg * SUBLANES, SUBLANES), SUBLANES)
                for lc in range(HALF_STATES // SCAN_LANES):
                    glanes = slice(hf * HALF_STATES + lc * SCAN_LANES,
                                   hf * HALF_STATES + (lc + 1) * SCAN_LANES)
                    hr, hi = _scan_lanes(bur_ref, bui_ref, ar_ref, ai_ref, hf, lc,
                                         (sr_ref[srows, glanes], si_ref[srows, glanes]),
                                         lambda t: pl.ds(sg * (SUBLANES * tt) + t, SUBLANES, stride=tt),
                                         tt, True)
                    sr_ref[srows, glanes] = hr
                    si_ref[srows, glanes] = hi
                return carry

            lax.fori_loop(0, rows // tt // SUBLANES, scan_group, 0)
            if hf == S5_HALVES - 1:
                srt_ref[...] = sr_ref[...].T
                sit_ref[...] = si_ref[...].T

    for rsl in row_blocks(IN_ROWS):
        h = _rms(x_rows(rsl), g1_ref[...]).astype(BF16)
        u, v = _cm_uv(_dot(h, win_ref[:, S5_WIDTH:]), lng_ref, lnb_ref)
        for k in range(cm_blocks):
            us_ref[k, rsl, :] = u[:, k * LANES:(k + 1) * LANES]
            vs_ref[k, rsl, :] = v[:, k * LANES:(k + 1) * LANES]
        xs = _dot(h, win_ref[:, :S5_WIDTH])
        xs_ref[rsl, :] = xs
        b_proj(0, xs[:, :HALF_IN].astype(BF16), rsl)

    first_head = lax.broadcasted_iota(jnp.int32, (CHUNK, LANES), 1) < CM_HEAD_DIM
    for j in range(CM_HEADS // 2):
        w_pair = wtril_ref[j]
        bias = cmix_ref[j, :CHUNK, :]
        for b in range(nb):
            seq_rows = pl.ds(b, CHUNK, stride=nb)
            r = _dot(w_pair, vs_ref[j, seq_rows, :].astype(BF16))
            mixed = jnp.where(first_head, r[:CHUNK], r[CHUNK:]) + bias
            mx_ref[j, seq_rows, :] = us_ref[j, seq_rows, :] * mixed

    @pl.when(step == n_prompt - 1)
    def _():
        for j in range(cm_blocks):
            for b in range(nb):
                pv_ref[b, :, j * LANES:(j + 1) * LANES] = vs_ref[j, pl.ds(b, CHUNK, stride=nb), :]

    @pl.when(is_sample)
    def _():
        tt = dec_seq
        n_seq = rows // tt
        sv_ref[...] = _load_lane_blocks(vs_ref, cm_blocks).reshape(n_seq, tt, CM_WIDTH)
        def mix_block(k, carry):
            v_at = [vs_ref[k, pl.ds(s, n_seq, stride=tt), :] for s in range(tt)]
            for t in range(tt):
                w_row = CHUNK + t * tt
                acc = cmix_ref[k, t:t + 1, :] + cmix_ref[k, w_row:w_row + 1, :] * v_at[0]
                for s in range(1, t + 1):
                    acc = acc + cmix_ref[k, w_row + s:w_row + s + 1, :] * v_at[s]
                trows = pl.ds(t, n_seq, stride=tt)
                mx_ref[k, trows, :] = us_ref[k, trows, :] * acc
            return carry

        lax.fori_loop(0, cm_blocks, mix_block, 0)

    scan_half(0)
    for rsl in row_blocks(MID_ROWS):
        y0_ref[rsl, :] = c_proj(0, rsl)
        b_proj(1, xs_ref[rsl, HALF_IN:].astype(BF16), rsl)
        o_cm = jnp.concatenate([mx_ref[k, rsl, :] for k in range(cm_blocks)], axis=1)
        n_cm = _rms(o_cm, gcm_ref[...]).astype(BF16)
        x1_ref[rsl, :] = x_rows(rsl) + _dot(n_cm, wout_ref[S5_WIDTH:, :])
    scan_half(1)
    for rsl in row_blocks(GATE_ROWS):
        y = jnp.concatenate([y0_ref[rsl, :], c_proj(1, rsl)], axis=1) + dskip_ref[...] * xs_ref[rsl, :]
        g = jax.nn.gelu(y)
        o_s5 = g * jax.nn.sigmoid(_dot(g.astype(BF16), wglu_ref[...]) + bglu_ref[...])
        n_s5 = _rms(o_s5, gs5_ref[...]).astype(BF16)
        x1_ref[rsl, :] = x1_ref[rsl, :] + _dot(n_s5, wout_ref[:S5_WIDTH, :])


def _const_spec(shape):
    nd = len(shape)
    return pl.BlockSpec(shape, lambda i, nd=nd: (0,) * nd, pipeline_mode=pl.Buffered(1))


_ANY_SPEC = pl.BlockSpec(memory_space=pl.ANY)


def _mixer(x_prompt, x_sample, h0, big_w, small_w):
    nb, seq, _ = x_prompt.shape
    dec_batch, dec_seq, _ = x_sample.shape
    assert nb == SUBLANES and seq % CHUNK == 0
    assert dec_batch * dec_seq == TILE_ROWS and dec_seq <= SUBLANES and dec_batch == CHUNK
    n_prompt = seq // CHUNK
    w_in, w_out, w_glu = big_w
    in_specs = [_ANY_SPEC] * 7 + [_const_spec(a.shape) for a in small_w]
    out_shape = (
        jax.ShapeDtypeStruct(((n_prompt + 1) * TILE_ROWS, D_MODEL), F32),
        jax.ShapeDtypeStruct((nb, N_STATES), F32),
        jax.ShapeDtypeStruct((nb, N_STATES), F32),
        jax.ShapeDtypeStruct((nb, CHUNK, CM_WIDTH), F32),
        jax.ShapeDtypeStruct((N_STATES, dec_batch), F32),
        jax.ShapeDtypeStruct((N_STATES, dec_batch), F32),
        jax.ShapeDtypeStruct((dec_batch, dec_seq, CM_WIDTH), F32),
    )
    out_specs = (pl.BlockSpec((TILE_ROWS, D_MODEL), lambda i: (i, 0)),) + tuple(
        _const_spec(o.shape) for o in out_shape[1:])
    scratch = [
        pltpu.VMEM((2, CHUNK, nb, D_MODEL), F32),
        pltpu.SemaphoreType.DMA((2, nb)),
        pltpu.VMEM((HALF_STATES // LANES, BLOCK_ROWS, LANES), F32),
        pltpu.VMEM((HALF_STATES // LANES, BLOCK_ROWS, LANES), F32),
        pltpu.VMEM((nb, N_STATES), F32),
        pltpu.VMEM((nb, N_STATES), F32),
        pltpu.VMEM((dec_batch, N_STATES), F32),
        pltpu.VMEM((dec_batch, N_STATES), F32),
        pltpu.VMEM((CM_WIDTH // LANES, BLOCK_ROWS, LANES), F32),
        pltpu.VMEM((CM_WIDTH // LANES, BLOCK_ROWS, LANES), F32),
        pltpu.VMEM((CM_WIDTH // LANES, BLOCK_ROWS, LANES), F32),
        pltpu.VMEM((TILE_ROWS, S5_WIDTH), F32),
        pltpu.VMEM((TILE_ROWS, HALF_IN), F32),
        pltpu.VMEM(w_in.shape, BF16),
        pltpu.VMEM(w_out.shape, BF16),
        pltpu.VMEM(w_glu.shape, BF16),
        pltpu.VMEM((CM_HEADS // 2, 2 * CHUNK, CHUNK), BF16),
        pltpu.SemaphoreType.DMA(((w_in.shape[1] + w_out.shape[1] + w_glu.shape[1]) // LANES,)),
        pltpu.SemaphoreType.DMA((2,)),
    ]
    return pl.pallas_call(
        functools.partial(_mixer_kernel, n_prompt, dec_seq),
        grid=(n_prompt + 1,),
        in_specs=in_specs,
        out_specs=out_specs,
        out_shape=out_shape,
        scratch_shapes=scratch,
        compiler_params=pltpu.CompilerParams(
            dimension_semantics=("arbitrary",), vmem_limit_bytes=VMEM_LIMIT_BYTES),
        name="mixer",
    )(x_prompt, x_sample, *h0, w_in, w_out, w_glu, *small_w)


def _ffn_kernel(nb, n_prompt, x_ref, g2_ref, gf_ref, wg_hbm, wu_hbm, wd_hbm, yp_hbm, ys_hbm,
                yt_ref, out_sem, wg_ref, wu_ref, wd_ref, stage_gu_ref, stage_d_ref, w_sem):
    tt = TILE_ROWS // nb
    step = pl.program_id(0)
    n_steps = pl.num_programs(0)
    slot = step % 2
    n_chunks = D_FF // FF_CHUNK

    def prompt_writeback(s, sl):
        return [pltpu.make_async_copy(yt_ref.at[sl, :, b, :], yp_hbm.at[b, pl.ds(s * tt, tt), :],
                                      out_sem.at[sl, b]) for b in range(nb)]

    def sample_writeback(sl):
        return [pltpu.make_async_copy(yt_ref.at[sl], ys_hbm, out_sem.at[sl, 0])]

    def wait_writeback(s, sl):
        @pl.when(s < n_prompt)
        def _():
            for cp in prompt_writeback(s, sl):
                cp.wait()

        @pl.when(s >= n_prompt)
        def _():
            for cp in sample_writeback(sl):
                cp.wait()

    def weight_chunk_copies(c, sl):
        cols = pl.ds(c * FF_CHUNK, FF_CHUNK)
        return [pltpu.make_async_copy(wg_hbm.at[:, cols], stage_gu_ref.at[sl, 0], w_sem.at[sl, 0]),
                pltpu.make_async_copy(wu_hbm.at[:, cols], stage_gu_ref.at[sl, 1], w_sem.at[sl, 1]),
                pltpu.make_async_copy(wd_hbm.at[cols, :], stage_d_ref.at[sl], w_sem.at[sl, 2])]

    def stage_in_chunk(c):
        sl = c % FF_STAGE_SLOTS
        cols = slice(c * FF_CHUNK, (c + 1) * FF_CHUNK)
        for cp in weight_chunk_copies(c, sl):
            cp.wait()
        wg_ref[:, cols] = stage_gu_ref[sl, 0].astype(BF16)
        wu_ref[:, cols] = stage_gu_ref[sl, 1].astype(BF16)
        wd_ref[cols, :] = stage_d_ref[sl].astype(BF16)
        if c + FF_STAGE_SLOTS < n_chunks:
            for cp in weight_chunk_copies(c + FF_STAGE_SLOTS, sl):
                cp.start()

    @pl.when(step == 0)
    def _():
        for c in range(FF_STAGE_SLOTS):
            for cp in weight_chunk_copies(c, c):
                cp.start()
        for c in range(n_chunks):
            stage_in_chunk(c)

    @pl.when(step >= 2)
    def _():
        wait_writeback(step - 2, slot)

    halves = [slice(r, r + FFN_ROWS) for r in range(0, TILE_ROWS, FFN_ROWS)]
    hs = [_rms(x_ref[rsl, :], g2_ref[...]).astype(BF16) for rsl in halves]
    accs = [x_ref[rsl, :] for rsl in halves]
    for c in range(n_chunks):
        cols = slice(c * FF_CHUNK, (c + 1) * FF_CHUNK)
        for i in range(len(halves)):
            gate = _dot(hs[i], wg_ref[:, cols])
            up = _dot(hs[i], wu_ref[:, cols])
            accs[i] = accs[i] + _dot((jax.nn.silu(gate) * up).astype(BF16), wd_ref[cols, :])
    for i, rsl in enumerate(halves):
        groups = slice(rsl.start // nb, rsl.stop // nb)
        yt_ref[slot, groups] = _rms(accs[i], gf_ref[...]).reshape(groups.stop - groups.start, nb, D_MODEL)

    @pl.when(step < n_prompt)
    def _():
        for cp in prompt_writeback(step, slot):
            cp.start()

    @pl.when(step >= n_prompt)
    def _():
        for cp in sample_writeback(slot):
            cp.start()

    @pl.when(step == n_steps - 1)
    def _():
        wait_writeback(step - 1, 1 - slot)
        wait_writeback(step, slot)


def _ffn(x1, g2, gf, ffn_w, *, nb, seq):
    n = x1.shape[0]
    n_prompt = nb * seq // TILE_ROWS
    assert n == (n_prompt + 1) * TILE_ROWS and TILE_ROWS % nb == 0 and n_prompt >= 1
    tt = TILE_ROWS // nb
    w_gate, w_up, w_down = ffn_w
    return pl.pallas_call(
        functools.partial(_ffn_kernel, nb, n_prompt),
        grid=(n_prompt + 1,),
        in_specs=[pl.BlockSpec((TILE_ROWS, D_MODEL), lambda i: (i, 0)),
                  _const_spec(g2.shape), _const_spec(gf.shape)] + [_ANY_SPEC] * 3,
        out_specs=(_ANY_SPEC, _ANY_SPEC),
        out_shape=(jax.ShapeDtypeStruct((nb, seq, D_MODEL), F32),
                   jax.ShapeDtypeStruct((tt, nb, D_MODEL), F32)),
        scratch_shapes=[pltpu.VMEM((2, tt, nb, D_MODEL), F32),
                        pltpu.SemaphoreType.DMA((2, nb)),
                        pltpu.VMEM(w_gate.shape, BF16),
                        pltpu.VMEM(w_up.shape, BF16),
                        pltpu.VMEM(w_down.shape, BF16),
                        pltpu.VMEM((FF_STAGE_SLOTS, 2, D_MODEL, FF_CHUNK), F32),
                        pltpu.VMEM((FF_STAGE_SLOTS, FF_CHUNK, D_MODEL), F32),
                        pltpu.SemaphoreType.DMA((FF_STAGE_SLOTS, 3))],
        compiler_params=pltpu.CompilerParams(
            dimension_semantics=("arbitrary",), vmem_limit_bytes=VMEM_LIMIT_BYTES),
        name="ffn",
    )(x1, g2, gf, w_gate, w_up, w_down)


def kernel(x_prompt, x_sample, state_s5_re, state_s5_im, norm1, w_in, lam_re, lam_im, log_dt, b_re, b_im, c_re, c_im, d_skip, w_glu, b_glu, cm_ln_g, cm_ln_b, w_s, b_s, g_s5, g_cm, w_out, norm2, w_gate, w_up, w_down, norm_f):
    depth = norm1.shape[0]
    assert depth == 1
    l = 0
    batch, seq, _ = x_prompt.shape
    dec_batch, dec_seq, _ = x_sample.shape

    ar, ai, bdr, bdi, cdr, cdi = pl.pallas_call(
        _prep_kernel,
        out_shape=(
            jax.ShapeDtypeStruct((1, N_STATES), F32),
            jax.ShapeDtypeStruct((1, N_STATES), F32),
            jax.ShapeDtypeStruct((S5_HALVES, HALF_IN, HALF_STATES), BF16),
            jax.ShapeDtypeStruct((S5_HALVES, HALF_IN, HALF_STATES), BF16),
            jax.ShapeDtypeStruct((S5_HALVES, HALF_STATES, HALF_IN), BF16),
            jax.ShapeDtypeStruct((S5_HALVES, HALF_STATES, HALF_IN), BF16),
        ),
        name="s5_prepare",
    )(lam_re[l], lam_im[l], log_dt[l][:, None], jnp.swapaxes(b_re, 2, 3), jnp.swapaxes(b_im, 2, 3),
      c_re[l].reshape(S5_WIDTH, S5_STATE), c_im[l].reshape(S5_WIDTH, S5_STATE))

    row = lambda a: a.reshape(1, -1)
    wmix_p = w_s[l].reshape(CM_HEADS // 2, 2 * CHUNK, CHUNK)
    per_head = jnp.concatenate([b_s[l].T, w_s[l][:, :dec_seq, :dec_seq].reshape(CM_HEADS, -1).T], axis=0)
    cmix = jnp.repeat(per_head, CM_HEAD_DIM, axis=1)
    cmix = jnp.moveaxis(cmix.reshape(-1, CM_WIDTH // LANES, LANES), 1, 0)
    small_w = [row(norm1[l]), bdr, bdi, cdr, cdi, ar, ai, row(d_skip[l]), row(b_glu[l]),
               row(cm_ln_g[l]), row(cm_ln_b[l]), row(g_s5[l]), row(g_cm[l]), wmix_p, cmix]
    to_state_major = lambda a: jnp.transpose(a, (1, 2, 0)).reshape(N_STATES, dec_batch)
    from_state_major = lambda a: jnp.transpose(a.reshape(S5_GROUPS, S5_STATE, dec_batch), (2, 0, 1))[None]
    h0 = (to_state_major(state_s5_re[l]), to_state_major(state_s5_im[l]))

    x1, pr, pi_, pv, sr, si, sv = _mixer(x_prompt, x_sample, h0, (w_in[l], w_out[l], w_glu[l]), small_w)
    y_prompt, y_sample = _ffn(x1, row(norm2[l]), row(norm_f), (w_gate[l], w_up[l], w_down[l]),
                              nb=batch, seq=seq)

    st = lambda a, n: a.reshape(1, n, S5_GROUPS, S5_STATE)
    return (y_prompt, y_sample.reshape(dec_batch, dec_seq, D_MODEL),
            st(pr, batch), st(pi_, batch), pv[None],
            from_state_major(sr), from_state_major(si), sv[None])
```

```python
import functools

import jax
import jax.numpy as jnp
from jax import lax
from jax.experimental import pallas as pl
from jax.experimental.pallas import tpu as pltpu

F32 = jnp.float32
BF16 = jnp.bfloat16

D_MODEL = 1024
S5_WIDTH = 512
S5_GROUP = 16
S5_GROUPS = 32
S5_STATE = 64
N_STATES = S5_GROUPS * S5_STATE
CM_WIDTH = 512
CM_HEADS = 8
CM_HEAD_DIM = 64
CHUNK = 128
D_FF = 2816
EPS = 1e-6

S5_HALVES = 2
HALF_GROUPS = S5_GROUPS // S5_HALVES
HALF_IN = S5_WIDTH // S5_HALVES
HALF_STATES = N_STATES // S5_HALVES
SUBLANES = 8
LANES = 128
SCAN_LANES = 1024
SCAN_BLOCKS = SCAN_LANES // LANES
MXU_COLS = 256
TILE_ROWS = SUBLANES * CHUNK
BLOCK_ROWS = TILE_ROWS + SUBLANES
IN_ROWS = 256
MID_ROWS = 1024
GATE_ROWS = 1024
FF_CHUNK = 256
FFN_ROWS = 512
STAGE_ROWS = 128
FF_STAGE_SLOTS = 6
VMEM_LIMIT_BYTES = 58 * 1024 * 1024


def _rms(x, g):
    r = lax.rsqrt(jnp.mean(x * x, axis=-1, keepdims=True) + EPS)
    return x * r * g


def _dot(a, b):
    return jnp.dot(a, b, preferred_element_type=F32)


def _load_lane_blocks(ref, n_blocks):
    return jnp.concatenate([ref[k, :TILE_ROWS, :] for k in range(n_blocks)], axis=1)


def _cast_weights(weights, slots, sems):
    plan = []
    for w_hbm, w_bf_ref in weights:
        n_rows, n_cols = w_hbm.shape
        for k in range(n_cols // LANES):
            i = len(plan)
            stage = slots[i].at[pl.ds(0, n_rows), :]
            copy = pltpu.make_async_copy(w_hbm.at[:, pl.ds(k * LANES, LANES)], stage, sems.at[i])
            plan.append((copy, stage, w_bf_ref, k))
    for copy, _, _, _ in plan:
        copy.start()
    for copy, stage, w_bf_ref, k in plan:
        copy.wait()
        w_bf_ref[:, k * LANES:(k + 1) * LANES] = stage[...].astype(BF16)


def _prep_kernel(lr_ref, li_ref, ldt_ref, br_ref, bi_ref, cr_ref, ci_ref,
                 ar_ref, ai_ref, bdr_ref, bdi_ref, cdr_ref, cdi_ref):
    dt = jnp.exp(ldt_ref[...])
    lr = lr_ref[...]
    li = li_ref[...]
    mag = jnp.exp(lr * dt)
    ar = mag * jnp.cos(li * dt)
    ai = mag * jnp.sin(li * dt)
    den = lr * lr + li * li
    qr = ((ar - 1.0) * lr + ai * li) / den
    qi = (ai * lr - (ar - 1.0) * li) / den
    for g in range(S5_GROUPS):
        ar_ref[:, g * S5_STATE:(g + 1) * S5_STATE] = ar[g:g + 1, :]
        ai_ref[:, g * S5_STATE:(g + 1) * S5_STATE] = ai[g:g + 1, :]

    def iota(shape, dim):
        return lax.broadcasted_iota(jnp.int32, shape, dim)

    rep = (iota((S5_STATE, HALF_STATES), 1) % S5_STATE == iota((S5_STATE, HALF_STATES), 0)).astype(BF16)

    def repeat_lanes(x, rep):
        hi = x.astype(BF16)
        rest = x - hi.astype(F32)
        mid = rest.astype(BF16)
        lo = (rest - mid.astype(F32)).astype(BF16)
        return _dot(hi, rep) + _dot(mid, rep) + _dot(lo, rep)

    diag = iota((HALF_IN, HALF_STATES), 0) // S5_GROUP == iota((HALF_IN, HALF_STATES), 1) // S5_STATE
    for hv in range(S5_HALVES):
        groups = range(hv * HALF_GROUPS, (hv + 1) * HALF_GROUPS)
        bb_r = jnp.concatenate([qr[g:g + 1, :] * br_ref[0, g] - qi[g:g + 1, :] * bi_ref[0, g]
                                for g in groups], axis=0)
        bb_i = jnp.concatenate([qr[g:g + 1, :] * bi_ref[0, g] + qi[g:g + 1, :] * br_ref[0, g]
                                for g in groups], axis=0)
        for bb, o_ref in ((bb_r, bdr_ref), (bb_i, bdi_ref)):
            o_ref[hv] = jnp.where(diag, repeat_lanes(bb, rep), 0.0).astype(BF16)
        for c_ref, o_ref in ((cr_ref, cdr_ref), (ci_ref, cdi_ref)):
            c_half = c_ref[hv * HALF_IN:(hv + 1) * HALF_IN, :]
            o_ref[hv] = jnp.where(diag, repeat_lanes(c_half, rep), 0.0).T.astype(BF16)


def _cm_uv(p_cm, lng_ref, lnb_ref):
    uv = jax.nn.gelu(p_cm)
    u = uv[:, :CM_WIDTH]
    vv = uv[:, CM_WIDTH:]
    mu = jnp.mean(vv, axis=-1, keepdims=True)
    vc = vv - mu
    v = vc * lax.rsqrt(jnp.mean(vc * vc, axis=-1, keepdims=True) + EPS) * lng_ref[...] + lnb_ref[...]
    return u, v


def _scan_lanes(bur_ref, bui_ref, ar_ref, ai_ref, hf, lc, h_init, rows_at, n_steps, unrolled):
    blocks = range(lc * SCAN_BLOCKS, (lc + 1) * SCAN_BLOCKS)
    glanes = slice(hf * HALF_STATES + lc * SCAN_LANES, hf * HALF_STATES + (lc + 1) * SCAN_LANES)
    a_r = jnp.broadcast_to(ar_ref[:, glanes], (SUBLANES, SCAN_LANES))
    a_i = jnp.broadcast_to(ai_ref[:, glanes], (SUBLANES, SCAN_LANES))

    def scan_step(t, carry):
        hr, hi = carry
        trows = rows_at(t)
        b_r = jnp.concatenate([bur_ref[k, trows, :] for k in blocks], axis=1)
        b_i = jnp.concatenate([bui_ref[k, trows, :] for k in blocks], axis=1)
        nr = a_r * hr - a_i * hi + b_r
        ni = a_r * hi + a_i * hr + b_i
        for n, k in enumerate(blocks):
            bur_ref[k, trows, :] = nr[:, n * LANES:(n + 1) * LANES]
            bui_ref[k, trows, :] = ni[:, n * LANES:(n + 1) * LANES]
        return nr, ni

    if unrolled:
        carry = h_init
        for t in range(n_steps):
            carry = scan_step(t, carry)
        return carry
    return lax.fori_loop(0, n_steps, scan_step, h_init, unroll=8)


def _mixer_kernel(n_prompt, dec_seq,
                  xp_hbm, xs_hbm, h0r_hbm, h0i_hbm, win_hbm, wout_hbm, wglu_hbm,
                  g1_ref, bdr_ref, bdi_ref, cdr_ref, cdi_ref, ar_ref, ai_ref, dskip_ref, bglu_ref,
                  lng_ref, lnb_ref, gs5_ref, gcm_ref, wmixp_ref, cmix_ref,
                  x1_ref, pr_ref, pi_ref, pv_ref, srt_ref, sit_ref, sv_ref,
                  xt_ref, in_sem, bur_ref, bui_ref, hsr_ref, hsi_ref, sr_ref, si_ref, cma_ref, cmb_ref, cmc_ref,
                  xs_ref, y0_ref, win_ref, wout_ref, wglu_ref, wtril_ref, w_sem, h0_sem):
    nb = SUBLANES
    rows = TILE_ROWS
    step = pl.program_id(0)
    slot = step % 2

    def fetch_prompt(s, sl):
        return [pltpu.make_async_copy(xp_hbm.at[b, pl.ds(s * CHUNK, CHUNK), :], xt_ref.at[sl, :, b, :],
                                      in_sem.at[sl, b]) for b in range(nb)]

    def fetch_sample(sl):
        return pltpu.make_async_copy(xs_hbm, xt_ref.at[sl], in_sem.at[sl, 0])

    @pl.when(step == 0)
    def _():
        for cp in fetch_prompt(0, 0):
            cp.start()
        staging = [ref.at[k] for ref in (bur_ref, bui_ref, cma_ref, cmb_ref, cmc_ref)
                   for k in range(ref.shape[0])]
        _cast_weights([(win_hbm, win_ref), (wout_hbm, wout_ref), (wglu_hbm, wglu_ref)], staging, w_sem)
        h0_copies = [pltpu.make_async_copy(h0r_hbm, srt_ref, h0_sem.at[0]),
                     pltpu.make_async_copy(h0i_hbm, sit_ref, h0_sem.at[1])]
        for cp in h0_copies:
            cp.start()
        for cp in h0_copies:
            cp.wait()
        sr_ref[...] = srt_ref[...].T
        si_ref[...] = sit_ref[...].T
        r_idx = lax.broadcasted_iota(jnp.int32, (2 * CHUNK, CHUNK), 0)
        c_idx = lax.broadcasted_iota(jnp.int32, (2 * CHUNK, CHUNK), 1)
        causal = (r_idx % CHUNK) >= c_idx
        for j in range(CM_HEADS // 2):
            wtril_ref[j] = jnp.where(causal, wmixp_ref[j], 0.0).astype(BF16)
        hsr_ref[...] = jnp.zeros_like(hsr_ref)
        hsi_ref[...] = jnp.zeros_like(hsi_ref)

    @pl.when(step + 1 < n_prompt)
    def _():
        for cp in fetch_prompt(step + 1, 1 - slot):
            cp.start()

    @pl.when(step + 1 == n_prompt)
    def _():
        fetch_sample(1 - slot).start()

    @pl.when(step < n_prompt)
    def _():
        for cp in fetch_prompt(step, slot):
            cp.wait()

    @pl.when(step == n_prompt)
    def _():
        fetch_sample(slot).wait()

    is_prompt = step < n_prompt
    is_sample = step == n_prompt
    cm_blocks = CM_WIDTH // LANES
    mxu_blocks = MXU_COLS // LANES
    us_ref, vs_ref, mx_ref = cma_ref, cmb_ref, cmc_ref
    def row_blocks(n):
        return [slice(r * n, (r + 1) * n) for r in range(rows // n)]

    def x_rows(rsl):
        groups = slice(rsl.start // nb, rsl.stop // nb)
        return xt_ref[slot, groups].reshape(rsl.stop - rsl.start, D_MODEL)

    def b_proj(hf, xs_h, rsl):
        for n in range(HALF_STATES // MXU_COLS):
            cols = slice(n * MXU_COLS, (n + 1) * MXU_COLS)
            b_r = _dot(xs_h, bdr_ref[hf, :, cols])
            b_i = _dot(xs_h, bdi_ref[hf, :, cols])
            for q in range(mxu_blocks):
                bur_ref[n * mxu_blocks + q, rsl, :] = b_r[:, q * LANES:(q + 1) * LANES]
                bui_ref[n * mxu_blocks + q, rsl, :] = b_i[:, q * LANES:(q + 1) * LANES]

    def c_proj(hf, rsl):
        y_h = None
        for n in range(HALF_STATES // MXU_COLS):
            krows = slice(n * MXU_COLS, (n + 1) * MXU_COLS)
            blocks = range(n * mxu_blocks, (n + 1) * mxu_blocks)
            h_r = jnp.concatenate([bur_ref[k, rsl, :] for k in blocks], axis=1).astype(BF16)
            h_i = jnp.concatenate([bui_ref[k, rsl, :] for k in blocks], axis=1).astype(BF16)
            part = _dot(h_r, cdr_ref[hf, krows, :]) - _dot(h_i, cdi_ref[hf, krows, :])
            y_h = part if y_h is None else y_h + part
        return y_h

    def scan_half(hf):
        @pl.when(is_prompt)
        def _():
            for lc in range(HALF_STATES // SCAN_LANES):
                glanes = slice(hf * HALF_STATES + lc * SCAN_LANES, hf * HALF_STATES + (lc + 1) * SCAN_LANES)
                hr, hi = _scan_lanes(bur_ref, bui_ref, ar_ref, ai_ref, hf, lc,
                                     (hsr_ref[:, glanes], hsi_ref[:, glanes]),
                                     lambda t: pl.ds(pl.multiple_of(t * nb, nb), nb), CHUNK, False)
                hsr_ref[:, glanes] = hr
                hsi_ref[:, glanes] = hi
                pr_ref[:, glanes] = hr
                pi_ref[:, glanes] = hi

        @pl.when(is_sample)
        def _():
            tt = dec_seq

            def scan_group(sg, carry):
                srows = pl.ds(pl.multiple_of(sg * SUBLANES, SUBLANES), SUBLANES)
                for lc in range(HALF_STATES // SCAN_LANES):
                    glanes = slice(hf * HALF_STATES + lc * SCAN_LANES,
                                   hf * HALF_STATES + (lc + 1) * SCAN_LANES)
                    hr, hi = _scan_lanes(bur_ref, bui_ref, ar_ref, ai_ref, hf, lc,
                                         (sr_ref[srows, glanes], si_ref[srows, glanes]),
                                         lambda t: pl.ds(sg * (SUBLANES * tt) + t, SUBLANES, stride=tt),
                                         tt, True)
                    sr_ref[srows, glanes] = hr
                    si_ref[srows, glanes] = hi
                return carry

            lax.fori_loop(0, rows // tt // SUBLANES, scan_group, 0)
            if hf == S5_HALVES - 1:
                srt_ref[...] = sr_ref[...].T
                sit_ref[...] = si_ref[...].T

    for rsl in row_blocks(IN_ROWS):
        h = _rms(x_rows(rsl), g1_ref[...]).astype(BF16)
        u, v = _cm_uv(_dot(h, win_ref[:, S5_WIDTH:]), lng_ref, lnb_ref)
        for k in range(cm_blocks):
            us_ref[k, rsl, :] = u[:, k * LANES:(k + 1) * LANES]
            vs_ref[k, rsl, :] = v[:, k * LANES:(k + 1) * LANES]
        xs = _dot(h, win_ref[:, :S5_WIDTH])
        xs_ref[rsl, :] = xs
        b_proj(0, xs[:, :HALF_IN].astype(BF16), rsl)

    first_head = lax.broadcasted_iota(jnp.int32, (CHUNK, LANES), 1) < CM_HEAD_DIM
    for j in range(CM_HEADS // 2):
        w_pair = wtril_ref[j]
        bias = cmix_ref[j, :CHUNK, :]
        for b in range(nb):
            seq_rows = pl.ds(b, CHUNK, stride=nb)
            r = _dot(w_pair, vs_ref[j, seq_rows, :].astype(BF16))
            mixed = jnp.where(first_head, r[:CHUNK], r[CHUNK:]) + bias
            mx_ref[j, seq_rows, :] = us_ref[j, seq_rows, :] * mixed

    @pl.when(step == n_prompt - 1)
    def _():
        for j in range(cm_blocks):
            for b in range(nb):
                pv_ref[b, :, j * LANES:(j + 1) * LANES] = vs_ref[j, pl.ds(b, CHUNK, stride=nb), :]

    @pl.when(is_sample)
    def _():
        tt = dec_seq
        n_seq = rows // tt
        sv_ref[...] = _load_lane_blocks(vs_ref, cm_blocks).reshape(n_seq, tt, CM_WIDTH)
        def mix_block(k, carry):
            v_at = [vs_ref[k, pl.ds(s, n_seq, stride=tt), :] for s in range(tt)]
            for t in range(tt):
                w_row = CHUNK + t * tt
                acc = cmix_ref[k, t:t + 1, :] + cmix_ref[k, w_row:w_row + 1, :] * v_at[0]
                for s in range(1, t + 1):
                    acc = acc + cmix_ref[k, w_row + s:w_row + s + 1, :] * v_at[s]
                trows = pl.ds(t, n_seq, stride=tt)
                mx_ref[k, trows, :] = us_ref[k, trows, :] * acc
            return carry

        lax.fori_loop(0, cm_blocks, mix_block, 0)

    scan_half(0)
    for rsl in row_blocks(MID_ROWS):
        y0_ref[rsl, :] = c_proj(0, rsl)
        b_proj(1, xs_ref[rsl, HALF_IN:].astype(BF16), rsl)
        o_cm = jnp.concatenate([mx_ref[k, rsl, :] for k in range(cm_blocks)], axis=1)
        n_cm = _rms(o_cm, gcm_ref[...]).astype(BF16)
        x1_ref[rsl, :] = x_rows(rsl) + _dot(n_cm, wout_ref[S5_WIDTH:, :])
    scan_half(1)
    for rsl in row_blocks(GATE_ROWS):
        y = jnp.concatenate([y0_ref[rsl, :], c_proj(1, rsl)], axis=1) + dskip_ref[...] * xs_ref[rsl, :]
        g = jax.nn.gelu(y)
        o_s5 = g * jax.nn.sigmoid(_dot(g.astype(BF16), wglu_ref[...]) + bglu_ref[...])
        n_s5 = _rms(o_s5, gs5_ref[...]).astype(BF16)
        x1_ref[rsl, :] = x1_ref[rsl, :] + _dot(n_s5, wout_ref[:S5_WIDTH, :])


def _const_spec(shape):
    nd = len(shape)
    return pl.BlockSpec(shape, lambda i, nd=nd: (0,) * nd, pipeline_mode=pl.Buffered(1))


_ANY_SPEC = pl.BlockSpec(memory_space=pl.ANY)


def _mixer(x_prompt, x_sample, h0, big_w, small_w):
    nb, seq, _ = x_prompt.shape
    dec_batch, dec_seq, _ = x_sample.shape
    assert nb == SUBLANES and seq % CHUNK == 0
    assert dec_batch * dec_seq == TILE_ROWS and dec_seq <= SUBLANES and dec_batch == CHUNK
    n_prompt = seq // CHUNK
    w_in, w_out, w_glu = big_w
    in_specs = [_ANY_SPEC] * 7 + [_const_spec(a.shape) for a in small_w]
    out_shape = (
        jax.ShapeDtypeStruct(((n_prompt + 1) * TILE_ROWS, D_MODEL), F32),
        jax.ShapeDtypeStruct((nb, N_STATES), F32),
        jax.ShapeDtypeStruct((nb, N_STATES), F32),
        jax.ShapeDtypeStruct((nb, CHUNK, CM_WIDTH), F32),
        jax.ShapeDtypeStruct((N_STATES, dec_batch), F32),
        jax.ShapeDtypeStruct((N_STATES, dec_batch), F32),
        jax.ShapeDtypeStruct((dec_batch, dec_seq, CM_WIDTH), F32),
    )
    out_specs = (pl.BlockSpec((TILE_ROWS, D_MODEL), lambda i: (i, 0)),) + tuple(
        _const_spec(o.shape) for o in out_shape[1:])
    scratch = [
        pltpu.VMEM((2, CHUNK, nb, D_MODEL), F32),
        pltpu.SemaphoreType.DMA((2, nb)),
        pltpu.VMEM((HALF_STATES // LANES, BLOCK_ROWS, LANES), F32),
        pltpu.VMEM((HALF_STATES // LANES, BLOCK_ROWS, LANES), F32),
        pltpu.VMEM((nb, N_STATES), F32),
        pltpu.VMEM((nb, N_STATES), F32),
        pltpu.VMEM((dec_batch, N_STATES), F32),
        pltpu.VMEM((dec_batch, N_STATES), F32),
        pltpu.VMEM((CM_WIDTH // LANES, BLOCK_ROWS, LANES), F32),
        pltpu.VMEM((CM_WIDTH // LANES, BLOCK_ROWS, LANES), F32),
        pltpu.VMEM((CM_WIDTH // LANES, BLOCK_ROWS, LANES), F32),
        pltpu.VMEM((TILE_ROWS, S5_WIDTH), F32),
        pltpu.VMEM((TILE_ROWS, HALF_IN), F32),
        pltpu.VMEM(w_in.shape, BF16),
        pltpu.VMEM(w_out.shape, BF16),
        pltpu.VMEM(w_glu.shape, BF16),
        pltpu.VMEM((CM_HEADS // 2, 2 * CHUNK, CHUNK), BF16),
        pltpu.SemaphoreType.DMA(((w_in.shape[1] + w_out.shape[1] + w_glu.shape[1]) // LANES,)),
        pltpu.SemaphoreType.DMA((2,)),
    ]
    return pl.pallas_call(
        functools.partial(_mixer_kernel, n_prompt, dec_seq),
        grid=(n_prompt + 1,),
        in_specs=in_specs,
        out_specs=out_specs,
        out_shape=out_shape,
        scratch_shapes=scratch,
        compiler_params=pltpu.CompilerParams(
            dimension_semantics=("arbitrary",), vmem_limit_bytes=VMEM_LIMIT_BYTES),
        name="mixer",
    )(x_prompt, x_sample, *h0, w_in, w_out, w_glu, *small_w)


def _ffn_kernel(nb, n_prompt, x_ref, g2_ref, gf_ref, wg_hbm, wu_hbm, wd_hbm, yp_hbm, ys_hbm,
                yt_ref, out_sem, wg_ref, wu_ref, wd_ref, stage_ref, w_sem):
    tt = TILE_ROWS // nb
    step = pl.program_id(0)
    n_steps = pl.num_programs(0)
    slot = step % 2
    n_chunks = D_FF // FF_CHUNK

    def prompt_writeback(s, sl):
        return [pltpu.make_async_copy(yt_ref.at[sl, :, b, :], yp_hbm.at[b, pl.ds(s * tt, tt), :],
                                      out_sem.at[sl, b]) for b in range(nb)]

    def sample_writeback(sl):
        return [pltpu.make_async_copy(yt_ref.at[sl], ys_hbm, out_sem.at[sl, 0])]

    def wait_writeback(s, sl):
        @pl.when(s < n_prompt)
        def _():
            for cp in prompt_writeback(s, sl):
                cp.wait()

        @pl.when(s >= n_prompt)
        def _():
            for cp in sample_writeback(sl):
                cp.wait()

    row_chunks = [(w_hbm, w_ref, r) for w_hbm, w_ref in ((wg_hbm, wg_ref), (wu_hbm, wu_ref), (wd_hbm, wd_ref))
                  for r in range(0, w_hbm.shape[0], STAGE_ROWS)]

    def chunk_copy(c):
        w_hbm, _, r = row_chunks[c]
        sl = c % FF_STAGE_SLOTS
        return pltpu.make_async_copy(w_hbm.at[pl.ds(r, STAGE_ROWS), :],
                                     stage_ref.at[sl, :, pl.ds(0, w_hbm.shape[1])], w_sem.at[sl])

    @pl.when(step == 0)
    def _():
        for c in range(FF_STAGE_SLOTS):
            chunk_copy(c).start()
        for c, (w_hbm, w_ref, r) in enumerate(row_chunks):
            chunk_copy(c).wait()
            w_ref[r:r + STAGE_ROWS, :] = stage_ref[c % FF_STAGE_SLOTS, :, :w_hbm.shape[1]].astype(BF16)
            if c + FF_STAGE_SLOTS < len(row_chunks):
                chunk_copy(c + FF_STAGE_SLOTS).start()

    @pl.when(step >= 2)
    def _():
        wait_writeback(step - 2, slot)

    halves = [slice(r, r + FFN_ROWS) for r in range(0, TILE_ROWS, FFN_ROWS)]
    hs = [_rms(x_ref[rsl, :], g2_ref[...]).astype(BF16) for rsl in halves]
    accs = [x_ref[rsl, :] for rsl in halves]
    for c in range(n_chunks):
        cols = slice(c * FF_CHUNK, (c + 1) * FF_CHUNK)
        for i in range(len(halves)):
            gate = _dot(hs[i], wg_ref[:, cols])
            up = _dot(hs[i], wu_ref[:, cols])
            accs[i] = accs[i] + _dot((jax.nn.silu(gate) * up).astype(BF16), wd_ref[cols, :])
    for i, rsl in enumerate(halves):
        groups = slice(rsl.start // nb, rsl.stop // nb)
        yt_ref[slot, groups] = _rms(accs[i], gf_ref[...]).reshape(groups.stop - groups.start, nb, D_MODEL)

    @pl.when(step < n_prompt)
    def _():
        for cp in prompt_writeback(step, slot):
            cp.start()

    @pl.when(step >= n_prompt)
    def _():
        for cp in sample_writeback(slot):
            cp.start()

    @pl.when(step == n_steps - 1)
    def _():
        wait_writeback(step - 1, 1 - slot)
        wait_writeback(step, slot)


def _ffn(x1, g2, gf, ffn_w, *, nb, seq):
    n = x1.shape[0]
    n_prompt = nb * seq // TILE_ROWS
    assert n == (n_prompt + 1) * TILE_ROWS and TILE_ROWS % nb == 0 and n_prompt >= 1
    tt = TILE_ROWS // nb
    w_gate, w_up, w_down = ffn_w
    return pl.pallas_call(
        functools.partial(_ffn_kernel, nb, n_prompt),
        grid=(n_prompt + 1,),
        in_specs=[pl.BlockSpec((TILE_ROWS, D_MODEL), lambda i: (i, 0)),
                  _const_spec(g2.shape), _const_spec(gf.shape)] + [_ANY_SPEC] * 3,
        out_specs=(_ANY_SPEC, _ANY_SPEC),
        out_shape=(jax.ShapeDtypeStruct((nb, seq, D_MODEL), F32),
                   jax.ShapeDtypeStruct((tt, nb, D_MODEL), F32)),
        scratch_shapes=[pltpu.VMEM((2, tt, nb, D_MODEL), F32),
                        pltpu.SemaphoreType.DMA((2, nb)),
                        pltpu.VMEM(w_gate.shape, BF16),
                        pltpu.VMEM(w_up.shape, BF16),
                        pltpu.VMEM(w_down.shape, BF16),
                        pltpu.VMEM((FF_STAGE_SLOTS, STAGE_ROWS, D_FF), F32),
                        pltpu.SemaphoreType.DMA((FF_STAGE_SLOTS,))],
        compiler_params=pltpu.CompilerParams(
            dimension_semantics=("arbitrary",), vmem_limit_bytes=VMEM_LIMIT_BYTES),
        name="ffn",
    )(x1, g2, gf, w_gate, w_up, w_down)


def kernel(x_prompt, x_sample, state_s5_re, state_s5_im, norm1, w_in, lam_re, lam_im, log_dt, b_re, b_im, c_re, c_im, d_skip, w_glu, b_glu, cm_ln_g, cm_ln_b, w_s, b_s, g_s5, g_cm, w_out, norm2, w_gate, w_up, w_down, norm_f):
    depth = norm1.shape[0]
    assert depth == 1
    l = 0
    batch, seq, _ = x_prompt.shape
    dec_batch, dec_seq, _ = x_sample.shape

    ar, ai, bdr, bdi, cdr, cdi = pl.pallas_call(
        _prep_kernel,
        out_shape=(
            jax.ShapeDtypeStruct((1, N_STATES), F32),
            jax.ShapeDtypeStruct((1, N_STATES), F32),
            jax.ShapeDtypeStruct((S5_HALVES, HALF_IN, HALF_STATES), BF16),
            jax.ShapeDtypeStruct((S5_HALVES, HALF_IN, HALF_STATES), BF16),
            jax.ShapeDtypeStruct((S5_HALVES, HALF_STATES, HALF_IN), BF16),
            jax.ShapeDtypeStruct((S5_HALVES, HALF_STATES, HALF_IN), BF16),
        ),
        name="s5_prepare",
    )(lam_re[l], lam_im[l], log_dt[l][:, None], jnp.swapaxes(b_re, 2, 3), jnp.swapaxes(b_im, 2, 3),
      c_re[l].reshape(S5_WIDTH, S5_STATE), c_im[l].reshape(S5_WIDTH, S5_STATE))

    row = lambda a: a.reshape(1, -1)
    wmix_p = w_s[l].reshape(CM_HEADS // 2, 2 * CHUNK, CHUNK)
    per_head = jnp.concatenate([b_s[l].T, w_s[l][:, :dec_seq, :dec_seq].reshape(CM_HEADS, -1).T], axis=0)
    cmix = jnp.repeat(per_head, CM_HEAD_DIM, axis=1)
    cmix = jnp.moveaxis(cmix.reshape(-1, CM_WIDTH // LANES, LANES), 1, 0)
    small_w = [row(norm1[l]), bdr, bdi, cdr, cdi, ar, ai, row(d_skip[l]), row(b_glu[l]),
               row(cm_ln_g[l]), row(cm_ln_b[l]), row(g_s5[l]), row(g_cm[l]), wmix_p, cmix]
    to_state_major = lambda a: jnp.transpose(a, (1, 2, 0)).reshape(N_STATES, dec_batch)
    from_state_major = lambda a: jnp.transpose(a.reshape(S5_GROUPS, S5_STATE, dec_batch), (2, 0, 1))[None]
    h0 = (to_state_major(state_s5_re[l]), to_state_major(state_s5_im[l]))

    x1, pr, pi_, pv, sr, si, sv = _mixer(x_prompt, x_sample, h0, (w_in[l], w_out[l], w_glu[l]), small_w)
    y_prompt, y_sample = _ffn(x1, row(norm2[l]), row(norm_f), (w_gate[l], w_up[l], w_down[l]),
                              nb=batch, seq=seq)

    st = lambda a, n: a.reshape(1, n, S5_GROUPS, S5_STATE)
    return (y_prompt, y_sample.reshape(dec_batch, dec_seq, D_MODEL),
            st(pr, batch), st(pi_, batch), pv[None],
            from_state_major(sr), from_state_major(si), sv[None])
```

```python
import functools

import jax
import jax.numpy as jnp
from jax import lax
from jax.experimental import pallas as pl
from jax.experimental.pallas import tpu as pltpu

F32 = jnp.float32
BF16 = jnp.bfloat16

D_MODEL = 1024
S5_WIDTH = 512
S5_GROUP = 16
S5_GROUPS = 32
S5_STATE = 64
N_STATES = S5_GROUPS * S5_STATE
CM_WIDTH = 512
CM_HEADS = 8
CM_HEAD_DIM = 64
CHUNK = 128
D_FF = 2816
EPS = 1e-6

S5_HALVES = 2
HALF_GROUPS = S5_GROUPS // S5_HALVES
HALF_IN = S5_WIDTH // S5_HALVES
HALF_STATES = N_STATES // S5_HALVES
SUBLANES = 8
LANES = 128
SCAN_LANES = 1024
SCAN_BLOCKS = SCAN_LANES // LANES
MXU_COLS = 256
TILE_ROWS = SUBLANES * CHUNK
BLOCK_ROWS = TILE_ROWS + SUBLANES
IN_ROWS = 256
MID_ROWS = 1024
GATE_ROWS = 1024
FF_CHUNK = 256
FFN_ROWS = 512
FF_STAGE_SLOTS = 4
VMEM_LIMIT_BYTES = 58 * 1024 * 1024


def _rms(x, g):
    r = lax.rsqrt(jnp.mean(x * x, axis=-1, keepdims=True) + EPS)
    return x * r * g


def _dot(a, b):
    return jnp.dot(a, b, preferred_element_type=F32)


def _load_lane_blocks(ref, n_blocks):
    return jnp.concatenate([ref[k, :TILE_ROWS, :] for k in range(n_blocks)], axis=1)


def _cast_weights(weights, slots, sems):
    plan = []
    for w_hbm, w_bf_ref in weights:
        n_rows, n_cols = w_hbm.shape
        for k in range(n_cols // LANES):
            i = len(plan)
            stage = slots[i].at[pl.ds(0, n_rows), :]
            copy = pltpu.make_async_copy(w_hbm.at[:, pl.ds(k * LANES, LANES)], stage, sems.at[i])
            plan.append((copy, stage, w_bf_ref, k))
    for copy, _, _, _ in plan:
        copy.start()
    for copy, stage, w_bf_ref, k in plan:
        copy.wait()
        w_bf_ref[:, k * LANES:(k + 1) * LANES] = stage[...].astype(BF16)


def _prep_kernel(lr_ref, li_ref, ldt_ref, br_ref, bi_ref, cr_ref, ci_ref,
                 ar_ref, ai_ref, bdr_ref, bdi_ref, cdr_ref, cdi_ref):
    dt = jnp.exp(ldt_ref[...])
    lr = lr_ref[...]
    li = li_ref[...]
    mag = jnp.exp(lr * dt)
    ar = mag * jnp.cos(li * dt)
    ai = mag * jnp.sin(li * dt)
    den = lr * lr + li * li
    qr = ((ar - 1.0) * lr + ai * li) / den
    qi = (ai * lr - (ar - 1.0) * li) / den
    for g in range(S5_GROUPS):
        ar_ref[:, g * S5_STATE:(g + 1) * S5_STATE] = ar[g:g + 1, :]
        ai_ref[:, g * S5_STATE:(g + 1) * S5_STATE] = ai[g:g + 1, :]

    def iota(shape, dim):
        return lax.broadcasted_iota(jnp.int32, shape, dim)

    rep = (iota((S5_STATE, HALF_STATES), 1) % S5_STATE == iota((S5_STATE, HALF_STATES), 0)).astype(BF16)

    def repeat_lanes(x, rep):
        hi = x.astype(BF16)
        rest = x - hi.astype(F32)
        mid = rest.astype(BF16)
        lo = (rest - mid.astype(F32)).astype(BF16)
        return _dot(hi, rep) + _dot(mid, rep) + _dot(lo, rep)

    diag = iota((HALF_IN, HALF_STATES), 0) // S5_GROUP == iota((HALF_IN, HALF_STATES), 1) // S5_STATE
    for hv in range(S5_HALVES):
        groups = range(hv * HALF_GROUPS, (hv + 1) * HALF_GROUPS)
        bb_r = jnp.concatenate([qr[g:g + 1, :] * br_ref[0, g] - qi[g:g + 1, :] * bi_ref[0, g]
                                for g in groups], axis=0)
        bb_i = jnp.concatenate([qr[g:g + 1, :] * bi_ref[0, g] + qi[g:g + 1, :] * br_ref[0, g]
                                for g in groups], axis=0)
        for bb, o_ref in ((bb_r, bdr_ref), (bb_i, bdi_ref)):
            o_ref[hv] = jnp.where(diag, repeat_lanes(bb, rep), 0.0).astype(BF16)
        for c_ref, o_ref in ((cr_ref, cdr_ref), (ci_ref, cdi_ref)):
            c_half = c_ref[hv * HALF_IN:(hv + 1) * HALF_IN, :]
            o_ref[hv] = jnp.where(diag, repeat_lanes(c_half, rep), 0.0).T.astype(BF16)


def _cm_uv(p_cm, lng_ref, lnb_ref):
    uv = jax.nn.gelu(p_cm)
    u = uv[:, :CM_WIDTH]
    vv = uv[:, CM_WIDTH:]
    mu = jnp.mean(vv, axis=-1, keepdims=True)
    vc = vv - mu
    v = vc * lax.rsqrt(jnp.mean(vc * vc, axis=-1, keepdims=True) + EPS) * lng_ref[...] + lnb_ref[...]
    return u, v


def _scan_lanes(bur_ref, bui_ref, ar_ref, ai_ref, hf, lc, h_init, rows_at, n_steps, unrolled):
    blocks = range(lc * SCAN_BLOCKS, (lc + 1) * SCAN_BLOCKS)
    glanes = slice(hf * HALF_STATES + lc * SCAN_LANES, hf * HALF_STATES + (lc + 1) * SCAN_LANES)
    a_r = jnp.broadcast_to(ar_ref[:, glanes], (SUBLANES, SCAN_LANES))
    a_i = jnp.broadcast_to(ai_ref[:, glanes], (SUBLANES, SCAN_LANES))

    def scan_step(t, carry):
        hr, hi = carry
        trows = rows_at(t)
        b_r = jnp.concatenate([bur_ref[k, trows, :] for k in blocks], axis=1)
        b_i = jnp.concatenate([bui_ref[k, trows, :] for k in blocks], axis=1)
        nr = a_r * hr - a_i * hi + b_r
        ni = a_r * hi + a_i * hr + b_i
        for n, k in enumerate(blocks):
            bur_ref[k, trows, :] = nr[:, n * LANES:(n + 1) * LANES]
            bui_ref[k, trows, :] = ni[:, n * LANES:(n + 1) * LANES]
        return nr, ni

    if unrolled:
        carry = h_init
        for t in range(n_steps):
            carry = scan_step(t, carry)
        return carry
    return lax.fori_loop(0, n_steps, scan_step, h_init, unroll=32)


def _mixer_kernel(n_prompt, dec_seq,
                  xp_hbm, xs_hbm, h0r_hbm, h0i_hbm, win_hbm, wout_hbm, wglu_hbm,
                  g1_ref, bdr_ref, bdi_ref, cdr_ref, cdi_ref, ar_ref, ai_ref, dskip_ref, bglu_ref,
                  lng_ref, lnb_ref, gs5_ref, gcm_ref, wmixp_ref, cmix_ref,
                  x1_ref, pr_ref, pi_ref, pv_ref, srt_ref, sit_ref, sv_ref,
                  xt_ref, in_sem, bur_ref, bui_ref, hsr_ref, hsi_ref, sr_ref, si_ref, cma_ref, cmb_ref, cmc_ref,
                  xs_ref, y0_ref, win_ref, wout_ref, wglu_ref, wtril_ref, w_sem, h0_sem):
    nb = SUBLANES
    rows = TILE_ROWS
    step = pl.program_id(0)
    slot = step % 2

    def fetch_prompt(s, sl):
        return [pltpu.make_async_copy(xp_hbm.at[b, pl.ds(s * CHUNK, CHUNK), :], xt_ref.at[sl, :, b, :],
                                      in_sem.at[sl, b]) for b in range(nb)]

    def fetch_sample(sl):
        return pltpu.make_async_copy(xs_hbm, xt_ref.at[sl], in_sem.at[sl, 0])

    @pl.when(step == 0)
    def _():
        for cp in fetch_prompt(0, 0):
            cp.start()
        h0_copies = [pltpu.make_async_copy(h0r_hbm, srt_ref, h0_sem.at[0]),
                     pltpu.make_async_copy(h0i_hbm, sit_ref, h0_sem.at[1])]
        for cp in h0_copies:
            cp.start()
        staging = [ref.at[k] for ref in (bur_ref, bui_ref, cma_ref, cmb_ref, cmc_ref)
                   for k in range(ref.shape[0])]
        _cast_weights([(win_hbm, win_ref), (wout_hbm, wout_ref), (wglu_hbm, wglu_ref)], staging, w_sem)
        for cp in h0_copies:
            cp.wait()
        sr_ref[...] = srt_ref[...].T
        si_ref[...] = sit_ref[...].T
        r_idx = lax.broadcasted_iota(jnp.int32, (2 * CHUNK, CHUNK), 0)
        c_idx = lax.broadcasted_iota(jnp.int32, (2 * CHUNK, CHUNK), 1)
        causal = (r_idx % CHUNK) >= c_idx
        for j in range(CM_HEADS // 2):
            wtril_ref[j] = jnp.where(causal, wmixp_ref[j], 0.0).astype(BF16)
        hsr_ref[...] = jnp.zeros_like(hsr_ref)
        hsi_ref[...] = jnp.zeros_like(hsi_ref)

    @pl.when(step + 1 < n_prompt)
    def _():
        for cp in fetch_prompt(step + 1, 1 - slot):
            cp.start()

    @pl.when(step + 1 == n_prompt)
    def _():
        fetch_sample(1 - slot).start()

    @pl.when(step < n_prompt)
    def _():
        for cp in fetch_prompt(step, slot):
            cp.wait()

    @pl.when(step == n_prompt)
    def _():
        fetch_sample(slot).wait()

    is_prompt = step < n_prompt
    is_sample = step == n_prompt
    cm_blocks = CM_WIDTH // LANES
    mxu_blocks = MXU_COLS // LANES
    us_ref, vs_ref, mx_ref = cma_ref, cmb_ref, cmc_ref
    def row_blocks(n):
        return [slice(r * n, (r + 1) * n) for r in range(rows // n)]

    def x_rows(rsl):
        groups = slice(rsl.start // nb, rsl.stop // nb)
        return xt_ref[slot, groups].reshape(rsl.stop - rsl.start, D_MODEL)

    def b_proj(hf, xs_h, rsl):
        for n in range(HALF_STATES // MXU_COLS):
            cols = slice(n * MXU_COLS, (n + 1) * MXU_COLS)
            b_r = _dot(xs_h, bdr_ref[hf, :, cols])
            b_i = _dot(xs_h, bdi_ref[hf, :, cols])
            for q in range(mxu_blocks):
                bur_ref[n * mxu_blocks + q, rsl, :] = b_r[:, q * LANES:(q + 1) * LANES]
                bui_ref[n * mxu_blocks + q, rsl, :] = b_i[:, q * LANES:(q + 1) * LANES]

    def c_proj(hf, rsl):
        y_h = None
        for n in range(HALF_STATES // MXU_COLS):
            krows = slice(n * MXU_COLS, (n + 1) * MXU_COLS)
            blocks = range(n * mxu_blocks, (n + 1) * mxu_blocks)
            h_r = jnp.concatenate([bur_ref[k, rsl, :] for k in blocks], axis=1).astype(BF16)
            h_i = jnp.concatenate([bui_ref[k, rsl, :] for k in blocks], axis=1).astype(BF16)
            part = _dot(h_r, cdr_ref[hf, krows, :]) - _dot(h_i, cdi_ref[hf, krows, :])
            y_h = part if y_h is None else y_h + part
        return y_h

    def scan_half(hf):
        @pl.when(is_prompt)
        def _():
            for lc in range(HALF_STATES // SCAN_LANES):
                glanes = slice(hf * HALF_STATES + lc * SCAN_LANES, hf * HALF_STATES + (lc + 1) * SCAN_LANES)
                hr, hi = _scan_lanes(bur_ref, bui_ref, ar_ref, ai_ref, hf, lc,
                                     (hsr_ref[:, glanes], hsi_ref[:, glanes]),
                                     lambda t: pl.ds(pl.multiple_of(t * nb, nb), nb), CHUNK, False)
                hsr_ref[:, glanes] = hr
                hsi_ref[:, glanes] = hi
                pr_ref[:, glanes] = hr
                pi_ref[:, glanes] = hi

        @pl.when(is_sample)
        def _():
            tt = dec_seq

            def scan_group(sg, carry):
                srows = pl.ds(pl.multiple_of(sg * SUBLANES, SUBLANES), SUBLANES)
                for lc in range(HALF_STATES // SCAN_LANES):
                    glanes = slice(hf * HALF_STATES + lc * SCAN_LANES,
                                   hf * HALF_STATES + (lc + 1) * SCAN_LANES)
                    hr, hi = _scan_lanes(bur_ref, bui_ref, ar_ref, ai_ref, hf, lc,
                                         (sr_ref[srows, glanes], si_ref[srows, glanes]),
                                         lambda t: pl.ds(sg * (SUBLANES * tt) + t, SUBLANES, stride=tt),
                                         tt, True)
                    sr_ref[srows, glanes] = hr
                    si_ref[srows, glanes] = hi
                return carry

            lax.fori_loop(0, rows // tt // SUBLANES, scan_group, 0)
            if hf == S5_HALVES - 1:
                srt_ref[...] = sr_ref[...].T
                sit_ref[...] = si_ref[...].T

    for rsl in row_blocks(IN_ROWS):
        h = _rms(x_rows(rsl), g1_ref[...]).astype(BF16)
        u, v = _cm_uv(_dot(h, win_ref[:, S5_WIDTH:]), lng_ref, lnb_ref)
        for k in range(cm_blocks):
            us_ref[k, rsl, :] = u[:, k * LANES:(k + 1) * LANES]
            vs_ref[k, rsl, :] = v[:, k * LANES:(k + 1) * LANES]
        xs = _dot(h, win_ref[:, :S5_WIDTH])
        xs_ref[rsl, :] = xs
        b_proj(0, xs[:, :HALF_IN].astype(BF16), rsl)

    first_head = lax.broadcasted_iota(jnp.int32, (CHUNK, LANES), 1) < CM_HEAD_DIM
    for j in range(CM_HEADS // 2):
        w_pair = wtril_ref[j]
        bias = cmix_ref[j, :CHUNK, :]
        for b in range(nb):
            seq_rows = pl.ds(b, CHUNK, stride=nb)
            r = _dot(w_pair, vs_ref[j, seq_rows, :].astype(BF16))
            mixed = jnp.where(first_head, r[:CHUNK], r[CHUNK:]) + bias
            mx_ref[j, seq_rows, :] = us_ref[j, seq_rows, :] * mixed

    @pl.when(step == n_prompt - 1)
    def _():
        for j in range(cm_blocks):
            for b in range(nb):
                pv_ref[b, :, j * LANES:(j + 1) * LANES] = vs_ref[j, pl.ds(b, CHUNK, stride=nb), :]

    @pl.when(is_sample)
    def _():
        tt = dec_seq
        n_seq = rows // tt
        sv_ref[...] = _load_lane_blocks(vs_ref, cm_blocks).reshape(n_seq, tt, CM_WIDTH)
        def mix_block(k, carry):
            v_at = [vs_ref[k, pl.ds(s, n_seq, stride=tt), :] for s in range(tt)]
            for t in range(tt):
                w_row = CHUNK + t * tt
                acc = cmix_ref[k, t:t + 1, :] + cmix_ref[k, w_row:w_row + 1, :] * v_at[0]
                for s in range(1, t + 1):
                    acc = acc + cmix_ref[k, w_row + s:w_row + s + 1, :] * v_at[s]
                trows = pl.ds(t, n_seq, stride=tt)
                mx_ref[k, trows, :] = us_ref[k, trows, :] * acc
            return carry

        lax.fori_loop(0, cm_blocks, mix_block, 0)

    scan_half(0)
    for rsl in row_blocks(MID_ROWS):
        y0_ref[rsl, :] = c_proj(0, rsl)
        b_proj(1, xs_ref[rsl, HALF_IN:].astype(BF16), rsl)
        o_cm = jnp.concatenate([mx_ref[k, rsl, :] for k in range(cm_blocks)], axis=1)
        n_cm = _rms(o_cm, gcm_ref[...]).astype(BF16)
        x1_ref[rsl, :] = x_rows(rsl) + _dot(n_cm, wout_ref[S5_WIDTH:, :])
    scan_half(1)
    for rsl in row_blocks(GATE_ROWS):
        y = jnp.concatenate([y0_ref[rsl, :], c_proj(1, rsl)], axis=1) + dskip_ref[...] * xs_ref[rsl, :]
        g = jax.nn.gelu(y)
        o_s5 = g * jax.nn.sigmoid(_dot(g.astype(BF16), wglu_ref[...]) + bglu_ref[...])
        n_s5 = _rms(o_s5, gs5_ref[...]).astype(BF16)
        x1_ref[rsl, :] = x1_ref[rsl, :] + _dot(n_s5, wout_ref[:S5_WIDTH, :])


def _const_spec(shape):
    nd = len(shape)
    return pl.BlockSpec(shape, lambda i, nd=nd: (0,) * nd, pipeline_mode=pl.Buffered(1))


_ANY_SPEC = pl.BlockSpec(memory_space=pl.ANY)


def _mixer(x_prompt, x_sample, h0, big_w, small_w):
    nb, seq, _ = x_prompt.shape
    dec_batch, dec_seq, _ = x_sample.shape
    assert nb == SUBLANES and seq % CHUNK == 0
    assert dec_batch * dec_seq == TILE_ROWS and dec_seq <= SUBLANES and dec_batch == CHUNK
    n_prompt = seq // CHUNK
    w_in, w_out, w_glu = big_w
    in_specs = [_ANY_SPEC] * 7 + [_const_spec(a.shape) for a in small_w]
    out_shape = (
        jax.ShapeDtypeStruct(((n_prompt + 1) * TILE_ROWS, D_MODEL), F32),
        jax.ShapeDtypeStruct((nb, N_STATES), F32),
        jax.ShapeDtypeStruct((nb, N_STATES), F32),
        jax.ShapeDtypeStruct((nb, CHUNK, CM_WIDTH), F32),
        jax.ShapeDtypeStruct((N_STATES, dec_batch), F32),
        jax.ShapeDtypeStruct((N_STATES, dec_batch), F32),
        jax.ShapeDtypeStruct((dec_batch, dec_seq, CM_WIDTH), F32),
    )
    out_specs = (pl.BlockSpec((TILE_ROWS, D_MODEL), lambda i: (i, 0)),) + tuple(
        _const_spec(o.shape) for o in out_shape[1:])
    scratch = [
        pltpu.VMEM((2, CHUNK, nb, D_MODEL), F32),
        pltpu.SemaphoreType.DMA((2, nb)),
        pltpu.VMEM((HALF_STATES // LANES, BLOCK_ROWS, LANES), F32),
        pltpu.VMEM((HALF_STATES // LANES, BLOCK_ROWS, LANES), F32),
        pltpu.VMEM((nb, N_STATES), F32),
        pltpu.VMEM((nb, N_STATES), F32),
        pltpu.VMEM((dec_batch, N_STATES), F32),
        pltpu.VMEM((dec_batch, N_STATES), F32),
        pltpu.VMEM((CM_WIDTH // LANES, BLOCK_ROWS, LANES), F32),
        pltpu.VMEM((CM_WIDTH // LANES, BLOCK_ROWS, LANES), F32),
        pltpu.VMEM((CM_WIDTH // LANES, BLOCK_ROWS, LANES), F32),
        pltpu.VMEM((TILE_ROWS, S5_WIDTH), F32),
        pltpu.VMEM((TILE_ROWS, HALF_IN), F32),
        pltpu.VMEM(w_in.shape, BF16),
        pltpu.VMEM(w_out.shape, BF16),
        pltpu.VMEM(w_glu.shape, BF16),
        pltpu.VMEM((CM_HEADS // 2, 2 * CHUNK, CHUNK), BF16),
        pltpu.SemaphoreType.DMA(((w_in.shape[1] + w_out.shape[1] + w_glu.shape[1]) // LANES,)),
        pltpu.SemaphoreType.DMA((2,)),
    ]
    return pl.pallas_call(
        functools.partial(_mixer_kernel, n_prompt, dec_seq),
        grid=(n_prompt + 1,),
        in_specs=in_specs,
        out_specs=out_specs,
        out_shape=out_shape,
        scratch_shapes=scratch,
        compiler_params=pltpu.CompilerParams(
            dimension_semantics=("arbitrary",), vmem_limit_bytes=VMEM_LIMIT_BYTES),
        name="mixer",
    )(x_prompt, x_sample, *h0, w_in, w_out, w_glu, *small_w)


def _ffn_kernel(nb, n_prompt, x_ref, g2_ref, gf_ref, wg_hbm, wu_hbm, wd_hbm, yp_hbm, ys_hbm,
                yt_ref, out_sem, wg_ref, wu_ref, wd_ref, stage_gu_ref, stage_d_ref, w_sem):
    tt = TILE_ROWS // nb
    step = pl.program_id(0)
    n_steps = pl.num_programs(0)
    slot = step % 2
    n_chunks = D_FF // FF_CHUNK

    def prompt_writeback(s, sl):
        return [pltpu.make_async_copy(yt_ref.at[sl, :, b, :], yp_hbm.at[b, pl.ds(s * tt, tt), :],
                                      out_sem.at[sl, b]) for b in range(nb)]

    def sample_writeback(sl):
        return [pltpu.make_async_copy(yt_ref.at[sl], ys_hbm, out_sem.at[sl, 0])]

    def wait_writeback(s, sl):
        @pl.when(s < n_prompt)
        def _():
            for cp in prompt_writeback(s, sl):
                cp.wait()

        @pl.when(s >= n_prompt)
        def _():
            for cp in sample_writeback(sl):
                cp.wait()

    def weight_chunk_copies(c, sl):
        cols = pl.ds(c * FF_CHUNK, FF_CHUNK)
        return [pltpu.make_async_copy(wg_hbm.at[:, cols], stage_gu_ref.at[sl, 0], w_sem.at[sl, 0]),
                pltpu.make_async_copy(wu_hbm.at[:, cols], stage_gu_ref.at[sl, 1], w_sem.at[sl, 1]),
                pltpu.make_async_copy(wd_hbm.at[cols, :], stage_d_ref.at[sl], w_sem.at[sl, 2])]

    def stage_in_chunk(c):
        sl = c % FF_STAGE_SLOTS
        cols = slice(c * FF_CHUNK, (c + 1) * FF_CHUNK)
        for cp in weight_chunk_copies(c, sl):
            cp.wait()
        wg_ref[:, cols] = stage_gu_ref[sl, 0].astype(BF16)
        wu_ref[:, cols] = stage_gu_ref[sl, 1].astype(BF16)
        wd_ref[cols, :] = stage_d_ref[sl].astype(BF16)
        if c + FF_STAGE_SLOTS < n_chunks:
            for cp in weight_chunk_copies(c + FF_STAGE_SLOTS, sl):
                cp.start()

    @pl.when(step == 0)
    def _():
        for c in range(FF_STAGE_SLOTS):
            for cp in weight_chunk_copies(c, c):
                cp.start()
        for c in range(n_chunks):
            stage_in_chunk(c)

    @pl.when(step >= 2)
    def _():
        wait_writeback(step - 2, slot)

    halves = [slice(r, r + FFN_ROWS) for r in range(0, TILE_ROWS, FFN_ROWS)]
    hs = [_rms(x_ref[rsl, :], g2_ref[...]).astype(BF16) for rsl in halves]
    accs = [x_ref[rsl, :] for rsl in halves]
    for c in range(n_chunks):
        cols = slice(c * FF_CHUNK, (c + 1) * FF_CHUNK)
        for i in range(len(halves)):
            gate = _dot(hs[i], wg_ref[:, cols])
            up = _dot(hs[i], wu_ref[:, cols])
            accs[i] = accs[i] + _dot((jax.nn.silu(gate) * up).astype(BF16), wd_ref[cols, :])
    for i, rsl in enumerate(halves):
        groups = slice(rsl.start // nb, rsl.stop // nb)
        yt_ref[slot, groups] = _rms(accs[i], gf_ref[...]).reshape(groups.stop - groups.start, nb, D_MODEL)

    @pl.when(step < n_prompt)
    def _():
        for cp in prompt_writeback(step, slot):
            cp.start()

    @pl.when(step >= n_prompt)
    def _():
        for cp in sample_writeback(slot):
            cp.start()

    @pl.when(step == n_steps - 1)
    def _():
        wait_writeback(step - 1, 1 - slot)
        wait_writeback(step, slot)


def _ffn(x1, g2, gf, ffn_w, *, nb, seq):
    n = x1.shape[0]
    n_prompt = nb * seq // TILE_ROWS
    assert n == (n_prompt + 1) * TILE_ROWS and TILE_ROWS % nb == 0 and n_prompt >= 1
    tt = TILE_ROWS // nb
    w_gate, w_up, w_down = ffn_w
    return pl.pallas_call(
        functools.partial(_ffn_kernel, nb, n_prompt),
        grid=(n_prompt + 1,),
        in_specs=[pl.BlockSpec((TILE_ROWS, D_MODEL), lambda i: (i, 0)),
                  _const_spec(g2.shape), _const_spec(gf.shape)] + [_ANY_SPEC] * 3,
        out_specs=(_ANY_SPEC, _ANY_SPEC),
        out_shape=(jax.ShapeDtypeStruct((nb, seq, D_MODEL), F32),
                   jax.ShapeDtypeStruct((tt, nb, D_MODEL), F32)),
        scratch_shapes=[pltpu.VMEM((2, tt, nb, D_MODEL), F32),
                        pltpu.SemaphoreType.DMA((2, nb)),
                        pltpu.VMEM(w_gate.shape, BF16),
                        pltpu.VMEM(w_up.shape, BF16),
                        pltpu.VMEM(w_down.shape, BF16),
                        pltpu.VMEM((FF_STAGE_SLOTS, 2, D_MODEL, FF_CHUNK), F32),
                        pltpu.VMEM((FF_STAGE_SLOTS, FF_CHUNK, D_MODEL), F32),
                        pltpu.SemaphoreType.DMA((FF_STAGE_SLOTS, 3))],
        compiler_params=pltpu.CompilerParams(
            dimension_semantics=("arbitrary",), vmem_limit_bytes=VMEM_LIMIT_BYTES),
        name="ffn",
    )(x1, g2, gf, w_gate, w_up, w_down)


def kernel(x_prompt, x_sample, state_s5_re, state_s5_im, norm1, w_in, lam_re, lam_im, log_dt, b_re, b_im, c_re, c_im, d_skip, w_glu, b_glu, cm_ln_g, cm_ln_b, w_s, b_s, g_s5, g_cm, w_out, norm2, w_gate, w_up, w_down, norm_f):
    depth = norm1.shape[0]
    assert depth == 1
    l = 0
    batch, seq, _ = x_prompt.shape
    dec_batch, dec_seq, _ = x_sample.shape

    ar, ai, bdr, bdi, cdr, cdi = pl.pallas_call(
        _prep_kernel,
        out_shape=(
            jax.ShapeDtypeStruct((1, N_STATES), F32),
            jax.ShapeDtypeStruct((1, N_STATES), F32),
            jax.ShapeDtypeStruct((S5_HALVES, HALF_IN, HALF_STATES), BF16),
            jax.ShapeDtypeStruct((S5_HALVES, HALF_IN, HALF_STATES), BF16),
            jax.ShapeDtypeStruct((S5_HALVES, HALF_STATES, HALF_IN), BF16),
            jax.ShapeDtypeStruct((S5_HALVES, HALF_STATES, HALF_IN), BF16),
        ),
        name="s5_prepare",
    )(lam_re[l], lam_im[l], log_dt[l][:, None], jnp.swapaxes(b_re, 2, 3), jnp.swapaxes(b_im, 2, 3),
      c_re[l].reshape(S5_WIDTH, S5_STATE), c_im[l].reshape(S5_WIDTH, S5_STATE))

    row = lambda a: a.reshape(1, -1)
    wmix_p = w_s[l].reshape(CM_HEADS // 2, 2 * CHUNK, CHUNK)
    per_head = jnp.concatenate([b_s[l].T, w_s[l][:, :dec_seq, :dec_seq].reshape(CM_HEADS, -1).T], axis=0)
    cmix = jnp.repeat(per_head, CM_HEAD_DIM, axis=1)
    cmix = jnp.moveaxis(cmix.reshape(-1, CM_WIDTH // LANES, LANES), 1, 0)
    small_w = [row(norm1[l]), bdr, bdi, cdr, cdi, ar, ai, row(d_skip[l]), row(b_glu[l]),
               row(cm_ln_g[l]), row(cm_ln_b[l]), row(g_s5[l]), row(g_cm[l]), wmix_p, cmix]
    to_state_major = lambda a: jnp.transpose(a, (1, 2, 0)).reshape(N_STATES, dec_batch)
    from_state_major = lambda a: jnp.transpose(a.reshape(S5_GROUPS, S5_STATE, dec_batch), (2, 0, 1))[None]
    h0 = (to_state_major(state_s5_re[l]), to_state_major(state_s5_im[l]))

    x1, pr, pi_, pv, sr, si, sv = _mixer(x_prompt, x_sample, h0, (w_in[l], w_out[l], w_glu[l]), small_w)
    y_prompt, y_sample = _ffn(x1, row(norm2[l]), row(norm_f), (w_gate[l], w_up[l], w_down[l]),
                              nb=batch, seq=seq)

    st = lambda a, n: a.reshape(1, n, S5_GROUPS, S5_STATE)
    return (y_prompt, y_sample.reshape(dec_batch, dec_seq, D_MODEL),
            st(pr, batch), st(pi_, batch), pv[None],
            from_state_major(sr), from_state_major(si), sv[None])
```

```python
import functools

import jax
import jax.numpy as jnp
from jax import lax
from jax.experimental import pallas as pl
from jax.experimental.pallas import tpu as pltpu

F32 = jnp.float32
BF16 = jnp.bfloat16

D_MODEL = 1024
S5_WIDTH = 512
S5_GROUP = 16
S5_GROUPS = 32
S5_STATE = 64
N_STATES = S5_GROUPS * S5_STATE
CM_WIDTH = 512
CM_HEADS = 8
CM_HEAD_DIM = 64
CHUNK = 128
D_FF = 2816
EPS = 1e-6

S5_HALVES = 2
HALF_GROUPS = S5_GROUPS // S5_HALVES
HALF_IN = S5_WIDTH // S5_HALVES
HALF_STATES = N_STATES // S5_HALVES
SUBLANES = 8
LANES = 128
SCAN_LANES = 1024
SCAN_BLOCKS = SCAN_LANES // LANES
MXU_COLS = 256
TILE_ROWS = SUBLANES * CHUNK
BLOCK_ROWS = TILE_ROWS + SUBLANES
IN_ROWS = 256
MID_ROWS = 1024
GATE_ROWS = 1024
FF_CHUNK = 256
FFN_ROWS = 512
FF_STAGE_SLOTS = 4
VMEM_LIMIT_BYTES = 58 * 1024 * 1024


def _rms(x, g):
    r = lax.rsqrt(jnp.mean(x * x, axis=-1, keepdims=True) + EPS)
    return x * r * g


def _dot(a, b):
    return jnp.dot(a, b, preferred_element_type=F32)


def _load_lane_blocks(ref, n_blocks):
    return jnp.concatenate([ref[k, :TILE_ROWS, :] for k in range(n_blocks)], axis=1)


def _cast_weights(weights, slots, sems):
    plan = []
    for w_hbm, w_bf_ref in weights:
        n_rows, n_cols = w_hbm.shape
        for k in range(n_cols // LANES):
            i = len(plan)
            stage = slots[i].at[pl.ds(0, n_rows), :]
            copy = pltpu.make_async_copy(w_hbm.at[:, pl.ds(k * LANES, LANES)], stage, sems.at[i])
            plan.append((copy, stage, w_bf_ref, k))
    for copy, _, _, _ in plan:
        copy.start()
    for copy, stage, w_bf_ref, k in plan:
        copy.wait()
        w_bf_ref[:, k * LANES:(k + 1) * LANES] = stage[...].astype(BF16)


def _prep_kernel(lr_ref, li_ref, ldt_ref, br_ref, bi_ref, cr_ref, ci_ref,
                 ar_ref, ai_ref, bdr_ref, bdi_ref, cdr_ref, cdi_ref):
    on_diag = (lax.broadcasted_iota(jnp.int32, (S5_GROUPS, S5_GROUPS), 0)
               == lax.broadcasted_iota(jnp.int32, (S5_GROUPS, S5_GROUPS), 1))
    log_dt = jnp.sum(jnp.where(on_diag, ldt_ref[...], 0.0), axis=1, keepdims=True)
    dt = jnp.exp(log_dt)
    lr = lr_ref[...]
    li = li_ref[...]
    mag = jnp.exp(lr * dt)
    ar = mag * jnp.cos(li * dt)
    ai = mag * jnp.sin(li * dt)
    den = lr * lr + li * li
    qr = ((ar - 1.0) * lr + ai * li) / den
    qi = (ai * lr - (ar - 1.0) * li) / den
    for g in range(S5_GROUPS):
        ar_ref[:, g * S5_STATE:(g + 1) * S5_STATE] = ar[g:g + 1, :]
        ai_ref[:, g * S5_STATE:(g + 1) * S5_STATE] = ai[g:g + 1, :]

    def iota(shape, dim):
        return lax.broadcasted_iota(jnp.int32, shape, dim)

    rep = (iota((S5_STATE, HALF_STATES), 1) % S5_STATE == iota((S5_STATE, HALF_STATES), 0)).astype(BF16)

    def repeat_lanes(x, rep):
        hi = x.astype(BF16)
        rest = x - hi.astype(F32)
        mid = rest.astype(BF16)
        lo = (rest - mid.astype(F32)).astype(BF16)
        return _dot(hi, rep) + _dot(mid, rep) + _dot(lo, rep)

    diag = iota((HALF_IN, HALF_STATES), 0) // S5_GROUP == iota((HALF_IN, HALF_STATES), 1) // S5_STATE
    for hv in range(S5_HALVES):
        groups = range(hv * HALF_GROUPS, (hv + 1) * HALF_GROUPS)
        bb_r = jnp.concatenate([qr[g:g + 1, :] * br_ref[0, g] - qi[g:g + 1, :] * bi_ref[0, g]
                                for g in groups], axis=0)
        bb_i = jnp.concatenate([qr[g:g + 1, :] * bi_ref[0, g] + qi[g:g + 1, :] * br_ref[0, g]
                                for g in groups], axis=0)
        for bb, o_ref in ((bb_r, bdr_ref), (bb_i, bdi_ref)):
            o_ref[hv] = jnp.where(diag, repeat_lanes(bb, rep), 0.0).astype(BF16)
        for c_ref, o_ref in ((cr_ref, cdr_ref), (ci_ref, cdi_ref)):
            c_half = c_ref[hv * HALF_IN:(hv + 1) * HALF_IN, :]
            o_ref[hv] = jnp.where(diag, repeat_lanes(c_half, rep), 0.0).T.astype(BF16)


def _cm_uv(p_cm, lng_ref, lnb_ref):
    uv = jax.nn.gelu(p_cm)
    u = uv[:, :CM_WIDTH]
    vv = uv[:, CM_WIDTH:]
    mu = jnp.mean(vv, axis=-1, keepdims=True)
    vc = vv - mu
    v = vc * lax.rsqrt(jnp.mean(vc * vc, axis=-1, keepdims=True) + EPS) * lng_ref[...] + lnb_ref[...]
    return u, v


def _scan_lanes(bur_ref, bui_ref, ar_ref, ai_ref, hf, lc, h_init, rows_at, n_steps, unrolled):
    blocks = range(lc * SCAN_BLOCKS, (lc + 1) * SCAN_BLOCKS)
    glanes = slice(hf * HALF_STATES + lc * SCAN_LANES, hf * HALF_STATES + (lc + 1) * SCAN_LANES)
    a_r = jnp.broadcast_to(ar_ref[:, glanes], (SUBLANES, SCAN_LANES))
    a_i = jnp.broadcast_to(ai_ref[:, glanes], (SUBLANES, SCAN_LANES))

    def scan_step(t, carry):
        hr, hi = carry
        trows = rows_at(t)
        b_r = jnp.concatenate([bur_ref[k, trows, :] for k in blocks], axis=1)
        b_i = jnp.concatenate([bui_ref[k, trows, :] for k in blocks], axis=1)
        nr = a_r * hr - a_i * hi + b_r
        ni = a_r * hi + a_i * hr + b_i
        for n, k in enumerate(blocks):
            bur_ref[k, trows, :] = nr[:, n * LANES:(n + 1) * LANES]
            bui_ref[k, trows, :] = ni[:, n * LANES:(n + 1) * LANES]
        return nr, ni

    if unrolled:
        carry = h_init
        for t in range(n_steps):
            carry = scan_step(t, carry)
        return carry
    return lax.fori_loop(0, n_steps, scan_step, h_init, unroll=32)


def _mixer_kernel(n_prompt, dec_seq,
                  xp_hbm, xs_hbm, h0r_hbm, h0i_hbm, win_hbm, wout_hbm, wglu_hbm,
                  g1_ref, bdr_ref, bdi_ref, cdr_ref, cdi_ref, ar_ref, ai_ref, dskip_ref, bglu_ref,
                  lng_ref, lnb_ref, gs5_ref, gcm_ref, wmixp_ref, cmix_ref,
                  x1_ref, pr_ref, pi_ref, pv_ref, srt_ref, sit_ref, sv_ref,
                  xt_ref, in_sem, bur_ref, bui_ref, hsr_ref, hsi_ref, sr_ref, si_ref, cma_ref, cmb_ref, cmc_ref,
                  xs_ref, y0_ref, win_ref, wout_ref, wglu_ref, wtril_ref, w_sem, h0_sem):
    nb = SUBLANES
    rows = TILE_ROWS
    step = pl.program_id(0)
    slot = step % 2

    def fetch_prompt(s, sl):
        return [pltpu.make_async_copy(xp_hbm.at[b, pl.ds(s * CHUNK, CHUNK), :], xt_ref.at[sl, :, b, :],
                                      in_sem.at[sl, b]) for b in range(nb)]

    def fetch_sample(sl):
        return pltpu.make_async_copy(xs_hbm, xt_ref.at[sl], in_sem.at[sl, 0])

    @pl.when(step == 0)
    def _():
        for cp in fetch_prompt(0, 0):
            cp.start()
        h0_copies = [pltpu.make_async_copy(h0r_hbm, srt_ref, h0_sem.at[0]),
                     pltpu.make_async_copy(h0i_hbm, sit_ref, h0_sem.at[1])]
        for cp in h0_copies:
            cp.start()
        staging = [ref.at[k] for ref in (bur_ref, bui_ref, cma_ref, cmb_ref, cmc_ref)
                   for k in range(ref.shape[0])]
        _cast_weights([(win_hbm, win_ref), (wout_hbm, wout_ref), (wglu_hbm, wglu_ref)], staging, w_sem)
        for cp in h0_copies:
            cp.wait()
        sr_ref[...] = srt_ref[...].T
        si_ref[...] = sit_ref[...].T
        r_idx = lax.broadcasted_iota(jnp.int32, (2 * CHUNK, CHUNK), 0)
        c_idx = lax.broadcasted_iota(jnp.int32, (2 * CHUNK, CHUNK), 1)
        causal = (r_idx % CHUNK) >= c_idx
        for j in range(CM_HEADS // 2):
            wtril_ref[j] = jnp.where(causal, wmixp_ref[j], 0.0).astype(BF16)
        hsr_ref[...] = jnp.zeros_like(hsr_ref)
        hsi_ref[...] = jnp.zeros_like(hsi_ref)

    @pl.when(step + 1 < n_prompt)
    def _():
        for cp in fetch_prompt(step + 1, 1 - slot):
            cp.start()

    @pl.when(step + 1 == n_prompt)
    def _():
        fetch_sample(1 - slot).start()

    @pl.when(step < n_prompt)
    def _():
        for cp in fetch_prompt(step, slot):
            cp.wait()

    @pl.when(step == n_prompt)
    def _():
        fetch_sample(slot).wait()

    is_prompt = step < n_prompt
    is_sample = step == n_prompt
    cm_blocks = CM_WIDTH // LANES
    mxu_blocks = MXU_COLS // LANES
    us_ref, vs_ref, mx_ref = cma_ref, cmb_ref, cmc_ref
    def row_blocks(n):
        return [slice(r * n, (r + 1) * n) for r in range(rows // n)]

    def x_rows(rsl):
        groups = slice(rsl.start // nb, rsl.stop // nb)
        return xt_ref[slot, groups].reshape(rsl.stop - rsl.start, D_MODEL)

    def b_proj(hf, xs_h, rsl):
        for n in range(HALF_STATES // MXU_COLS):
            cols = slice(n * MXU_COLS, (n + 1) * MXU_COLS)
            b_r = _dot(xs_h, bdr_ref[hf, :, cols])
            b_i = _dot(xs_h, bdi_ref[hf, :, cols])
            for q in range(mxu_blocks):
                bur_ref[n * mxu_blocks + q, rsl, :] = b_r[:, q * LANES:(q + 1) * LANES]
                bui_ref[n * mxu_blocks + q, rsl, :] = b_i[:, q * LANES:(q + 1) * LANES]

    def c_proj(hf, rsl):
        y_h = None
        for n in range(HALF_STATES // MXU_COLS):
            krows = slice(n * MXU_COLS, (n + 1) * MXU_COLS)
            blocks = range(n * mxu_blocks, (n + 1) * mxu_blocks)
            h_r = jnp.concatenate([bur_ref[k, rsl, :] for k in blocks], axis=1).astype(BF16)
            h_i = jnp.concatenate([bui_ref[k, rsl, :] for k in blocks], axis=1).astype(BF16)
            part = _dot(h_r, cdr_ref[hf, krows, :]) - _dot(h_i, cdi_ref[hf, krows, :])
            y_h = part if y_h is None else y_h + part
        return y_h

    def scan_half(hf):
        @pl.when(is_prompt)
        def _():
            for lc in range(HALF_STATES // SCAN_LANES):
                glanes = slice(hf * HALF_STATES + lc * SCAN_LANES, hf * HALF_STATES + (lc + 1) * SCAN_LANES)
                hr, hi = _scan_lanes(bur_ref, bui_ref, ar_ref, ai_ref, hf, lc,
                                     (hsr_ref[:, glanes], hsi_ref[:, glanes]),
                                     lambda t: pl.ds(pl.multiple_of(t * nb, nb), nb), CHUNK, False)
                hsr_ref[:, glanes] = hr
                hsi_ref[:, glanes] = hi
                pr_ref[:, glanes] = hr
                pi_ref[:, glanes] = hi

        @pl.when(is_sample)
        def _():
            tt = dec_seq

            def scan_group(sg, carry):
                srows = pl.ds(pl.multiple_of(sg * SUBLANES, SUBLANES), SUBLANES)
                for lc in range(HALF_STATES // SCAN_LANES):
                    glanes = slice(hf * HALF_STATES + lc * SCAN_LANES,
                                   hf * HALF_STATES + (lc + 1) * SCAN_LANES)
                    hr, hi = _scan_lanes(bur_ref, bui_ref, ar_ref, ai_ref, hf, lc,
                                         (sr_ref[srows, glanes], si_ref[srows, glanes]),
                                         lambda t: pl.ds(sg * (SUBLANES * tt) + t, SUBLANES, stride=tt),
                                         tt, True)
                    sr_ref[srows, glanes] = hr
                    si_ref[srows, glanes] = hi
                return carry

            lax.fori_loop(0, rows // tt // SUBLANES, scan_group, 0)
            if hf == S5_HALVES - 1:
                srt_ref[...] = sr_ref[...].T
                sit_ref[...] = si_ref[...].T

    for rsl in row_blocks(IN_ROWS):
        h = _rms(x_rows(rsl), g1_ref[...]).astype(BF16)
        u, v = _cm_uv(_dot(h, win_ref[:, S5_WIDTH:]), lng_ref, lnb_ref)
        for k in range(cm_blocks):
            us_ref[k, rsl, :] = u[:, k * LANES:(k + 1) * LANES]
            vs_ref[k, rsl, :] = v[:, k * LANES:(k + 1) * LANES]
        xs = _dot(h, win_ref[:, :S5_WIDTH])
        xs_ref[rsl, :] = xs
        b_proj(0, xs[:, :HALF_IN].astype(BF16), rsl)

    first_head = lax.broadcasted_iota(jnp.int32, (CHUNK, LANES), 1) < CM_HEAD_DIM
    for j in range(CM_HEADS // 2):
        w_pair = wtril_ref[j]
        bias = cmix_ref[j, :CHUNK, :]
        for b in range(nb):
            seq_rows = pl.ds(b, CHUNK, stride=nb)
            r = _dot(w_pair, vs_ref[j, seq_rows, :].astype(BF16))
            mixed = jnp.where(first_head, r[:CHUNK], r[CHUNK:]) + bias
            mx_ref[j, seq_rows, :] = us_ref[j, seq_rows, :] * mixed

    @pl.when(step == n_prompt - 1)
    def _():
        for j in range(cm_blocks):
            for b in range(nb):
                pv_ref[b, :, j * LANES:(j + 1) * LANES] = vs_ref[j, pl.ds(b, CHUNK, stride=nb), :]

    @pl.when(is_sample)
    def _():
        tt = dec_seq
        n_seq = rows // tt
        sv_ref[...] = _load_lane_blocks(vs_ref, cm_blocks).reshape(n_seq, tt, CM_WIDTH)
        def mix_block(k, carry):
            v_at = [vs_ref[k, pl.ds(s, n_seq, stride=tt), :] for s in range(tt)]
            for t in range(tt):
                w_row = CHUNK + t * tt
                acc = cmix_ref[k, t:t + 1, :] + cmix_ref[k, w_row:w_row + 1, :] * v_at[0]
                for s in range(1, t + 1):
                    acc = acc + cmix_ref[k, w_row + s:w_row + s + 1, :] * v_at[s]
                trows = pl.ds(t, n_seq, stride=tt)
                mx_ref[k, trows, :] = us_ref[k, trows, :] * acc
            return carry

        lax.fori_loop(0, cm_blocks, mix_block, 0)

    scan_half(0)
    for rsl in row_blocks(MID_ROWS):
        y0_ref[rsl, :] = c_proj(0, rsl)
        b_proj(1, xs_ref[rsl, HALF_IN:].astype(BF16), rsl)
        o_cm = jnp.concatenate([mx_ref[k, rsl, :] for k in range(cm_blocks)], axis=1)
        n_cm = _rms(o_cm, gcm_ref[...]).astype(BF16)
        x1_ref[rsl, :] = x_rows(rsl) + _dot(n_cm, wout_ref[S5_WIDTH:, :])
    scan_half(1)
    for rsl in row_blocks(GATE_ROWS):
        y = jnp.concatenate([y0_ref[rsl, :], c_proj(1, rsl)], axis=1) + dskip_ref[...] * xs_ref[rsl, :]
        g = jax.nn.gelu(y)
        o_s5 = g * jax.nn.sigmoid(_dot(g.astype(BF16), wglu_ref[...]) + bglu_ref[...])
        n_s5 = _rms(o_s5, gs5_ref[...]).astype(BF16)
        x1_ref[rsl, :] = x1_ref[rsl, :] + _dot(n_s5, wout_ref[:S5_WIDTH, :])


def _const_spec(shape):
    nd = len(shape)
    return pl.BlockSpec(shape, lambda i, nd=nd: (0,) * nd, pipeline_mode=pl.Buffered(1))


_ANY_SPEC = pl.BlockSpec(memory_space=pl.ANY)


def _mixer(x_prompt, x_sample, h0, big_w, small_w):
    nb, seq, _ = x_prompt.shape
    dec_batch, dec_seq, _ = x_sample.shape
    assert nb == SUBLANES and seq % CHUNK == 0
    assert dec_batch * dec_seq == TILE_ROWS and dec_seq <= SUBLANES and dec_batch == CHUNK
    n_prompt = seq // CHUNK
    w_in, w_out, w_glu = big_w
    in_specs = [_ANY_SPEC] * 7 + [_const_spec(a.shape) for a in small_w]
    out_shape = (
        jax.ShapeDtypeStruct(((n_prompt + 1) * TILE_ROWS, D_MODEL), F32),
        jax.ShapeDtypeStruct((nb, N_STATES), F32),
        jax.ShapeDtypeStruct((nb, N_STATES), F32),
        jax.ShapeDtypeStruct((nb, CHUNK, CM_WIDTH), F32),
        jax.ShapeDtypeStruct((N_STATES, dec_batch), F32),
        jax.ShapeDtypeStruct((N_STATES, dec_batch), F32),
        jax.ShapeDtypeStruct((dec_batch, dec_seq, CM_WIDTH), F32),
    )
    out_specs = (pl.BlockSpec((TILE_ROWS, D_MODEL), lambda i: (i, 0)),) + tuple(
        _const_spec(o.shape) for o in out_shape[1:])
    scratch = [
        pltpu.VMEM((2, CHUNK, nb, D_MODEL), F32),
        pltpu.SemaphoreType.DMA((2, nb)),
        pltpu.VMEM((HALF_STATES // LANES, BLOCK_ROWS, LANES), F32),
        pltpu.VMEM((HALF_STATES // LANES, BLOCK_ROWS, LANES), F32),
        pltpu.VMEM((nb, N_STATES), F32),
        pltpu.VMEM((nb, N_STATES), F32),
        pltpu.VMEM((dec_batch, N_STATES), F32),
        pltpu.VMEM((dec_batch, N_STATES), F32),
        pltpu.VMEM((CM_WIDTH // LANES, BLOCK_ROWS, LANES), F32),
        pltpu.VMEM((CM_WIDTH // LANES, BLOCK_ROWS, LANES), F32),
        pltpu.VMEM((CM_WIDTH // LANES, BLOCK_ROWS, LANES), F32),
        pltpu.VMEM((TILE_ROWS, S5_WIDTH), F32),
        pltpu.VMEM((TILE_ROWS, HALF_IN), F32),
        pltpu.VMEM(w_in.shape, BF16),
        pltpu.VMEM(w_out.shape, BF16),
        pltpu.VMEM(w_glu.shape, BF16),
        pltpu.VMEM((CM_HEADS // 2, 2 * CHUNK, CHUNK), BF16),
        pltpu.SemaphoreType.DMA(((w_in.shape[1] + w_out.shape[1] + w_glu.shape[1]) // LANES,)),
        pltpu.SemaphoreType.DMA((2,)),
    ]
    return pl.pallas_call(
        functools.partial(_mixer_kernel, n_prompt, dec_seq),
        grid=(n_prompt + 1,),
        in_specs=in_specs,
        out_specs=out_specs,
        out_shape=out_shape,
        scratch_shapes=scratch,
        compiler_params=pltpu.CompilerParams(
            dimension_semantics=("arbitrary",), vmem_limit_bytes=VMEM_LIMIT_BYTES),
        name="mixer",
    )(x_prompt, x_sample, *h0, w_in, w_out, w_glu, *small_w)


def _ffn_kernel(nb, n_prompt, x_ref, g2_ref, gf_ref, wg_hbm, wu_hbm, wd_hbm, yp_hbm, ys_hbm,
                yt_ref, out_sem, wg_ref, wu_ref, wd_ref, stage_gu_ref, stage_d_ref, w_sem):
    tt = TILE_ROWS // nb
    step = pl.program_id(0)
    n_steps = pl.num_programs(0)
    slot = step % 2
    n_chunks = D_FF // FF_CHUNK

    def prompt_writeback(s, sl):
        return [pltpu.make_async_copy(yt_ref.at[sl, :, b, :], yp_hbm.at[b, pl.ds(s * tt, tt), :],
                                      out_sem.at[sl, b]) for b in range(nb)]

    def sample_writeback(sl):
        return [pltpu.make_async_copy(yt_ref.at[sl], ys_hbm, out_sem.at[sl, 0])]

    def wait_writeback(s, sl):
        @pl.when(s < n_prompt)
        def _():
            for cp in prompt_writeback(s, sl):
                cp.wait()

        @pl.when(s >= n_prompt)
        def _():
            for cp in sample_writeback(sl):
                cp.wait()

    def weight_chunk_copies(c, sl):
        cols = pl.ds(c * FF_CHUNK, FF_CHUNK)
        return [pltpu.make_async_copy(wg_hbm.at[:, cols], stage_gu_ref.at[sl, 0], w_sem.at[sl, 0]),
                pltpu.make_async_copy(wu_hbm.at[:, cols], stage_gu_ref.at[sl, 1], w_sem.at[sl, 1]),
                pltpu.make_async_copy(wd_hbm.at[cols, :], stage_d_ref.at[sl], w_sem.at[sl, 2])]

    def stage_in_chunk(c):
        sl = c % FF_STAGE_SLOTS
        cols = slice(c * FF_CHUNK, (c + 1) * FF_CHUNK)
        for cp in weight_chunk_copies(c, sl):
            cp.wait()
        wg_ref[:, cols] = stage_gu_ref[sl, 0].astype(BF16)
        wu_ref[:, cols] = stage_gu_ref[sl, 1].astype(BF16)
        wd_ref[cols, :] = stage_d_ref[sl].astype(BF16)
        if c + FF_STAGE_SLOTS < n_chunks:
            for cp in weight_chunk_copies(c + FF_STAGE_SLOTS, sl):
                cp.start()

    @pl.when(step == 0)
    def _():
        for c in range(FF_STAGE_SLOTS):
            for cp in weight_chunk_copies(c, c):
                cp.start()
        for c in range(n_chunks):
            stage_in_chunk(c)

    @pl.when(step >= 2)
    def _():
        wait_writeback(step - 2, slot)

    halves = [slice(r, r + FFN_ROWS) for r in range(0, TILE_ROWS, FFN_ROWS)]
    hs = [_rms(x_ref[rsl, :], g2_ref[...]).astype(BF16) for rsl in halves]
    accs = [x_ref[rsl, :] for rsl in halves]
    for c in range(n_chunks):
        cols = slice(c * FF_CHUNK, (c + 1) * FF_CHUNK)
        for i in range(len(halves)):
            gate = _dot(hs[i], wg_ref[:, cols])
            up = _dot(hs[i], wu_ref[:, cols])
            accs[i] = accs[i] + _dot((jax.nn.silu(gate) * up).astype(BF16), wd_ref[cols, :])
    for i, rsl in enumerate(halves):
        groups = slice(rsl.start // nb, rsl.stop // nb)
        yt_ref[slot, groups] = _rms(accs[i], gf_ref[...]).reshape(groups.stop - groups.start, nb, D_MODEL)

    @pl.when(step < n_prompt)
    def _():
        for cp in prompt_writeback(step, slot):
            cp.start()

    @pl.when(step >= n_prompt)
    def _():
        for cp in sample_writeback(slot):
            cp.start()

    @pl.when(step == n_steps - 1)
    def _():
        wait_writeback(step - 1, 1 - slot)
        wait_writeback(step, slot)


def _ffn(x1, g2, gf, ffn_w, *, nb, seq):
    n = x1.shape[0]
    n_prompt = nb * seq // TILE_ROWS
    assert n == (n_prompt + 1) * TILE_ROWS and TILE_ROWS % nb == 0 and n_prompt >= 1
    tt = TILE_ROWS // nb
    w_gate, w_up, w_down = ffn_w
    return pl.pallas_call(
        functools.partial(_ffn_kernel, nb, n_prompt),
        grid=(n_prompt + 1,),
        in_specs=[pl.BlockSpec((TILE_ROWS, D_MODEL), lambda i: (i, 0)),
                  _const_spec(g2.shape), _const_spec(gf.shape)] + [_ANY_SPEC] * 3,
        out_specs=(_ANY_SPEC, _ANY_SPEC),
        out_shape=(jax.ShapeDtypeStruct((nb, seq, D_MODEL), F32),
                   jax.ShapeDtypeStruct((tt, nb, D_MODEL), F32)),
        scratch_shapes=[pltpu.VMEM((2, tt, nb, D_MODEL), F32),
                        pltpu.SemaphoreType.DMA((2, nb)),
                        pltpu.VMEM(w_gate.shape, BF16),
                        pltpu.VMEM(w_up.shape, BF16),
                        pltpu.VMEM(w_down.shape, BF16),
                        pltpu.VMEM((FF_STAGE_SLOTS, 2, D_MODEL, FF_CHUNK), F32),
                        pltpu.VMEM((FF_STAGE_SLOTS, FF_CHUNK, D_MODEL), F32),
                        pltpu.SemaphoreType.DMA((FF_STAGE_SLOTS, 3))],
        compiler_params=pltpu.CompilerParams(
            dimension_semantics=("arbitrary",), vmem_limit_bytes=VMEM_LIMIT_BYTES),
        name="ffn",
    )(x1, g2, gf, w_gate, w_up, w_down)


def kernel(x_prompt, x_sample, state_s5_re, state_s5_im, norm1, w_in, lam_re, lam_im, log_dt, b_re, b_im, c_re, c_im, d_skip, w_glu, b_glu, cm_ln_g, cm_ln_b, w_s, b_s, g_s5, g_cm, w_out, norm2, w_gate, w_up, w_down, norm_f):
    depth = norm1.shape[0]
    assert depth == 1
    l = 0
    batch, seq, _ = x_prompt.shape
    dec_batch, dec_seq, _ = x_sample.shape

    ar, ai, bdr, bdi, cdr, cdi = pl.pallas_call(
        _prep_kernel,
        out_shape=(
            jax.ShapeDtypeStruct((1, N_STATES), F32),
            jax.ShapeDtypeStruct((1, N_STATES), F32),
            jax.ShapeDtypeStruct((S5_HALVES, HALF_IN, HALF_STATES), BF16),
            jax.ShapeDtypeStruct((S5_HALVES, HALF_IN, HALF_STATES), BF16),
            jax.ShapeDtypeStruct((S5_HALVES, HALF_STATES, HALF_IN), BF16),
            jax.ShapeDtypeStruct((S5_HALVES, HALF_STATES, HALF_IN), BF16),
        ),
        name="s5_prepare",
    )(lam_re[l], lam_im[l], log_dt[l:l + 1], jnp.swapaxes(b_re, 2, 3), jnp.swapaxes(b_im, 2, 3),
      c_re[l].reshape(S5_WIDTH, S5_STATE), c_im[l].reshape(S5_WIDTH, S5_STATE))

    row = lambda a: a.reshape(1, -1)
    wmix_p = w_s[l].reshape(CM_HEADS // 2, 2 * CHUNK, CHUNK)
    per_head = jnp.concatenate([b_s[l].T, w_s[l][:, :dec_seq, :dec_seq].reshape(CM_HEADS, -1).T], axis=0)
    cmix = jnp.repeat(per_head, CM_HEAD_DIM, axis=1)
    cmix = jnp.moveaxis(cmix.reshape(-1, CM_WIDTH // LANES, LANES), 1, 0)
    small_w = [row(norm1[l]), bdr, bdi, cdr, cdi, ar, ai, row(d_skip[l]), row(b_glu[l]),
               row(cm_ln_g[l]), row(cm_ln_b[l]), row(g_s5[l]), row(g_cm[l]), wmix_p, cmix]
    to_state_major = lambda a: jnp.transpose(a, (1, 2, 0)).reshape(N_STATES, dec_batch)
    from_state_major = lambda a: jnp.transpose(a.reshape(S5_GROUPS, S5_STATE, dec_batch), (2, 0, 1))[None]
    h0 = (to_state_major(state_s5_re[l]), to_state_major(state_s5_im[l]))

    x1, pr, pi_, pv, sr, si, sv = _mixer(x_prompt, x_sample, h0, (w_in[l], w_out[l], w_glu[l]), small_w)
    y_prompt, y_sample = _ffn(x1, row(norm2[l]), row(norm_f), (w_gate[l], w_up[l], w_down[l]),
                              nb=batch, seq=seq)

    st = lambda a, n: a.reshape(1, n, S5_GROUPS, S5_STATE)
    return (y_prompt, y_sample.reshape(dec_batch, dec_seq, D_MODEL),
            st(pr, batch), st(pi_, batch), pv[None],
            from_state_major(sr), from_state_major(si), sv[None])
```

```python
import functools

import jax
import jax.numpy as jnp
from jax import lax
from jax.experimental import pallas as pl
from jax.experimental.pallas import tpu as pltpu

F32 = jnp.float32
BF16 = jnp.bfloat16

D_MODEL = 1024
S5_WIDTH = 512
S5_GROUP = 16
S5_GROUPS = 32
S5_STATE = 64
N_STATES = S5_GROUPS * S5_STATE
CM_WIDTH = 512
CM_HEADS = 8
CM_HEAD_DIM = 64
CHUNK = 128
D_FF = 2816
EPS = 1e-6

S5_HALVES = 2
HALF_GROUPS = S5_GROUPS // S5_HALVES
HALF_IN = S5_WIDTH // S5_HALVES
HALF_STATES = N_STATES // S5_HALVES
SUBLANES = 8
LANES = 128
SCAN_LANES = 1024
SCAN_BLOCKS = SCAN_LANES // LANES
MXU_COLS = 256
TILE_ROWS = SUBLANES * CHUNK
BLOCK_ROWS = TILE_ROWS + SUBLANES
IN_ROWS = 256
MID_ROWS = 1024
GATE_ROWS = 1024
FF_CHUNK = 256
FFN_ROWS = 512
FF_STAGE_SLOTS = 4
VMEM_LIMIT_BYTES = 58 * 1024 * 1024


def _rms(x, g):
    r = lax.rsqrt(jnp.mean(x * x, axis=-1, keepdims=True) + EPS)
    return x * r * g


def _dot(a, b):
    return jnp.dot(a, b, preferred_element_type=F32)


def _load_lane_blocks(ref, n_blocks):
    return jnp.concatenate([ref[k, :TILE_ROWS, :] for k in range(n_blocks)], axis=1)


def _cast_weights(weights, slots, sems, start_after=()):
    plan = []
    for w_hbm, w_bf_ref in weights:
        n_rows, n_cols = w_hbm.shape
        for k in range(n_cols // LANES):
            i = len(plan)
            stage = slots[i].at[pl.ds(0, n_rows), :]
            copy = pltpu.make_async_copy(w_hbm.at[:, pl.ds(k * LANES, LANES)], stage, sems.at[i])
            plan.append((copy, stage, w_bf_ref, k))
    for copy, _, _, _ in plan:
        copy.start()
    for copy in start_after:
        copy.start()
    for copy, stage, w_bf_ref, k in plan:
        copy.wait()
        w_bf_ref[:, k * LANES:(k + 1) * LANES] = stage[...].astype(BF16)


def _prep_kernel(lr_ref, li_ref, ldt_ref, br_ref, bi_ref, cr_ref, ci_ref,
                 ar_ref, ai_ref, bdr_ref, bdi_ref, cdr_ref, cdi_ref):
    dt = jnp.exp(ldt_ref[...])
    lr = lr_ref[...]
    li = li_ref[...]
    mag = jnp.exp(lr * dt)
    ar = mag * jnp.cos(li * dt)
    ai = mag * jnp.sin(li * dt)
    den = lr * lr + li * li
    qr = ((ar - 1.0) * lr + ai * li) / den
    qi = (ai * lr - (ar - 1.0) * li) / den
    for g in range(S5_GROUPS):
        ar_ref[:, g * S5_STATE:(g + 1) * S5_STATE] = ar[g:g + 1, :]
        ai_ref[:, g * S5_STATE:(g + 1) * S5_STATE] = ai[g:g + 1, :]

    def iota(shape, dim):
        return lax.broadcasted_iota(jnp.int32, shape, dim)

    rep = (iota((S5_STATE, HALF_STATES), 1) % S5_STATE == iota((S5_STATE, HALF_STATES), 0)).astype(BF16)

    def repeat_lanes(x, rep):
        hi = x.astype(BF16)
        rest = x - hi.astype(F32)
        mid = rest.astype(BF16)
        lo = (rest - mid.astype(F32)).astype(BF16)
        return _dot(hi, rep) + _dot(mid, rep) + _dot(lo, rep)

    diag = iota((HALF_IN, HALF_STATES), 0) // S5_GROUP == iota((HALF_IN, HALF_STATES), 1) // S5_STATE
    for hv in range(S5_HALVES):
        groups = range(hv * HALF_GROUPS, (hv + 1) * HALF_GROUPS)
        bb_r = jnp.concatenate([qr[g:g + 1, :] * br_ref[0, g] - qi[g:g + 1, :] * bi_ref[0, g]
                                for g in groups], axis=0)
        bb_i = jnp.concatenate([qr[g:g + 1, :] * bi_ref[0, g] + qi[g:g + 1, :] * br_ref[0, g]
                                for g in groups], axis=0)
        for bb, o_ref in ((bb_r, bdr_ref), (bb_i, bdi_ref)):
            o_ref[hv] = jnp.where(diag, repeat_lanes(bb, rep), 0.0).astype(BF16)
        for c_ref, o_ref in ((cr_ref, cdr_ref), (ci_ref, cdi_ref)):
            c_half = c_ref[hv * HALF_IN:(hv + 1) * HALF_IN, :]
            o_ref[hv] = jnp.where(diag, repeat_lanes(c_half, rep), 0.0).T.astype(BF16)


def _cm_uv(p_cm, lng_ref, lnb_ref):
    uv = jax.nn.gelu(p_cm)
    u = uv[:, :CM_WIDTH]
    vv = uv[:, CM_WIDTH:]
    mu = jnp.mean(vv, axis=-1, keepdims=True)
    vc = vv - mu
    v = vc * lax.rsqrt(jnp.mean(vc * vc, axis=-1, keepdims=True) + EPS) * lng_ref[...] + lnb_ref[...]
    return u, v


def _scan_lanes(bur_ref, bui_ref, ar_ref, ai_ref, hf, lc, h_init, rows_at, n_steps, unrolled):
    blocks = range(lc * SCAN_BLOCKS, (lc + 1) * SCAN_BLOCKS)
    glanes = slice(hf * HALF_STATES + lc * SCAN_LANES, hf * HALF_STATES + (lc + 1) * SCAN_LANES)
    a_r = jnp.broadcast_to(ar_ref[:, glanes], (SUBLANES, SCAN_LANES))
    a_i = jnp.broadcast_to(ai_ref[:, glanes], (SUBLANES, SCAN_LANES))

    def scan_step(t, carry):
        hr, hi = carry
        trows = rows_at(t)
        b_r = jnp.concatenate([bur_ref[k, trows, :] for k in blocks], axis=1)
        b_i = jnp.concatenate([bui_ref[k, trows, :] for k in blocks], axis=1)
        nr = a_r * hr - a_i * hi + b_r
        ni = a_r * hi + a_i * hr + b_i
        for n, k in enumerate(blocks):
            bur_ref[k, trows, :] = nr[:, n * LANES:(n + 1) * LANES]
            bui_ref[k, trows, :] = ni[:, n * LANES:(n + 1) * LANES]
        return nr, ni

    if unrolled:
        carry = h_init
        for t in range(n_steps):
            carry = scan_step(t, carry)
        return carry
    return lax.fori_loop(0, n_steps, scan_step, h_init, unroll=32)


def _mixer_kernel(n_prompt, dec_seq,
                  xp_hbm, xs_hbm, h0r_hbm, h0i_hbm, win_hbm, wout_hbm, wglu_hbm,
                  g1_ref, bdr_ref, bdi_ref, cdr_ref, cdi_ref, ar_ref, ai_ref, dskip_ref, bglu_ref,
                  lng_ref, lnb_ref, gs5_ref, gcm_ref, wmixp_ref, cmix_ref,
                  x1_ref, pr_ref, pi_ref, pv_ref, srt_ref, sit_ref, sv_ref,
                  xt_ref, in_sem, bur_ref, bui_ref, hsr_ref, hsi_ref, sr_ref, si_ref, cma_ref, cmb_ref, cmc_ref,
                  xs_ref, y0_ref, win_ref, wout_ref, wglu_ref, wtril_ref, w_sem, h0_sem):
    nb = SUBLANES
    rows = TILE_ROWS
    step = pl.program_id(0)
    slot = step % 2

    def fetch_prompt(s, sl):
        return [pltpu.make_async_copy(xp_hbm.at[b, pl.ds(s * CHUNK, CHUNK), :], xt_ref.at[sl, :, b, :],
                                      in_sem.at[sl, b]) for b in range(nb)]

    def fetch_sample(sl):
        return pltpu.make_async_copy(xs_hbm, xt_ref.at[sl], in_sem.at[sl, 0])

    @pl.when(step == 0)
    def _():
        h0_copies = [pltpu.make_async_copy(h0r_hbm, srt_ref, h0_sem.at[0]),
                     pltpu.make_async_copy(h0i_hbm, sit_ref, h0_sem.at[1])]
        staging = [ref.at[k] for ref in (bur_ref, bui_ref, cma_ref, cmb_ref, cmc_ref)
                   for k in range(ref.shape[0])]
        _cast_weights([(win_hbm, win_ref), (wout_hbm, wout_ref), (wglu_hbm, wglu_ref)], staging, w_sem,
                      start_after=h0_copies + fetch_prompt(0, 0))
        for cp in h0_copies:
            cp.wait()
        sr_ref[...] = srt_ref[...].T
        si_ref[...] = sit_ref[...].T
        r_idx = lax.broadcasted_iota(jnp.int32, (2 * CHUNK, CHUNK), 0)
        c_idx = lax.broadcasted_iota(jnp.int32, (2 * CHUNK, CHUNK), 1)
        causal = (r_idx % CHUNK) >= c_idx
        for j in range(CM_HEADS // 2):
            wtril_ref[j] = jnp.where(causal, wmixp_ref[j], 0.0).astype(BF16)
        hsr_ref[...] = jnp.zeros_like(hsr_ref)
        hsi_ref[...] = jnp.zeros_like(hsi_ref)

    @pl.when(step + 1 < n_prompt)
    def _():
        for cp in fetch_prompt(step + 1, 1 - slot):
            cp.start()

    @pl.when(step + 1 == n_prompt)
    def _():
        fetch_sample(1 - slot).start()

    @pl.when(step < n_prompt)
    def _():
        for cp in fetch_prompt(step, slot):
            cp.wait()

    @pl.when(step == n_prompt)
    def _():
        fetch_sample(slot).wait()

    is_prompt = step < n_prompt
    is_sample = step == n_prompt
    cm_blocks = CM_WIDTH // LANES
    mxu_blocks = MXU_COLS // LANES
    us_ref, vs_ref, mx_ref = cma_ref, cmb_ref, cmc_ref
    def row_blocks(n):
        return [slice(r * n, (r + 1) * n) for r in range(rows // n)]

    def x_rows(rsl):
        groups = slice(rsl.start // nb, rsl.stop // nb)
        return xt_ref[slot, groups].reshape(rsl.stop - rsl.start, D_MODEL)

    def b_proj(hf, xs_h, rsl):
        for n in range(HALF_STATES // MXU_COLS):
            cols = slice(n * MXU_COLS, (n + 1) * MXU_COLS)
            b_r = _dot(xs_h, bdr_ref[hf, :, cols])
            b_i = _dot(xs_h, bdi_ref[hf, :, cols])
            for q in range(mxu_blocks):
                bur_ref[n * mxu_blocks + q, rsl, :] = b_r[:, q * LANES:(q + 1) * LANES]
                bui_ref[n * mxu_blocks + q, rsl, :] = b_i[:, q * LANES:(q + 1) * LANES]

    def c_proj(hf, rsl):
        y_h = None
        for n in range(HALF_STATES // MXU_COLS):
            krows = slice(n * MXU_COLS, (n + 1) * MXU_COLS)
            blocks = range(n * mxu_blocks, (n + 1) * mxu_blocks)
            h_r = jnp.concatenate([bur_ref[k, rsl, :] for k in blocks], axis=1).astype(BF16)
            h_i = jnp.concatenate([bui_ref[k, rsl, :] for k in blocks], axis=1).astype(BF16)
            part = _dot(h_r, cdr_ref[hf, krows, :]) - _dot(h_i, cdi_ref[hf, krows, :])
            y_h = part if y_h is None else y_h + part
        return y_h

    def scan_half(hf):
        @pl.when(is_prompt)
        def _():
            for lc in range(HALF_STATES // SCAN_LANES):
                glanes = slice(hf * HALF_STATES + lc * SCAN_LANES, hf * HALF_STATES + (lc + 1) * SCAN_LANES)
                hr, hi = _scan_lanes(bur_ref, bui_ref, ar_ref, ai_ref, hf, lc,
                                     (hsr_ref[:, glanes], hsi_ref[:, glanes]),
                                     lambda t: pl.ds(pl.multiple_of(t * nb, nb), nb), CHUNK, False)
                hsr_ref[:, glanes] = hr
                hsi_ref[:, glanes] = hi
                pr_ref[:, glanes] = hr
                pi_ref[:, glanes] = hi

        @pl.when(is_sample)
        def _():
            tt = dec_seq

            def scan_group(sg, carry):
                srows = pl.ds(pl.multiple_of(sg * SUBLANES, SUBLANES), SUBLANES)
                for lc in range(HALF_STATES // SCAN_LANES):
                    glanes = slice(hf * HALF_STATES + lc * SCAN_LANES,
                                   hf * HALF_STATES + (lc + 1) * SCAN_LANES)
                    hr, hi = _scan_lanes(bur_ref, bui_ref, ar_ref, ai_ref, hf, lc,
                                         (sr_ref[srows, glanes], si_ref[srows, glanes]),
                                         lambda t: pl.ds(sg * (SUBLANES * tt) + t, SUBLANES, stride=tt),
                                         tt, True)
                    sr_ref[srows, glanes] = hr
                    si_ref[srows, glanes] = hi
                return carry

            lax.fori_loop(0, rows // tt // SUBLANES, scan_group, 0)
            if hf == S5_HALVES - 1:
                srt_ref[...] = sr_ref[...].T
                sit_ref[...] = si_ref[...].T

    for rsl in row_blocks(IN_ROWS):
        h = _rms(x_rows(rsl), g1_ref[...]).astype(BF16)
        u, v = _cm_uv(_dot(h, win_ref[:, S5_WIDTH:]), lng_ref, lnb_ref)
        for k in range(cm_blocks):
            us_ref[k, rsl, :] = u[:, k * LANES:(k + 1) * LANES]
            vs_ref[k, rsl, :] = v[:, k * LANES:(k + 1) * LANES]
        xs = _dot(h, win_ref[:, :S5_WIDTH])
        xs_ref[rsl, :] = xs
        b_proj(0, xs[:, :HALF_IN].astype(BF16), rsl)

    first_head = lax.broadcasted_iota(jnp.int32, (CHUNK, LANES), 1) < CM_HEAD_DIM
    for j in range(CM_HEADS // 2):
        w_pair = wtril_ref[j]
        bias = cmix_ref[j, :CHUNK, :]
        for b in range(nb):
            seq_rows = pl.ds(b, CHUNK, stride=nb)
            r = _dot(w_pair, vs_ref[j, seq_rows, :].astype(BF16))
            mixed = jnp.where(first_head, r[:CHUNK], r[CHUNK:]) + bias
            mx_ref[j, seq_rows, :] = us_ref[j, seq_rows, :] * mixed

    @pl.when(step == n_prompt - 1)
    def _():
        for j in range(cm_blocks):
            for b in range(nb):
                pv_ref[b, :, j * LANES:(j + 1) * LANES] = vs_ref[j, pl.ds(b, CHUNK, stride=nb), :]

    @pl.when(is_sample)
    def _():
        tt = dec_seq
        n_seq = rows // tt
        sv_ref[...] = _load_lane_blocks(vs_ref, cm_blocks).reshape(n_seq, tt, CM_WIDTH)
        def mix_block(k, carry):
            v_at = [vs_ref[k, pl.ds(s, n_seq, stride=tt), :] for s in range(tt)]
            for t in range(tt):
                w_row = CHUNK + t * tt
                acc = cmix_ref[k, t:t + 1, :] + cmix_ref[k, w_row:w_row + 1, :] * v_at[0]
                for s in range(1, t + 1):
                    acc = acc + cmix_ref[k, w_row + s:w_row + s + 1, :] * v_at[s]
                trows = pl.ds(t, n_seq, stride=tt)
                mx_ref[k, trows, :] = us_ref[k, trows, :] * acc
            return carry

        lax.fori_loop(0, cm_blocks, mix_block, 0)

    scan_half(0)
    for rsl in row_blocks(MID_ROWS):
        y0_ref[rsl, :] = c_proj(0, rsl)
        b_proj(1, xs_ref[rsl, HALF_IN:].astype(BF16), rsl)
        o_cm = jnp.concatenate([mx_ref[k, rsl, :] for k in range(cm_blocks)], axis=1)
        n_cm = _rms(o_cm, gcm_ref[...]).astype(BF16)
        x1_ref[rsl, :] = x_rows(rsl) + _dot(n_cm, wout_ref[S5_WIDTH:, :])
    scan_half(1)
    for rsl in row_blocks(GATE_ROWS):
        y = jnp.concatenate([y0_ref[rsl, :], c_proj(1, rsl)], axis=1) + dskip_ref[...] * xs_ref[rsl, :]
        g = jax.nn.gelu(y)
        o_s5 = g * jax.nn.sigmoid(_dot(g.astype(BF16), wglu_ref[...]) + bglu_ref[...])
        n_s5 = _rms(o_s5, gs5_ref[...]).astype(BF16)
        x1_ref[rsl, :] = x1_ref[rsl, :] + _dot(n_s5, wout_ref[:S5_WIDTH, :])


def _const_spec(shape):
    nd = len(shape)
    return pl.BlockSpec(shape, lambda i, nd=nd: (0,) * nd, pipeline_mode=pl.Buffered(1))


_ANY_SPEC = pl.BlockSpec(memory_space=pl.ANY)


def _mixer(x_prompt, x_sample, h0, big_w, small_w):
    nb, seq, _ = x_prompt.shape
    dec_batch, dec_seq, _ = x_sample.shape
    assert nb == SUBLANES and seq % CHUNK == 0
    assert dec_batch * dec_seq == TILE_ROWS and dec_seq <= SUBLANES and dec_batch == CHUNK
    n_prompt = seq // CHUNK
    w_in, w_out, w_glu = big_w
    in_specs = [_ANY_SPEC] * 7 + [_const_spec(a.shape) for a in small_w]
    out_shape = (
        jax.ShapeDtypeStruct(((n_prompt + 1) * TILE_ROWS, D_MODEL), F32),
        jax.ShapeDtypeStruct((nb, N_STATES), F32),
        jax.ShapeDtypeStruct((nb, N_STATES), F32),
        jax.ShapeDtypeStruct((nb, CHUNK, CM_WIDTH), F32),
        jax.ShapeDtypeStruct((N_STATES, dec_batch), F32),
        jax.ShapeDtypeStruct((N_STATES, dec_batch), F32),
        jax.ShapeDtypeStruct((dec_batch, dec_seq, CM_WIDTH), F32),
    )
    out_specs = (pl.BlockSpec((TILE_ROWS, D_MODEL), lambda i: (i, 0)),) + tuple(
        _const_spec(o.shape) for o in out_shape[1:])
    scratch = [
        pltpu.VMEM((2, CHUNK, nb, D_MODEL), F32),
        pltpu.SemaphoreType.DMA((2, nb)),
        pltpu.VMEM((HALF_STATES // LANES, BLOCK_ROWS, LANES), F32),
        pltpu.VMEM((HALF_STATES // LANES, BLOCK_ROWS, LANES), F32),
        pltpu.VMEM((nb, N_STATES), F32),
        pltpu.VMEM((nb, N_STATES), F32),
        pltpu.VMEM((dec_batch, N_STATES), F32),
        pltpu.VMEM((dec_batch, N_STATES), F32),
        pltpu.VMEM((CM_WIDTH // LANES, BLOCK_ROWS, LANES), F32),
        pltpu.VMEM((CM_WIDTH // LANES, BLOCK_ROWS, LANES), F32),
        pltpu.VMEM((CM_WIDTH // LANES, BLOCK_ROWS, LANES), F32),
        pltpu.VMEM((TILE_ROWS, S5_WIDTH), F32),
        pltpu.VMEM((TILE_ROWS, HALF_IN), F32),
        pltpu.VMEM(w_in.shape, BF16),
        pltpu.VMEM(w_out.shape, BF16),
        pltpu.VMEM(w_glu.shape, BF16),
        pltpu.VMEM((CM_HEADS // 2, 2 * CHUNK, CHUNK), BF16),
        pltpu.SemaphoreType.DMA(((w_in.shape[1] + w_out.shape[1] + w_glu.shape[1]) // LANES,)),
        pltpu.SemaphoreType.DMA((2,)),
    ]
    return pl.pallas_call(
        functools.partial(_mixer_kernel, n_prompt, dec_seq),
        grid=(n_prompt + 1,),
        in_specs=in_specs,
        out_specs=out_specs,
        out_shape=out_shape,
        scratch_shapes=scratch,
        compiler_params=pltpu.CompilerParams(
            dimension_semantics=("arbitrary",), vmem_limit_bytes=VMEM_LIMIT_BYTES),
        name="mixer",
    )(x_prompt, x_sample, *h0, w_in, w_out, w_glu, *small_w)


def _ffn_kernel(nb, n_prompt, x_ref, g2_ref, gf_ref, wg_hbm, wu_hbm, wd_hbm, yp_hbm, ys_hbm,
                yt_ref, out_sem, wg_ref, wu_ref, wd_ref, stage_gu_ref, stage_d_ref, w_sem):
    tt = TILE_ROWS // nb
    step = pl.program_id(0)
    n_steps = pl.num_programs(0)
    slot = step % 2
    n_chunks = D_FF // FF_CHUNK

    def prompt_writeback(s, sl):
        return [pltpu.make_async_copy(yt_ref.at[sl, :, b, :], yp_hbm.at[b, pl.ds(s * tt, tt), :],
                                      out_sem.at[sl, b]) for b in range(nb)]

    def sample_writeback(sl):
        return [pltpu.make_async_copy(yt_ref.at[sl], ys_hbm, out_sem.at[sl, 0])]

    def wait_writeback(s, sl):
        @pl.when(s < n_prompt)
        def _():
            for cp in prompt_writeback(s, sl):
                cp.wait()

        @pl.when(s >= n_prompt)
        def _():
            for cp in sample_writeback(sl):
                cp.wait()

    def weight_chunk_copies(c, sl):
        cols = pl.ds(c * FF_CHUNK, FF_CHUNK)
        return [pltpu.make_async_copy(wg_hbm.at[:, cols], stage_gu_ref.at[sl, 0], w_sem.at[sl, 0]),
                pltpu.make_async_copy(wu_hbm.at[:, cols], stage_gu_ref.at[sl, 1], w_sem.at[sl, 1]),
                pltpu.make_async_copy(wd_hbm.at[cols, :], stage_d_ref.at[sl], w_sem.at[sl, 2])]

    def stage_in_chunk(c):
        sl = c % FF_STAGE_SLOTS
        cols = slice(c * FF_CHUNK, (c + 1) * FF_CHUNK)
        for cp in weight_chunk_copies(c, sl):
            cp.wait()
        wg_ref[:, cols] = stage_gu_ref[sl, 0].astype(BF16)
        wu_ref[:, cols] = stage_gu_ref[sl, 1].astype(BF16)
        wd_ref[cols, :] = stage_d_ref[sl].astype(BF16)
        if c + FF_STAGE_SLOTS < n_chunks:
            for cp in weight_chunk_copies(c + FF_STAGE_SLOTS, sl):
                cp.start()

    @pl.when(step == 0)
    def _():
        for c in range(FF_STAGE_SLOTS):
            for cp in weight_chunk_copies(c, c):
                cp.start()
        for c in range(n_chunks):
            stage_in_chunk(c)

    @pl.when(step >= 2)
    def _():
        wait_writeback(step - 2, slot)

    halves = [slice(r, r + FFN_ROWS) for r in range(0, TILE_ROWS, FFN_ROWS)]
    hs = [_rms(x_ref[rsl, :], g2_ref[...]).astype(BF16) for rsl in halves]
    accs = [x_ref[rsl, :] for rsl in halves]
    for c in range(n_chunks):
        cols = slice(c * FF_CHUNK, (c + 1) * FF_CHUNK)
        for i in range(len(halves)):
            gate = _dot(hs[i], wg_ref[:, cols])
            up = _dot(hs[i], wu_ref[:, cols])
            accs[i] = accs[i] + _dot((jax.nn.silu(gate) * up).astype(BF16), wd_ref[cols, :])
    for i, rsl in enumerate(halves):
        groups = slice(rsl.start // nb, rsl.stop // nb)
        yt_ref[slot, groups] = _rms(accs[i], gf_ref[...]).reshape(groups.stop - groups.start, nb, D_MODEL)

    @pl.when(step < n_prompt)
    def _():
        for cp in prompt_writeback(step, slot):
            cp.start()

    @pl.when(step >= n_prompt)
    def _():
        for cp in sample_writeback(slot):
            cp.start()

    @pl.when(step == n_steps - 1)
    def _():
        wait_writeback(step - 1, 1 - slot)
        wait_writeback(step, slot)


def _ffn(x1, g2, gf, ffn_w, *, nb, seq):
    n = x1.shape[0]
    n_prompt = nb * seq // TILE_ROWS
    assert n == (n_prompt + 1) * TILE_ROWS and TILE_ROWS % nb == 0 and n_prompt >= 1
    tt = TILE_ROWS // nb
    w_gate, w_up, w_down = ffn_w
    return pl.pallas_call(
        functools.partial(_ffn_kernel, nb, n_prompt),
        grid=(n_prompt + 1,),
        in_specs=[pl.BlockSpec((TILE_ROWS, D_MODEL), lambda i: (i, 0)),
                  _const_spec(g2.shape), _const_spec(gf.shape)] + [_ANY_SPEC] * 3,
        out_specs=(_ANY_SPEC, _ANY_SPEC),
        out_shape=(jax.ShapeDtypeStruct((nb, seq, D_MODEL), F32),
                   jax.ShapeDtypeStruct((tt, nb, D_MODEL), F32)),
        scratch_shapes=[pltpu.VMEM((2, tt, nb, D_MODEL), F32),
                        pltpu.SemaphoreType.DMA((2, nb)),
                        pltpu.VMEM(w_gate.shape, BF16),
                        pltpu.VMEM(w_up.shape, BF16),
                        pltpu.VMEM(w_down.shape, BF16),
                        pltpu.VMEM((FF_STAGE_SLOTS, 2, D_MODEL, FF_CHUNK), F32),
                        pltpu.VMEM((FF_STAGE_SLOTS, FF_CHUNK, D_MODEL), F32),
                        pltpu.SemaphoreType.DMA((FF_STAGE_SLOTS, 3))],
        compiler_params=pltpu.CompilerParams(
            dimension_semantics=("arbitrary",), vmem_limit_bytes=VMEM_LIMIT_BYTES),
        name="ffn",
    )(x1, g2, gf, w_gate, w_up, w_down)


def kernel(x_prompt, x_sample, state_s5_re, state_s5_im, norm1, w_in, lam_re, lam_im, log_dt, b_re, b_im, c_re, c_im, d_skip, w_glu, b_glu, cm_ln_g, cm_ln_b, w_s, b_s, g_s5, g_cm, w_out, norm2, w_gate, w_up, w_down, norm_f):
    depth = norm1.shape[0]
    assert depth == 1
    l = 0
    batch, seq, _ = x_prompt.shape
    dec_batch, dec_seq, _ = x_sample.shape

    ar, ai, bdr, bdi, cdr, cdi = pl.pallas_call(
        _prep_kernel,
        out_shape=(
            jax.ShapeDtypeStruct((1, N_STATES), F32),
            jax.ShapeDtypeStruct((1, N_STATES), F32),
            jax.ShapeDtypeStruct((S5_HALVES, HALF_IN, HALF_STATES), BF16),
            jax.ShapeDtypeStruct((S5_HALVES, HALF_IN, HALF_STATES), BF16),
            jax.ShapeDtypeStruct((S5_HALVES, HALF_STATES, HALF_IN), BF16),
            jax.ShapeDtypeStruct((S5_HALVES, HALF_STATES, HALF_IN), BF16),
        ),
        name="s5_prepare",
    )(lam_re[l], lam_im[l], log_dt[l][:, None], jnp.swapaxes(b_re, 2, 3), jnp.swapaxes(b_im, 2, 3),
      c_re[l].reshape(S5_WIDTH, S5_STATE), c_im[l].reshape(S5_WIDTH, S5_STATE))

    row = lambda a: a.reshape(1, -1)
    wmix_p = w_s[l].reshape(CM_HEADS // 2, 2 * CHUNK, CHUNK)
    per_head = jnp.concatenate([b_s[l].T, w_s[l][:, :dec_seq, :dec_seq].reshape(CM_HEADS, -1).T], axis=0)
    cmix = jnp.repeat(per_head, CM_HEAD_DIM, axis=1)
    cmix = jnp.moveaxis(cmix.reshape(-1, CM_WIDTH // LANES, LANES), 1, 0)
    small_w = [row(norm1[l]), bdr, bdi, cdr, cdi, ar, ai, row(d_skip[l]), row(b_glu[l]),
               row(cm_ln_g[l]), row(cm_ln_b[l]), row(g_s5[l]), row(g_cm[l]), wmix_p, cmix]
    to_state_major = lambda a: jnp.transpose(a, (1, 2, 0)).reshape(N_STATES, dec_batch)
    from_state_major = lambda a: jnp.transpose(a.reshape(S5_GROUPS, S5_STATE, dec_batch), (2, 0, 1))[None]
    h0 = (to_state_major(state_s5_re[l]), to_state_major(state_s5_im[l]))

    x1, pr, pi_, pv, sr, si, sv = _mixer(x_prompt, x_sample, h0, (w_in[l], w_out[l], w_glu[l]), small_w)
    y_prompt, y_sample = _ffn(x1, row(norm2[l]), row(norm_f), (w_gate[l], w_up[l], w_down[l]),
                              nb=batch, seq=seq)

    st = lambda a, n: a.reshape(1, n, S5_GROUPS, S5_STATE)
    return (y_prompt, y_sample.reshape(dec_batch, dec_seq, D_MODEL),
            st(pr, batch), st(pi_, batch), pv[None],
            from_state_major(sr), from_state_major(si), sv[None])
```
